```python
import jax
import jax.numpy as jnp
from jax import lax
import numpy as np

D_MODEL = 1024
BATCH = 8
SEQ = 2048
DEPTH = 4
DEC_BATCH = 32
DEC_SEQ = 1
PAST_LEN = 8192
PAGE_SIZE = 128

N_MIXERS = 4
N_A = len(range(0, DEPTH, N_MIXERS))
N_B = len(range(1, DEPTH, N_MIXERS))
N_C = len(range(2, DEPTH, N_MIXERS))
N_D = len(range(3, DEPTH, N_MIXERS))

HEAD_DIM = 64
ROPE_FRACTION = 4
ROPE_THETA = 500000.0
NORM_EPS = 1e-6
QBLOCK = 128
NEG_INF = -1e30
TINY = 1e-30

A_HEADS = D_MODEL // HEAD_DIM
A_KV_HEADS = A_HEADS // 4
A_CMP_STRIDE = 16
A_CMP_LEN = 2 * A_CMP_STRIDE
A_SEL_LEN = 64
A_SEL_TOPK = 16
A_WINDOW = 512
A_SEL_QBLOCK = 32
A_FORCED = 1e4
A_WIDTH = A_HEADS * HEAD_DIM
A_SPLITS = (A_WIDTH,) + (A_KV_HEADS * HEAD_DIM,) * 6 + (3 * A_HEADS, A_WIDTH)
A_IN = sum(A_SPLITS)

B_HEADS = D_MODEL // HEAD_DIM
B_KV_HEADS = B_HEADS // 4
B_IDX_HEADS = 8
B_IDX_DIM = 64
B_TOPK_MAX = 256
B_WIDTH = B_HEADS * HEAD_DIM
B_SPLITS = (B_WIDTH, B_KV_HEADS * HEAD_DIM, B_KV_HEADS * HEAD_DIM, B_IDX_HEADS * B_IDX_DIM, B_IDX_DIM, B_IDX_HEADS, B_WIDTH)
B_IN = sum(B_SPLITS)

C_GROUPS = ((128, 1), (512, 4), (2048, 16))
C_HEADS = D_MODEL // (2 * HEAD_DIM)
C_WIDTH = C_HEADS * HEAD_DIM
C_QBLOCK = 64
C_SPLITS = (C_WIDTH,) * (3 * len(C_GROUPS) + 1)
C_IN = sum(C_SPLITS)

D_HEADS = D_MODEL // HEAD_DIM
D_WIDTH = D_HEADS * HEAD_DIM
D_SPLITS = (D_WIDTH,) * 3 + (D_HEADS, D_WIDTH)
D_IN = sum(D_SPLITS)
D_FORGET_BIAS = 2.0

kernel_name = 'hybrid_nsa_dsa_dilated_fox_adaln_step'


def rms_norm(x, g):
    xf = x.astype(jnp.float32)
    y = xf * lax.rsqrt(jnp.mean(xf * xf, axis=-1, keepdims=True) + NORM_EPS)
    return (y * g.astype(jnp.float32)).astype(x.dtype)


def modulate(x, c, g, ada_w, ada_b):
    mod = (jnp.dot(jax.nn.silu(c), ada_w) + ada_b)[:, None, :]
    shift, scale, gate = jnp.split(mod, 3, axis=-1)
    return rms_norm(x, g) * (1 + scale) + shift, gate


def rope(x, pos):
    rd = x.shape[-1] // ROPE_FRACTION
    half = rd // 2
    inv = ROPE_THETA ** (-(jnp.arange(half, dtype=jnp.float32) / half))
    ang = pos.astype(jnp.float32)[:, None] * inv[None, :]
    cos, sin = jnp.cos(ang)[:, None, :], jnp.sin(ang)[:, None, :]
    xf = x.astype(jnp.float32)
    x1, x2 = xf[..., :half], xf[..., half:rd]
    return jnp.concatenate([x1 * cos - x2 * sin, x2 * cos + x1 * sin, xf[..., rd:]], axis=-1).astype(x.dtype)


def split_cols(u, sizes):
    return jnp.split(u, np.cumsum(sizes)[:-1].tolist(), axis=-1)


def heads(t, n):
    return t.reshape(t.shape[:-1] + (n, t.shape[-1] // n))


def to_blocks(x, axis, size):
    n = x.shape[axis] // size
    x = x.reshape(x.shape[:axis] + (n, size) + x.shape[axis + 1:])
    return jnp.moveaxis(x, axis, 0)


def from_blocks(x, axis):
    x = jnp.moveaxis(x, 0, axis)
    return x.reshape(x.shape[:axis] + (x.shape[axis] * x.shape[axis + 1],) + x.shape[axis + 2:])


def gather_pages(pool, layer, page_table):
    g = pool[layer, page_table]
    return g.reshape((g.shape[0], g.shape[1] * g.shape[2]) + g.shape[3:])


def pad_seq(x, mult):
    extra = -x.shape[1] % mult
    return jnp.pad(x, [(0, 0), (0, extra)] + [(0, 0)] * (x.ndim - 2))


def masked_softmax(s, mask):
    s = jnp.where(mask, s, NEG_INF)
    m = jnp.max(s, axis=-1, keepdims=True)
    e = jnp.where(mask, jnp.exp(s - m), 0.0)
    den = jnp.sum(e, axis=-1, keepdims=True)
    return e / jnp.maximum(den, TINY), m, den


def attend(q, k, v, mask, bias=None):
    B, Tq, H, dh = q.shape
    G = k.shape[2]
    qg = q.astype(jnp.float32).reshape(B, Tq, G, H // G, dh)
    s = jnp.einsum('btgrd,bsgd->bgrts', qg, k.astype(jnp.float32)) * dh ** -0.5
    if bias is not None:
        s = s + bias
    p, _, _ = masked_softmax(s, mask)
    o = jnp.einsum('bgrts,bsgd->btgrd', p, v.astype(jnp.float32)).reshape(B, Tq, H, dh)
    return o, p


def local_attend(q, qpos, k, v, kpos, window):
    d = qpos[:, None] - kpos[None, :]
    mask = (d >= 0) & (d < window) & (kpos[None, :] >= 0)
    return attend(q, k, v, mask)[0]


def banded_prompt(q, pos, k, v, window):
    pad = ((0, 0), (window, 0), (0, 0), (0, 0))
    kp, vp = jnp.pad(k, pad), jnp.pad(v, pad)
    span = QBLOCK + window

    def body(a):
        qb, qpos = a
        start = qpos[0]
        kb = lax.dynamic_slice_in_dim(kp, start, span, axis=1)
        vb = lax.dynamic_slice_in_dim(vp, start, span, axis=1)
        return local_attend(qb, qpos, kb, vb, start - window + jnp.arange(span), window)

    return from_blocks(lax.map(body, (to_blocks(q, 1, QBLOCK), pos.reshape(-1, QBLOCK))), 1)


def gated_out(o, z, w_out):
    B, T = o.shape[:2]
    y = o.reshape(B, T, -1) * jax.nn.silu(z.astype(jnp.float32))
    return jnp.dot(y.astype(w_out.dtype), w_out)


def nsa_project(h, pos, w_in):
    q, kc, vc, ks, vs, kw, vw, gl, z = split_cols(jnp.dot(h, w_in), A_SPLITS)
    q = heads(q, A_HEADS)
    kc, vc, ks, vs, kw, vw = [heads(t, A_KV_HEADS) for t in (kc, vc, ks, vs, kw, vw)]
    gates = jax.nn.sigmoid(heads(gl, A_HEADS).astype(jnp.float32))
    return dict(q=q, q_rot=rope(q, pos), kc=kc, vc=vc, ks=rope(ks, pos), vs=vs,
                kw=rope(kw, pos), vw=vw, gates=gates, z=z)


def nsa_compressed(q, qpos, kc, vc, wk, wv):
    B, L, G, dh = kc.shape
    S = A_CMP_STRIDE

    def compress(x, w):
        ch = x.astype(jnp.float32).reshape(B, L // S, S, G, dh)
        lo = jnp.einsum('bnigd,id->bngd', ch, w[:S].astype(jnp.float32))
        hi = jnp.einsum('bnigd,id->bngd', ch, w[S:].astype(jnp.float32))
        return lo[:, :-1] + hi[:, 1:]

    kb, vb = compress(kc, wk), compress(vc, wv)
    ends = jnp.arange(kb.shape[1]) * S + A_CMP_LEN - 1
    return attend(q, kb, vb, ends[None, :] <= qpos[:, None])


def nsa_select_blocks(p_cmp, qpos, n_blocks):
    pg = jnp.sum(p_cmp, axis=2)
    ncb = pg.shape[-1]
    ratio = A_SEL_LEN // A_CMP_STRIDE
    span = ratio + A_CMP_LEN // A_CMP_STRIDE - 1
    j = jnp.arange(n_blocks)
    ci = j[:, None] * ratio - (A_CMP_LEN // A_CMP_STRIDE - 1) + jnp.arange(span)[None, :]
    ok = (ci >= 0) & (ci < ncb)
    imp = jnp.sum(jnp.where(ok, pg[..., jnp.clip(ci, 0, ncb - 1)], 0.0), axis=-1)
    qb = qpos // A_SEL_LEN
    forced = (j[None, :] == 0) | (j[None, :] == qb[:, None])
    allowed = j[None, :] <= qb[:, None]
    score = jnp.where(allowed, jnp.where(forced, A_FORCED, imp), -1.0)
    top, idx = lax.top_k(score, min(A_SEL_TOPK, n_blocks))
    return idx, top >= 0.0


def to_sel_blocks(k):
    B, L, G, dh = k.shape
    return k.reshape(B, L // A_SEL_LEN, A_SEL_LEN, G, dh).transpose(0, 3, 1, 2, 4)


def nsa_selected(q, qpos, idx, valid, ksb, vsb):
    B, Tq, H, dh = q.shape
    G, S = ksb.shape[1], ksb.shape[3]
    n = idx.shape[-1]
    take = jax.vmap(jax.vmap(lambda blk, ix: blk[ix]))
    kg = take(ksb, idx).reshape(B, G, Tq, n * S, dh)
    vg = take(vsb, idx).reshape(B, G, Tq, n * S, dh)
    kpos = (idx[..., None] * S + jnp.arange(S)).reshape(B, G, Tq, n * S)
    mask = (jnp.repeat(valid, S, axis=-1) & (kpos <= qpos[:, None]))[:, :, None]
    qg = q.astype(jnp.float32).reshape(B, Tq, G, H // G, dh)
    s = jnp.einsum('btgrd,bgtkd->bgrtk', qg, kg.astype(jnp.float32)) * dh ** -0.5
    p, _, _ = masked_softmax(s, mask)
    return jnp.einsum('bgrtk,bgtkd->btgrd', p, vg.astype(jnp.float32)).reshape(B, Tq, H, dh)


def nsa_merge(pr, o_cmp, o_sel, o_win, w_out):
    g = pr['gates']
    o = g[..., 0:1] * o_cmp + g[..., 1:2] * o_sel + g[..., 2:3] * o_win
    return gated_out(o, pr['z'], w_out)


def nsa_prompt(pr, pos, cmp_wk, cmp_wv, w_out):
    T = pos.shape[0]
    o_cmp, p_cmp = nsa_compressed(pr['q'], pos, pr['kc'], pr['vc'], cmp_wk, cmp_wv)
    idx, valid = nsa_select_blocks(p_cmp, pos, T // A_SEL_LEN)
    ksb, vsb = to_sel_blocks(pr['ks']), to_sel_blocks(pr['vs'])
    o_sel = from_blocks(lax.map(lambda a: nsa_selected(a[0], a[1], a[2], a[3], ksb, vsb),
                                (to_blocks(pr['q_rot'], 1, A_SEL_QBLOCK), pos.reshape(-1, A_SEL_QBLOCK),
                                 to_blocks(idx, 2, A_SEL_QBLOCK), to_blocks(valid, 2, A_SEL_QBLOCK))), 1)
    o_win = banded_prompt(pr['q_rot'], pos, pr['kw'], pr['vw'], A_WINDOW)
    return nsa_merge(pr, o_cmp, o_sel, o_win, w_out)


def nsa_sample(pr, pos, cmp_past, sel_past, win_buf, cmp_wk, cmp_wv, w_out):
    kc = pad_seq(jnp.concatenate([cmp_past[:, :, 0], pr['kc']], axis=1), A_SEL_LEN)
    vc = pad_seq(jnp.concatenate([cmp_past[:, :, 1], pr['vc']], axis=1), A_SEL_LEN)
    ks = pad_seq(jnp.concatenate([sel_past[:, :, 0], pr['ks']], axis=1), A_SEL_LEN)
    vs = pad_seq(jnp.concatenate([sel_past[:, :, 1], pr['vs']], axis=1), A_SEL_LEN)
    o_cmp, p_cmp = nsa_compressed(pr['q'], pos, kc, vc, cmp_wk, cmp_wv)
    idx, valid = nsa_select_blocks(p_cmp, pos, kc.shape[1] // A_SEL_LEN)
    o_sel = nsa_selected(pr['q_rot'], pos, idx, valid, to_sel_blocks(ks), to_sel_blocks(vs))
    wb = win_buf.shape[1]
    win_all = jnp.concatenate([win_buf, jnp.stack([pr['kw'], pr['vw']], axis=2).astype(win_buf.dtype)], axis=1)
    kpos = PAST_LEN - wb + jnp.arange(win_all.shape[1])
    o_win = local_attend(pr['q_rot'], pos, win_all[:, :, 0], win_all[:, :, 1], kpos, A_WINDOW)
    return nsa_merge(pr, o_cmp, o_sel, o_win, w_out), win_all[:, -wb:]


def dsa_project(h, pos, w_in):
    q, k, v, qi, ki, wi, z = split_cols(jnp.dot(h, w_in), B_SPLITS)
    return dict(q=rope(heads(q, B_HEADS), pos), k=rope(heads(k, B_KV_HEADS), pos), v=heads(v, B_KV_HEADS),
                qi=rope(heads(qi, B_IDX_HEADS), pos), ki=rope(heads(ki, 1), pos)[:, :, 0],
                wi=wi.astype(jnp.float32) * (B_IDX_HEADS * B_IDX_DIM) ** -0.5, z=z)


def dsa_core(q, qi, wi, qpos, k, v, ki, kpos, topk):
    B, Tq, H, dh = q.shape
    G = k.shape[2]
    logits = jnp.einsum('bthd,bsd->bths', qi.astype(jnp.float32), ki.astype(jnp.float32))
    score = jnp.einsum('bth,bths->bts', wi, jax.nn.relu(logits))
    score = jnp.where(kpos[None, :] <= qpos[:, None], score, NEG_INF)
    _, idx = lax.top_k(score, topk)
    valid = kpos[idx] <= qpos[None, :, None]
    take = jax.vmap(lambda rows, ix: rows[ix])
    kg, vg = take(k, idx), take(v, idx)
    qg = q.astype(jnp.float32).reshape(B, Tq, G, H // G, dh)
    s = jnp.einsum('btgrd,btkgd->bgrtk', qg, kg.astype(jnp.float32)) * dh ** -0.5
    p, _, _ = masked_softmax(s, valid[:, None, None])
    return jnp.einsum('bgrtk,btkgd->btgrd', p, vg.astype(jnp.float32)).reshape(B, Tq, H, dh)


def dsa_prompt(pr, pos, topk, w_out):
    def body(a):
        return dsa_core(a[0], a[1], a[2], a[3], pr['k'], pr['v'], pr['ki'], pos, topk)
    o = from_blocks(lax.map(body, (to_blocks(pr['q'], 1, QBLOCK), to_blocks(pr['qi'], 1, QBLOCK),
                                   to_blocks(pr['wi'], 1, QBLOCK), pos.reshape(-1, QBLOCK))), 1)
    return gated_out(o, pr['z'], w_out)


def dsa_sample(pr, pos, kv_past, kidx_past, topk, w_out):
    k = jnp.concatenate([kv_past[:, :, 0], pr['k']], axis=1)
    v = jnp.concatenate([kv_past[:, :, 1], pr['v']], axis=1)
    ki = jnp.concatenate([kidx_past, pr['ki']], axis=1)
    o = dsa_core(pr['q'], pr['qi'], pr['wi'], pos, k, v, ki, jnp.arange(k.shape[1]), topk)
    return gated_out(o, pr['z'], w_out)


def dil_project(h, pos, w_in):
    parts = split_cols(jnp.dot(h, w_in), C_SPLITS)
    ng = len(C_GROUPS)
    return dict(q=[rope(heads(parts[3 * g], C_HEADS), pos) for g in range(ng)],
                k=[rope(heads(parts[3 * g + 1], C_HEADS), pos) for g in range(ng)],
                v=[heads(parts[3 * g + 2], C_HEADS) for g in range(ng)], z=parts[-1])


def dilated_group(q, qpos, kbuf, vbuf, base, dil, n_keys):
    B, Tq, H, dh = q.shape
    dist = jnp.arange(n_keys) * dil
    ix = jnp.clip(base + jnp.arange(Tq)[:, None] - dist[None, :], 0, kbuf.shape[1] - 1)
    valid = (qpos[:, None] - dist[None, :]) >= 0
    kg, vg = kbuf[:, ix], vbuf[:, ix]
    s = jnp.einsum('bthd,btjhd->bhtj', q.astype(jnp.float32), kg.astype(jnp.float32)) * dh ** -0.5
    p, m, den = masked_softmax(s, valid)
    o = jnp.einsum('bhtj,btjhd->bthd', p, vg.astype(jnp.float32))
    return o, m[..., 0], den[..., 0]


def dilated_mix(qs, qpos, kbufs, vbufs, bases):
    res = [dilated_group(q, qpos, kb, vb, base, r, w // r + 1)
           for (w, r), q, kb, vb, base in zip(C_GROUPS, qs, kbufs, vbufs, bases)]
    o = jnp.stack([t[0] for t in res])
    m = jnp.stack([t[1] for t in res])
    den = jnp.stack([t[2] for t in res])
    wts = den * jnp.exp(m - jnp.max(m, axis=0, keepdims=True))
    alpha = wts / jnp.sum(wts, axis=0, keepdims=True)
    return jnp.einsum('gbht,gbthd->bthd', alpha, o)


def dil_prompt(pr, pos, w_out):
    kpads = [jnp.pad(k, ((0, 0), (w, 0), (0, 0), (0, 0))) for (w, _), k in zip(C_GROUPS, pr['k'])]
    vpads = [jnp.pad(v, ((0, 0), (w, 0), (0, 0), (0, 0))) for (w, _), v in zip(C_GROUPS, pr['v'])]

    def body(a):
        qpos, qs = a
        start = qpos[0]
        kb = [lax.dynamic_slice_in_dim(kp, start, C_QBLOCK + w, axis=1) for (w, _), kp in zip(C_GROUPS, kpads)]
        vb = [lax.dynamic_slice_in_dim(vp, start, C_QBLOCK + w, axis=1) for (w, _), vp in zip(C_GROUPS, vpads)]
        return dilated_mix(qs, qpos, kb, vb, [w for w, _ in C_GROUPS])

    o = from_blocks(lax.map(body, (pos.reshape(-1, C_QBLOCK), [to_blocks(q, 1, C_QBLOCK) for q in pr['q']])), 1)
    return gated_out(o, pr['z'], w_out)


def dil_sample(pr, pos, bufs, w_out):
    alls = [jnp.concatenate([buf, jnp.stack([k, v], axis=2).astype(buf.dtype)], axis=1)
            for buf, k, v in zip(bufs, pr['k'], pr['v'])]
    o = dilated_mix(pr['q'], pos, [a[:, :, 0] for a in alls], [a[:, :, 1] for a in alls], [b.shape[1] for b in bufs])
    return gated_out(o, pr['z'], w_out), [a[:, -b.shape[1]:] for a, b in zip(alls, bufs)]


def fox_project(h, w_in, f_bias):
    q, k, v, f, z = split_cols(jnp.dot(h, w_in), D_SPLITS)
    logf = jax.nn.log_sigmoid(f.astype(jnp.float32) + f_bias.astype(jnp.float32))
    return dict(q=heads(q, D_HEADS), k=heads(k, D_HEADS), v=heads(v, D_HEADS), logf=logf, z=z)


def fox_core(q, qpos, cq, k, v, kpos, ck):
    bias = (jnp.swapaxes(cq, 1, 2)[..., :, None] - jnp.swapaxes(ck, 1, 2)[..., None, :])[:, :, None]
    return attend(q, k, v, kpos[None, :] <= qpos[:, None], bias)[0]


def fox_prompt(pr, pos, w_out):
    c = jnp.cumsum(pr['logf'], axis=1)
    o = from_blocks(lax.map(lambda a: fox_core(a[0], a[1], a[2], pr['k'], pr['v'], pos, c),
                            (to_blocks(pr['q'], 1, QBLOCK), pos.reshape(-1, QBLOCK), to_blocks(c, 1, QBLOCK))), 1)
    return gated_out(o, pr['z'], w_out)


def fox_sample(pr, pos, kv_past, logf_past, w_out):
    k = jnp.concatenate([kv_past[:, :, 0], pr['k']], axis=1)
    v = jnp.concatenate([kv_past[:, :, 1], pr['v']], axis=1)
    c = jnp.cumsum(jnp.concatenate([logf_past.astype(jnp.float32), pr['logf']], axis=1), axis=1)
    o = fox_core(pr['q'], pos, c[:, PAST_LEN:], k, v, jnp.arange(k.shape[1]), c)
    return gated_out(o, pr['z'], w_out)


def setup_inputs(seed: int = 0) -> dict:
    key = jax.random.key(seed)
    keys = iter(jax.random.split(key, 40))

    def nrm(shape, scale=1.0):
        return scale * jax.random.normal(next(keys), shape, jnp.float32)

    n_pages = PAST_LEN // PAGE_SIZE
    n_used = DEC_BATCH * n_pages
    n_pool = n_used + (n_used + 3) // 4
    page_table = jax.random.permutation(next(keys), n_pool)[:n_used].reshape(DEC_BATCH, n_pages).astype(jnp.int32)
    wb = lambda w: min(w, PAST_LEN)
    ds = D_MODEL ** -0.5
    return {
        'x_prompt': nrm((BATCH, SEQ, D_MODEL)),
        'x_sample': nrm((DEC_BATCH, DEC_SEQ, D_MODEL)),
        'cache_a_cmp': nrm((N_A, n_pool, PAGE_SIZE, 2, A_KV_HEADS, HEAD_DIM)),
        'cache_a_sel': nrm((N_A, n_pool, PAGE_SIZE, 2, A_KV_HEADS, HEAD_DIM)),
        'cache_a_win': nrm((N_A, DEC_BATCH, wb(A_WINDOW), 2, A_KV_HEADS, HEAD_DIM)),
        'cache_b_kv': nrm((N_B, n_pool, PAGE_SIZE, 2, B_KV_HEADS, HEAD_DIM)),
        'cache_b_kidx': nrm((N_B, n_pool, PAGE_SIZE, B_IDX_DIM)),
        'cache_c_win0': nrm((N_C, DEC_BATCH, wb(C_GROUPS[0][0]), 2, C_HEADS, HEAD_DIM)),
        'cache_c_win1': nrm((N_C, DEC_BATCH, wb(C_GROUPS[1][0]), 2, C_HEADS, HEAD_DIM)),
        'cache_c_win2': nrm((N_C, DEC_BATCH, wb(C_GROUPS[2][0]), 2, C_HEADS, HEAD_DIM)),
        'cache_d_kv': nrm((N_D, n_pool, PAGE_SIZE, 2, D_HEADS, HEAD_DIM)),
        'cache_d_logf': jax.nn.log_sigmoid(D_FORGET_BIAS + nrm((N_D, n_pool, PAGE_SIZE, D_HEADS))),
        'page_table': page_table,
        'c_prompt': nrm((BATCH, D_MODEL)),
        'c_sample': nrm((DEC_BATCH, D_MODEL)),
        'ada_w': nrm((DEPTH, D_MODEL, 3 * D_MODEL), 0.5 * ds),
        'ada_b': nrm((DEPTH, 3 * D_MODEL), 0.02) + jnp.concatenate([jnp.zeros((2 * D_MODEL,), jnp.float32), jnp.ones((D_MODEL,), jnp.float32)]),
        'norm_g': 1.0 + nrm((DEPTH, D_MODEL), 0.02),
        'final_g': 1.0 + nrm((D_MODEL,), 0.02),
        'a_w_in': nrm((N_A, D_MODEL, A_IN), ds),
        'a_w_out': nrm((N_A, A_WIDTH, D_MODEL), A_WIDTH ** -0.5),
        'a_cmp_wk': (1.0 + nrm((N_A, A_CMP_LEN, HEAD_DIM), 0.1)) * A_CMP_LEN ** -0.5,
        'a_cmp_wv': (1.0 + nrm((N_A, A_CMP_LEN, HEAD_DIM), 0.1)) * A_CMP_LEN ** -0.5,
        'b_w_in': nrm((N_B, D_MODEL, B_IN), ds),
        'b_w_out': nrm((N_B, B_WIDTH, D_MODEL), B_WIDTH ** -0.5),
        'c_w_in': nrm((N_C, D_MODEL, C_IN), ds),
        'c_w_out': nrm((N_C, C_WIDTH, D_MODEL), C_WIDTH ** -0.5),
        'd_w_in': nrm((N_D, D_MODEL, D_IN), ds),
        'd_w_out': nrm((N_D, D_WIDTH, D_MODEL), D_WIDTH ** -0.5),
        'd_f_bias': D_FORGET_BIAS + nrm((N_D, D_HEADS), 0.5),
    }


def reference(x_prompt, x_sample, cache_a_cmp, cache_a_sel, cache_a_win, cache_b_kv, cache_b_kidx,
              cache_c_win0, cache_c_win1, cache_c_win2, cache_d_kv, cache_d_logf, page_table,
              c_prompt, c_sample, ada_w, ada_b, norm_g, final_g, a_w_in, a_w_out, a_cmp_wk, a_cmp_wv,
              b_w_in, b_w_out, c_w_in, c_w_out, d_w_in, d_w_out, d_f_bias):
    pos_p = jnp.arange(SEQ, dtype=jnp.int32)
    pos_s = PAST_LEN + jnp.arange(DEC_SEQ, dtype=jnp.int32)
    c_bufs = (cache_c_win0, cache_c_win1, cache_c_win2)
    st = {n: [] for n in ('a_cmp_p', 'a_cmp_s', 'a_sel_p', 'a_sel_s', 'a_win_p', 'a_win_s',
                          'b_kv_p', 'b_kv_s', 'b_kidx_p', 'b_kidx_s',
                          'c_win0_p', 'c_win0_s', 'c_win1_p', 'c_win1_s', 'c_win2_p', 'c_win2_s',
                          'd_kv_p', 'd_kv_s', 'd_logf_p', 'd_logf_s')}
    xp, xs = x_prompt, x_sample
    for i in range(DEPTH):
        kind, li = i % N_MIXERS, i // N_MIXERS
        hp, gp = modulate(xp, c_prompt, norm_g[i], ada_w[i], ada_b[i])
        hs, gs = modulate(xs, c_sample, norm_g[i], ada_w[i], ada_b[i])
        if kind == 0:
            pp, ps = nsa_project(hp, pos_p, a_w_in[li]), nsa_project(hs, pos_s, a_w_in[li])
            op = nsa_prompt(pp, pos_p, a_cmp_wk[li], a_cmp_wv[li], a_w_out[li])
            os_, win_s = nsa_sample(ps, pos_s, gather_pages(cache_a_cmp, li, page_table),
                                    gather_pages(cache_a_sel, li, page_table), cache_a_win[li],
                                    a_cmp_wk[li], a_cmp_wv[li], a_w_out[li])
            st['a_cmp_p'].append(jnp.stack([pp['kc'], pp['vc']], axis=2))
            st['a_cmp_s'].append(jnp.stack([ps['kc'], ps['vc']], axis=2))
            st['a_sel_p'].append(jnp.stack([pp['ks'], pp['vs']], axis=2))
            st['a_sel_s'].append(jnp.stack([ps['ks'], ps['vs']], axis=2))
            st['a_win_p'].append(jnp.stack([pp['kw'], pp['vw']], axis=2)[:, -min(A_WINDOW, SEQ):])
            st['a_win_s'].append(win_s)
        elif kind == 1:
            pp, ps = dsa_project(hp, pos_p, b_w_in[li]), dsa_project(hs, pos_s, b_w_in[li])
            op = dsa_prompt(pp, pos_p, min(B_TOPK_MAX, SEQ // 4), b_w_out[li])
            os_ = dsa_sample(ps, pos_s, gather_pages(cache_b_kv, li, page_table),
                             gather_pages(cache_b_kidx, li, page_table),
                             min(B_TOPK_MAX, (PAST_LEN + DEC_SEQ) // 4), b_w_out[li])
            st['b_kv_p'].append(jnp.stack([pp['k'], pp['v']], axis=2))
            st['b_kv_s'].append(jnp.stack([ps['k'], ps['v']], axis=2))
            st['b_kidx_p'].append(pp['ki'])
            st['b_kidx_s'].append(ps['ki'])
        elif kind == 2:
            pp, ps = dil_project(hp, pos_p, c_w_in[li]), dil_project(hs, pos_s, c_w_in[li])
            op = dil_prompt(pp, pos_p, c_w_out[li])
            os_, bufs = dil_sample(ps, pos_s, [b[li] for b in c_bufs], c_w_out[li])
            for g, (w, _) in enumerate(C_GROUPS):
                st['c_win%d_p' % g].append(jnp.stack([pp['k'][g], pp['v'][g]], axis=2)[:, -min(w, SEQ):])
                st['c_win%d_s' % g].append(bufs[g])
        else:
            pp, ps = fox_project(hp, d_w_in[li], d_f_bias[li]), fox_project(hs, d_w_in[li], d_f_bias[li])
            op = fox_prompt(pp, pos_p, d_w_out[li])
            os_ = fox_sample(ps, pos_s, gather_pages(cache_d_kv, li, page_table),
                             gather_pages(cache_d_logf, li, page_table), d_w_out[li])
            st['d_kv_p'].append(jnp.stack([pp['k'], pp['v']], axis=2))
            st['d_kv_s'].append(jnp.stack([ps['k'], ps['v']], axis=2))
            st['d_logf_p'].append(pp['logf'].astype(hp.dtype))
            st['d_logf_s'].append(ps['logf'].astype(hs.dtype))
        xp = xp + gp * op
        xs = xs + gs * os_
    y_prompt = rms_norm(xp, final_g)
    y_sample = rms_norm(xs, final_g)
    ns = {n: jnp.stack(v) for n, v in st.items()}
    return (y_prompt, y_sample,
            ns['a_cmp_p'], ns['a_cmp_s'], ns['a_sel_p'], ns['a_sel_s'], ns['a_win_p'], ns['a_win_s'],
            ns['b_kv_p'], ns['b_kv_s'], ns['b_kidx_p'], ns['b_kidx_s'],
            ns['c_win0_p'], ns['c_win0_s'], ns['c_win1_p'], ns['c_win1_s'], ns['c_win2_p'], ns['c_win2_s'],
            ns['d_kv_p'], ns['d_kv_s'], ns['d_logf_p'], ns['d_logf_s'])
```

```python
import functools
import math

import numpy as np
import jax
import jax.numpy as jnp
from jax import lax
from jax.experimental import pallas as pl
from jax.experimental.pallas import tpu as pltpu

F32 = jnp.float32
BF16 = jnp.bfloat16
I32 = jnp.int32

HEAD_DIM = 64
ROPE_DIMS = HEAD_DIM // 4
ROPE_THETA = 500000.0
NORM_EPS = 1e-6
NEG_INF = -1e30
TINY = 1e-30
PAGE = 128
LANES = 128
INT_MIN = -2 ** 31

A_KV = 4
A_CMP_STRIDE = 16
A_CMP_LEN = 32
A_SEL_LEN = 64
A_SEL_TOPK = 16
A_WINDOW = 512
A_FORCED = 1e4
B_IDX_HEADS = 8
B_IDX_DIM = 64
B_TOPK_MAX = 256
C_GROUPS = ((128, 1), (512, 4), (2048, 16))
C_HEADS = 8
QK_SCALE = HEAD_DIM ** -0.5

NN = ((1,), (0,))
NT = ((1,), (1,))

VMEM_LIMIT = 56 * 1024 * 1024


def _cp(*sem):
    return pltpu.CompilerParams(dimension_semantics=sem, vmem_limit_bytes=VMEM_LIMIT)


def _dotf(a, b, dims=NN):
    return lax.dot_general(a, b, (dims, ((), ())), preferred_element_type=F32)


def _split(a, terms):
    out = []
    for _ in range(terms - 1):
        h = a.astype(BF16)
        out.append(h)
        a = a - h.astype(F32)
    out.append(a.astype(BF16))
    return out


def _dot3(a, b, dims=NN):
    ah, al = _split(a, 2)
    bh, bl = _split(b, 2)
    return _dotf(ah, bh, dims) + (_dotf(ah, bl, dims) + _dotf(al, bh, dims))


def _dotp(a, b, dims, passes):
    if passes == 1:
        return _dotf(a.astype(BF16), b.astype(BF16), dims)
    return _dot3(a, b, dims)


def _dotx(a, m01, terms, dims=NN):
    acc = None
    for t in _split(a, terms):
        d = _dotf(t, m01, dims)
        acc = d if acc is None else acc + d
    return acc


def _xdot(m01, b, terms, dims=NN):
    acc = None
    for t in _split(b, terms):
        d = _dotf(m01, t, dims)
        acc = d if acc is None else acc + d
    return acc


def _iota(shape, axis):
    return lax.broadcasted_iota(I32, shape, axis)


def _onehot(cond):
    return jnp.where(cond, 1.0, 0.0).astype(BF16)


def _sigmoid(x):
    return 1.0 / (1.0 + jnp.exp(-x))


def _head_to_lane(n_lanes, n_cols):
    return _onehot((_iota((n_lanes, n_cols), 0) >> 6) == _iota((n_lanes, n_cols), 1))


def _lane_to_head(n_cols, n_lanes):
    return _onehot(_iota((n_cols, n_lanes), 0) == (_iota((n_cols, n_lanes), 1) >> 6))


def _sortable_key(x):
    x = jnp.where(x == 0.0, 0.0, x)
    b = lax.bitcast_convert_type(x, I32)
    return jnp.where(b < 0, b ^ 0x7FFFFFFF, b)


def _count_ge(key, cand):
    return jnp.sum(jnp.where(key >= cand, 1.0, 0.0), axis=-1, keepdims=True)


def _kth_largest_key(key, k):
    base = jnp.where(_count_ge(key, 0) >= k, 0, INT_MIN).astype(I32)

    def body(it, base):
        cand = base | jnp.left_shift(jnp.int32(1), 30 - it)
        return jnp.where(_count_ge(key, cand) >= k, cand, base)

    return lax.fori_loop(0, 31, body, base)


def _topk_mask(key, k):
    n = key.shape[1]
    thr = _kth_largest_key(key, k)
    gt = key > thr
    eq = key == thr
    need = k - jnp.sum(jnp.where(gt, 1.0, 0.0), axis=-1, keepdims=True)
    before = _onehot(_iota((LANES, LANES), 0) < _iota((LANES, LANES), 1))
    run = jnp.zeros_like(need)
    out = []
    for c in range(n // LANES):
        sl = slice(c * LANES, (c + 1) * LANES)
        eqf = jnp.where(eq[:, sl], 1.0, 0.0)
        prior = _dotf(eqf.astype(BF16), before) + run
        out.append(gt[:, sl] | (eq[:, sl] & (prior < need)))
        run = run + jnp.sum(eqf, axis=-1, keepdims=True)
    return jnp.concatenate(out, axis=1)


def _flash(q, k_ref, v_ref, klane, vlane, c_lo, c_hi, tk, mask_fn, passes, bias_fn=None):
    M = q.shape[0]

    def body(c, carry):
        m, l, acc = carry
        off = pl.multiple_of(c * tk, tk)
        k = k_ref[0, pl.ds(off, tk), klane:klane + HEAD_DIM]
        v = v_ref[0, pl.ds(off, tk), vlane:vlane + HEAD_DIM]
        s = _dotp(q, k, NT, passes)
        if bias_fn is not None:
            s = s + bias_fn(c)
        mask = mask_fn(c)
        s = jnp.where(mask, s, NEG_INF)
        m_new = jnp.maximum(m, jnp.max(s, axis=-1, keepdims=True))
        alpha = jnp.exp(m - m_new)
        e = jnp.where(mask, jnp.exp(s - m_new), 0.0)
        l = alpha * l + jnp.sum(e, axis=-1, keepdims=True)
        acc = alpha * acc + _dotp(e, v, NN, passes)
        return m_new, l, acc

    init = (jnp.full((M, 1), NEG_INF, F32), jnp.zeros((M, 1), F32), jnp.zeros((M, HEAD_DIM), F32))
    m, l, acc = lax.fori_loop(c_lo, c_hi, body, init)
    return acc / jnp.maximum(l, TINY), m, l


def _stack_heads(ref, first_head, n):
    return jnp.concatenate(
        [ref[0, :, (first_head + r) * HEAD_DIM:(first_head + r + 1) * HEAD_DIM] for r in range(n)], axis=0)


def _mod_kernel(c_ref, w_ref, b_ref, o_ref):
    c = c_ref[...]
    o_ref[0] = _dot3(c * _sigmoid(c), w_ref[0]) + b_ref[0]


def _mod_all(c_all, ada_w, ada_b):
    L, D, D3 = ada_w.shape
    NC = c_all.shape[0]
    tn = 1024
    return pl.pallas_call(
        _mod_kernel, grid=(L, D3 // tn),
        in_specs=[pl.BlockSpec((NC, D), lambda l, j: (0, 0)),
                  pl.BlockSpec((1, D, tn), lambda l, j: (l, 0, j)),
                  pl.BlockSpec((1, 1, tn), lambda l, j: (l, 0, j))],
        out_specs=pl.BlockSpec((1, NC, tn), lambda l, j: (l, 0, j)),
        out_shape=jax.ShapeDtypeStruct((L, NC, D3), F32),
        compiler_params=_cp("parallel", "parallel"), name="adaln_mod",
    )(c_all, ada_w, ada_b.reshape(L, 1, D3))


def _proj_kernel(x_ref, sc_ref, sh_ref, g_ref, wh_ref, wl_ref, cos_ref, sn_ref, sp_ref, *out_refs, segs):
    x = x_ref[...]
    y = x * lax.rsqrt(jnp.mean(x * x, axis=-1, keepdims=True) + NORM_EPS) * g_ref[...]
    h = y * (1.0 + sc_ref[0]) + sh_ref[0]
    hh, hl = _split(h, 2)
    cos, sn, sp = cos_ref[...], sn_ref[...], sp_ref[...]
    oi = 0
    for start, width, modes in segs:
        wh = wh_ref[:, start:start + width]
        wl = wl_ref[:, start:start + width]
        u = _dotf(hh, wh) + (_dotf(hh, wl) + _dotf(hl, wh))
        for mode in modes:
            o_ref = out_refs[oi]
            oi += 1
            n_rope = {"n": 0, "r": width, "rk": width // 2}[mode]
            for c in range(width // LANES):
                sl = slice(c * LANES, (c + 1) * LANES)
                uc = u[:, sl]
                if c * LANES < n_rope:
                    uc = uc * cos + pltpu.roll(uc, LANES - ROPE_DIMS // 2, 1) * sn + pltpu.roll(uc, ROPE_DIMS // 2, 1) * sp
                o_ref[:, sl] = uc


def _proj(x2, scale, shift, g, w, segs, tables, tiles_per_batch, tm):
    R, D = x2.shape
    RB = scale.shape[1]
    wh = w.astype(BF16)
    wl = (w - wh.astype(F32)).astype(BF16)
    groups, cur, cur_w = [], [], 0
    for seg in segs:
        if cur and cur_w + seg[1] > 2048:
            groups.append(cur)
            cur, cur_w = [], 0
        cur.append(seg)
        cur_w += seg[1]
    groups.append(cur)
    outs = []
    for grp in groups:
        c0 = grp[0][0]
        c1 = grp[-1][0] + grp[-1][1]
        local = tuple((s - c0, wd, modes) for s, wd, modes in grp)
        out_shapes, out_specs = [], []
        for s, wd, modes in grp:
            for _ in modes:
                out_shapes.append(jax.ShapeDtypeStruct((R, wd), F32))
                out_specs.append(pl.BlockSpec((tm, wd), lambda i: (i, 0)))
        tab_spec = pl.BlockSpec((tm, LANES), lambda i: (i % tiles_per_batch, 0))
        mod_spec = pl.BlockSpec((1, RB, D), lambda i: (i // tiles_per_batch, 0, 0))
        res = pl.pallas_call(
            functools.partial(_proj_kernel, segs=local), grid=(R // tm,),
            in_specs=[pl.BlockSpec((tm, D), lambda i: (i, 0)), mod_spec, mod_spec,
                      pl.BlockSpec((1, D), lambda i: (0, 0)),
                      pl.BlockSpec((D, c1 - c0), lambda i: (0, 0)),
                      pl.BlockSpec((D, c1 - c0), lambda i: (0, 0)),
                      tab_spec, tab_spec, tab_spec],
            out_specs=out_specs, out_shape=out_shapes,
            compiler_params=_cp("parallel"), name="norm_mod_proj",
        )(x2, scale, shift, g.reshape(1, D), wh[:, c0:c1], wl[:, c0:c1], *tables)
        outs.extend(res)
    return outs


def _rope_tables(pos):
    half = ROPE_DIMS // 2
    inv = ROPE_THETA ** (-(jnp.arange(half, dtype=F32) / half))
    ang = pos.astype(F32)[:, None] * inv[None, :]
    cos, sin = jnp.cos(ang), jnp.sin(ang)
    R = pos.shape[0]
    one = jnp.ones((R, HEAD_DIM - ROPE_DIMS), F32)
    zero = jnp.zeros((R, HEAD_DIM - ROPE_DIMS), F32)
    zh = jnp.zeros((R, half), F32)
    cos_h = jnp.concatenate([cos, cos, one], axis=1)
    sn_h = jnp.concatenate([-sin, zh, zero], axis=1)
    sp_h = jnp.concatenate([zh, sin, zero], axis=1)
    rep = LANES // HEAD_DIM
    return tuple(jnp.tile(t, (1, rep)) for t in (cos_h, sn_h, sp_h))


def _out_kernel(o_ref, z_ref, wh_ref, wl_ref, x_ref, gate_ref, *rest, final):
    z = z_ref[...]
    y = o_ref[...] * (z * _sigmoid(z))
    yh, yl = _split(y, 2)
    wh, wl = wh_ref[...], wl_ref[...]
    r = _dotf(yh, wh) + (_dotf(yh, wl) + _dotf(yl, wh))
    xn = x_ref[...] + gate_ref[0] * r
    if final:
        fg_ref, xo_ref, yo_ref = rest
        yo_ref[...] = xn * lax.rsqrt(jnp.mean(xn * xn, axis=-1, keepdims=True) + NORM_EPS) * fg_ref[...]
    else:
        (xo_ref,) = rest
    xo_ref[...] = xn


def _gated_out(o2, z2, w_out, x2, gate, tiles_per_batch, tm, final_g=None):
    R, W = o2.shape
    D = x2.shape[1]
    RB = gate.shape[1]
    wh = w_out.astype(BF16)
    wl = (w_out - wh.astype(F32)).astype(BF16)
    final = final_g is not None
    in_specs = [pl.BlockSpec((tm, W), lambda i: (i, 0)), pl.BlockSpec((tm, W), lambda i: (i, 0)),
                pl.BlockSpec((W, D), lambda i: (0, 0)), pl.BlockSpec((W, D), lambda i: (0, 0)),
                pl.BlockSpec((tm, D), lambda i: (i, 0)),
                pl.BlockSpec((1, RB, D), lambda i: (i // tiles_per_batch, 0, 0))]
    args = [o2, z2, wh, wl, x2, gate]
    out_shape = [jax.ShapeDtypeStruct((R, D), F32)]
    out_specs = [pl.BlockSpec((tm, D), lambda i: (i, 0))]
    if final:
        in_specs.append(pl.BlockSpec((1, D), lambda i: (0, 0)))
        args.append(final_g.reshape(1, D))
        out_shape.append(jax.ShapeDtypeStruct((R, D), F32))
        out_specs.append(pl.BlockSpec((tm, D), lambda i: (i, 0)))
    res = pl.pallas_call(
        functools.partial(_out_kernel, final=final), grid=(R // tm,),
        in_specs=in_specs, out_specs=out_specs, out_shape=out_shape,
        compiler_params=_cp("parallel"), name="gated_out_proj",
    )(*args)
    return res if final else (res[0], None)


def _cmp_kernel(kv_ref, w_ref, o_ref, *, nblk):
    S = A_CMP_STRIDE
    x = kv_ref[0].reshape(nblk, S, kv_ref.shape[2])
    lo = jnp.sum(x * w_ref[0:S, :], axis=1)
    hi = jnp.sum(x * w_ref[S:2 * S, :], axis=1)
    o_ref[0] = lo + pltpu.roll(hi, nblk - 1, 0)


def _cmp_weights(wk, wv, groups):
    return jnp.concatenate([jnp.tile(wk, (1, groups)), jnp.tile(wv, (1, groups))], axis=1)


def _nsa_compress_prompt(cmp3, wcat):
    B, T, W = cmp3.shape
    nblk = T // A_CMP_STRIDE
    return pl.pallas_call(
        functools.partial(_cmp_kernel, nblk=nblk), grid=(B,),
        in_specs=[pl.BlockSpec((1, T, W), lambda b: (b, 0, 0)), pl.BlockSpec((A_CMP_LEN, W), lambda b: (0, 0))],
        out_specs=pl.BlockSpec((1, nblk, W), lambda b: (b, 0, 0)),
        out_shape=jax.ShapeDtypeStruct((B, nblk, W), F32),
        compiler_params=_cp("parallel"), name="nsa_compress",
    )(cmp3, wcat)


def _nsa_prompt_kernel(q_ref, qr_ref, kvb_ref, sel_ref, win_ref, gl_ref, o_ref, *, tq, tk, T, passes):
    i = pl.program_id(1)
    t0 = i * tq
    R = 4
    nb = kvb_ref.shape[1]
    ncb = T // A_CMP_STRIDE - 1
    nsb = T // A_SEL_LEN
    gw = A_KV * HEAD_DIM
    gates = _sigmoid(gl_ref[0])

    n_io = _iota((R * tq, nb), 1)
    t_c = (_iota((R * tq, nb), 0) & (tq - 1)) + t0
    cmask = (n_io * A_CMP_STRIDE + (A_CMP_LEN - 1) <= t_c) & (n_io < ncb)

    ratio = A_SEL_LEN // A_CMP_STRIDE
    back = A_CMP_LEN // A_CMP_STRIDE - 1
    mn, mj = _iota((nb, LANES), 0), _iota((nb, LANES), 1)
    imp_mat = _onehot((mn >= mj * ratio - back) & (mn <= mj * ratio + ratio - 1) & (mn < ncb) & (mj < nsb))

    j_io = _iota((tq, LANES), 1)
    qb = (_iota((tq, LANES), 0) + t0) >> 6
    allowed = (j_io <= qb) & (j_io < nsb)
    forced = (j_io == 0) | (j_io == qb)

    trow = (_iota((R * tq, tk), 0) & (tq - 1)) + t0
    kcol = _iota((R * tq, tk), 1)
    c_hi = (t0 + tq + tk - 1) // tk
    w_lo = jnp.maximum(t0 - (A_WINDOW - 1), 0) // tk

    for g in range(A_KV):
        q4 = _stack_heads(q_ref, R * g, R) * QK_SCALE
        kb = kvb_ref[0, :, g * HEAD_DIM:(g + 1) * HEAD_DIM]
        vb = kvb_ref[0, :, gw + g * HEAD_DIM:gw + (g + 1) * HEAD_DIM]
        s = jnp.where(cmask, _dot3(q4, kb, NT), NEG_INF)
        m = jnp.max(s, axis=-1, keepdims=True)
        e = jnp.where(cmask, jnp.exp(s - m), 0.0)
        p = e / jnp.maximum(jnp.sum(e, axis=-1, keepdims=True), TINY)
        o_cmp = _dot3(p, vb)
        pg = p[0:tq] + p[tq:2 * tq] + p[2 * tq:3 * tq] + p[3 * tq:4 * tq]
        imp = _dotx(pg, imp_mat, 3)
        score = jnp.where(allowed, jnp.where(forced, A_FORCED, imp), -1.0)
        score = jnp.where(j_io < nsb, score, -2.0)
        rank = jnp.zeros((tq, LANES), F32)
        for j2 in range(nsb):
            col = score[:, j2:j2 + 1]
            rank = rank + jnp.where((col > score) | ((col == score) & (j2 < j_io)), 1.0, 0.0)
        sel = _onehot((rank < min(A_SEL_TOPK, nsb)) & allowed)
        sel4 = jnp.concatenate([sel] * R, axis=0)

        def sel_mask(c):
            blk = (_iota((LANES, tk), 1) + c * tk) >> 6
            hit = _dotf(sel4, _onehot(blk == _iota((LANES, tk), 0)))
            return (hit > 0.5) & (kcol + c * tk <= trow)

        def win_mask(c):
            d = trow - (kcol + c * tk)
            return (d >= 0) & (d < A_WINDOW)

        q4r = _stack_heads(qr_ref, R * g, R) * QK_SCALE
        o_sel, _, _ = _flash(q4r, sel_ref, sel_ref, g * HEAD_DIM, gw + g * HEAD_DIM, 0, c_hi, tk, sel_mask, passes)
        o_win, _, _ = _flash(q4r, win_ref, win_ref, g * HEAD_DIM, gw + g * HEAD_DIM, w_lo, c_hi, tk, win_mask, passes)
        outs = []
        for r in range(R):
            h = R * g + r
            rows = slice(r * tq, (r + 1) * tq)
            outs.append(gates[:, 3 * h:3 * h + 1] * o_cmp[rows] + gates[:, 3 * h + 1:3 * h + 2] * o_sel[rows]
                        + gates[:, 3 * h + 2:3 * h + 3] * o_win[rows])
        o_ref[0, :, g * R * HEAD_DIM:(g + 1) * R * HEAD_DIM] = jnp.concatenate(outs, axis=1)


def _nsa_prompt(q3, qr3, kvb, sel3, win3, gl3, passes, tq=128, tk=512):
    B, T, D = q3.shape
    nb = kvb.shape[1]
    full = lambda w: pl.BlockSpec((1, T, w), lambda b, i: (b, 0, 0))
    return pl.pallas_call(
        functools.partial(_nsa_prompt_kernel, tq=tq, tk=tk, T=T, passes=passes), grid=(B, T // tq),
        in_specs=[pl.BlockSpec((1, tq, D), lambda b, i: (b, i, 0)), pl.BlockSpec((1, tq, D), lambda b, i: (b, i, 0)),
                  pl.BlockSpec((1, nb, kvb.shape[2]), lambda b, i: (b, 0, 0)),
                  full(sel3.shape[2]), full(win3.shape[2]),
                  pl.BlockSpec((1, tq, LANES), lambda b, i: (b, i, 0))],
        out_specs=pl.BlockSpec((1, tq, D), lambda b, i: (b, i, 0)),
        out_shape=jax.ShapeDtypeStruct((B, T, D), F32),
        compiler_params=_cp("parallel", "arbitrary"), name="nsa_prompt",
    )(q3, qr3, kvb, sel3, win3, gl3)


def _dsa_prompt_kernel(q_ref, qi_ref, wi_ref, kv_ref, ki_ref, o_ref, ch_ref, *, tq, tk, T, topk, passes):
    i = pl.program_id(1)
    t0 = i * tq
    R = 4
    gw = kv_ref.shape[2] // 2
    wi = wi_ref[0] * (B_IDX_HEADS * B_IDX_DIM) ** -0.5
    ki = ki_ref[0, :, 0:B_IDX_DIM]
    score = jnp.zeros((tq, T), F32)
    for h in range(B_IDX_HEADS):
        logits = _dot3(qi_ref[0, :, h * B_IDX_DIM:(h + 1) * B_IDX_DIM], ki, NT)
        score = score + wi[:, h:h + 1] * jnp.maximum(logits, 0.0)
    causal = _iota((tq, T), 1) <= _iota((tq, T), 0) + t0
    score = jnp.where(causal, score, NEG_INF)
    ch_ref[...] = jnp.where(_topk_mask(_sortable_key(score), topk) & causal, 1.0, 0.0)
    c_hi = (t0 + tq + tk - 1) // tk

    def mask_fn(c):
        picked = ch_ref[:, pl.ds(pl.multiple_of(c * tk, tk), tk)]
        return jnp.concatenate([picked] * R, axis=0) > 0.5

    for g in range(gw // HEAD_DIM):
        q4 = _stack_heads(q_ref, R * g, R) * QK_SCALE
        o, _, _ = _flash(q4, kv_ref, kv_ref, g * HEAD_DIM, gw + g * HEAD_DIM, 0, c_hi, tk, mask_fn, passes)
        o_ref[0, :, g * R * HEAD_DIM:(g + 1) * R * HEAD_DIM] = jnp.concatenate(
            [o[r * tq:(r + 1) * tq] for r in range(R)], axis=1)


def _dsa_prompt(q3, qi3, wi3, kv3, ki3, topk, passes, tq=128, tk=512):
    B, T, D = q3.shape
    full = lambda w: pl.BlockSpec((1, T, w), lambda b, i: (b, 0, 0))
    blk = lambda w: pl.BlockSpec((1, tq, w), lambda b, i: (b, i, 0))
    return pl.pallas_call(
        functools.partial(_dsa_prompt_kernel, tq=tq, tk=tk, T=T, topk=topk, passes=passes), grid=(B, T // tq),
        in_specs=[blk(D), blk(qi3.shape[2]), blk(LANES), full(kv3.shape[2]), full(ki3.shape[2])],
        out_specs=blk(D), out_shape=jax.ShapeDtypeStruct((B, T, D), F32),
        scratch_shapes=[pltpu.VMEM((tq, T), F32)],
        compiler_params=_cp("parallel", "arbitrary"), name="dsa_prompt",
    )(q3, qi3, wi3, kv3, ki3)


def _dil_prompt_kernel(*refs, tq, tk, T, passes):
    ng = len(C_GROUPS)
    q_refs, k_refs, v_refs, o_ref = refs[0:ng], refs[ng:2 * ng], refs[2 * ng:3 * ng], refs[3 * ng]
    i = pl.program_id(2)
    t0 = i * tq
    trow = _iota((tq, tk), 0) + t0
    kcol = _iota((tq, tk), 1)
    c_hi = (t0 + tq + tk - 1) // tk
    outs = []
    for hh in range(LANES // HEAD_DIM):
        lane = hh * HEAD_DIM
        res = []
        for (w, r), q_ref, k_ref, v_ref in zip(C_GROUPS, q_refs, k_refs, v_refs):
            def mask_fn(c, w=w, r=r):
                d = trow - (kcol + c * tk)
                return (d >= 0) & (d <= w) & ((d & (r - 1)) == 0)

            c_lo = jnp.maximum(t0 - w, 0) // tk
            q = q_ref[0, :, lane:lane + HEAD_DIM] * QK_SCALE
            res.append(_flash(q, k_ref, v_ref, lane, lane, c_lo, c_hi, tk, mask_fn, passes))
        m_all = functools.reduce(jnp.maximum, [m for _, m, _ in res])
        wts = [den * jnp.exp(m - m_all) for _, m, den in res]
        tot = functools.reduce(lambda a, b: a + b, wts)
        outs.append(functools.reduce(lambda a, b: a + b, [(wt / tot) * o for wt, (o, _, _) in zip(wts, res)]))
    o_ref[0] = jnp.concatenate(outs, axis=1)


def _dil_prompt(qs, kvs, passes, tq=256, tk=512):
    B, T, W = qs[0].shape
    hp = W // LANES
    qspec = pl.BlockSpec((1, tq, LANES), lambda b, h, i: (b, i, h))
    kspec = pl.BlockSpec((1, T, LANES), lambda b, h, i: (b, 0, h))
    vspec = pl.BlockSpec((1, T, LANES), lambda b, h, i: (b, 0, hp + h))
    ng = len(C_GROUPS)
    return pl.pallas_call(
        functools.partial(_dil_prompt_kernel, tq=tq, tk=tk, T=T, passes=passes), grid=(B, hp, T // tq),
        in_specs=[qspec] * ng + [kspec] * ng + [vspec] * ng,
        out_specs=qspec, out_shape=jax.ShapeDtypeStruct((B, T, W), F32),
        compiler_params=_cp("parallel", "parallel", "arbitrary"), name="dilated_prompt",
    )(*qs, *kvs, *kvs)


def _logf_kernel(f_ref, b_ref, lf_ref, c_ref, *, T, tc):
    x = f_ref[0] + b_ref[...]
    lf = jnp.minimum(x, 0.0) - jnp.log(1.0 + jnp.exp(-jnp.abs(x)))
    lf_ref[0] = lf
    parts = _split(lf, 3)
    for c in range(T // tc):
        tri = _onehot(_iota((tc, T), 1) <= _iota((tc, T), 0) + c * tc)
        c_ref[0, c * tc:(c + 1) * tc, :] = functools.reduce(lambda a, b: a + b, [_dotf(tri, p) for p in parts])


def _fox_logf(f3, bias_row):
    B, T, W = f3.shape
    spec = pl.BlockSpec((1, T, W), lambda b: (b, 0, 0))
    return pl.pallas_call(
        functools.partial(_logf_kernel, T=T, tc=min(256, T)), grid=(B,),
        in_specs=[spec, pl.BlockSpec((1, W), lambda b: (0, 0))],
        out_specs=[spec, spec], out_shape=[jax.ShapeDtypeStruct((B, T, W), F32)] * 2,
        compiler_params=_cp("parallel"), name="fox_logf_cumsum",
    )(f3, bias_row)


def _fox_prompt_kernel(q_ref, k_ref, v_ref, cc_ref, cr_ref, o_ref, *, tq, tk, T, passes):
    hp = pl.program_id(1)
    i = pl.program_id(2)
    t0 = i * tq
    trow = _iota((tq, tk), 0) + t0
    kcol = _iota((tq, tk), 1)
    c_hi = (t0 + tq + tk - 1) // tk
    outs = []
    for hh in range(LANES // HEAD_DIM):
        h = hp * (LANES // HEAD_DIM) + hh
        lane = hh * HEAD_DIM
        c_col = jnp.sum(jnp.where(_iota((tq, LANES), 1) == h, cc_ref[0], 0.0), axis=-1, keepdims=True)

        def bias_fn(c, h=h, c_col=c_col):
            off = pl.multiple_of(c * tk, tk)
            return c_col - cr_ref[0, pl.ds(h, 1), pl.ds(off, tk)]

        def mask_fn(c):
            return kcol + c * tk <= trow

        q = q_ref[0, :, lane:lane + HEAD_DIM] * QK_SCALE
        o, _, _ = _flash(q, k_ref, v_ref, lane, lane, 0, c_hi, tk, mask_fn, passes, bias_fn)
        outs.append(o)
    o_ref[0] = jnp.concatenate(outs, axis=1)


def _fox_prompt(q3, kv3, c_col, c_row, passes, tq=256, tk=512):
    B, T, W = q3.shape
    hp = W // LANES
    return pl.pallas_call(
        functools.partial(_fox_prompt_kernel, tq=tq, tk=tk, T=T, passes=passes), grid=(B, hp, T // tq),
        in_specs=[pl.BlockSpec((1, tq, LANES), lambda b, h, i: (b, i, h)),
                  pl.BlockSpec((1, T, LANES), lambda b, h, i: (b, 0, h)),
                  pl.BlockSpec((1, T, LANES), lambda b, h, i: (b, 0, hp + h)),
                  pl.BlockSpec((1, tq, LANES), lambda b, h, i: (b, i, 0)),
                  pl.BlockSpec((1, c_row.shape[1], T), lambda b, h, i: (b, 0, 0))],
        out_specs=pl.BlockSpec((1, tq, LANES), lambda b, h, i: (b, i, h)),
        out_shape=jax.ShapeDtypeStruct((B, T, W), F32),
        compiler_params=_cp("parallel", "parallel", "arbitrary"), name="fox_prompt",
    )(q3, kv3, kv3, c_col, c_row)


def _dec_scores(kexp, q_row, head_cols):
    return _dotx(kexp * q_row, head_cols, 2)


def _dec_softmax_rows(s, valid):
    if valid is not None:
        s = jnp.where(valid, s, NEG_INF)
    m = jnp.max(s, axis=0, keepdims=True)
    e = jnp.exp(s - m)
    if valid is not None:
        e = jnp.where(valid, e, 0.0)
    den = jnp.sum(e, axis=0, keepdims=True)
    return e / jnp.maximum(den, TINY), m, den


def _dec_weighted_sum(p, vexp, col_lanes):
    return jnp.sum(_dotx(p, col_lanes, 2) * vexp, axis=0, keepdims=True)


def _dec_online_update(s, valid, vexp, col_lanes, m_ref, l_ref, acc_ref):
    if valid is not None:
        s = jnp.where(valid, s, NEG_INF)
    m_old = m_ref[...]
    m_new = jnp.maximum(m_old, jnp.max(s, axis=0, keepdims=True))
    alpha = jnp.exp(m_old - m_new)
    e = jnp.exp(s - m_new)
    if valid is not None:
        e = jnp.where(valid, e, 0.0)
    l_ref[...] = alpha * l_ref[...] + jnp.sum(e, axis=0, keepdims=True)
    a_exp = _dotx(jnp.broadcast_to(alpha, (8, LANES)), col_lanes, 2)[0:1]
    acc_ref[...] = acc_ref[...] * a_exp + _dec_weighted_sum(e, vexp, col_lanes)
    m_ref[...] = m_new


def _dec_finish(l_ref, acc_ref, col_lanes):
    l_exp = _dotx(jnp.broadcast_to(jnp.maximum(l_ref[...], TINY), (8, LANES)), col_lanes, 3)[0:1]
    return acc_ref[...] / l_exp


def _page_specs(n, block, layer, index_fn):
    specs = []
    for pi in range(n):
        def imap(b, s, *pf, pi=pi):
            return (layer, index_fn(b, s, pi, *pf)) + (0,) * (len(block) - 2)
        specs.append(pl.BlockSpec(block, imap))
    return specs


def _scmp_kernel(pt_ref, *refs, pp):
    pages, w_ref, o_ref = refs[:pp], refs[pp], refs[pp + 1]
    S = A_CMP_STRIDE
    W = w_ref.shape[1]
    n = PAGE // S
    for pi in range(pp):
        x = pages[pi][...].reshape(n, S, W)
        o_ref[0, pi * n:(pi + 1) * n, 0:W] = jnp.sum(x * w_ref[0:S, :], axis=1)
        o_ref[0, pi * n:(pi + 1) * n, W:2 * W] = jnp.sum(x * w_ref[S:2 * S, :], axis=1)


def _nsa_compress_sample(cache4, layer, page_table, wcat, pp=8):
    Bd, NP = page_table.shape
    W = cache4.shape[3]
    n = PAGE // A_CMP_STRIDE
    grid_spec = pltpu.PrefetchScalarGridSpec(
        num_scalar_prefetch=1, grid=(Bd, NP // pp),
        in_specs=_page_specs(pp, (None, None, PAGE, W), layer, lambda b, s, pi, pt: pt[b, s * pp + pi])
        + [pl.BlockSpec((A_CMP_LEN, W), lambda b, s, pt: (0, 0))],
        out_specs=pl.BlockSpec((1, pp * n, 2 * W), lambda b, s, pt: (b, s, 0)))
    return pl.pallas_call(
        functools.partial(_scmp_kernel, pp=pp), grid_spec=grid_spec,
        out_shape=jax.ShapeDtypeStruct((Bd, NP * n, 2 * W), F32),
        compiler_params=_cp("parallel", "arbitrary"), name="nsa_compress_paged",
    )(page_table, *([cache4] * pp), wcat)


def _nsa_sample_cmp_kernel(lohi_ref, q_ref, o_ref, idx_ref, *, P, nrows):
    W = lohi_ref.shape[2] // 2
    gw = W // 2
    nch = lohi_ref.shape[1]
    R = 4
    lpad = -(-(P + 1) // A_SEL_LEN) * A_SEL_LEN
    ncb = lpad // A_CMP_STRIDE - 1
    nsb = lpad // A_SEL_LEN
    qb = P // A_SEL_LEN
    lohi = lohi_ref[0]
    kvb = lohi[:, 0:W] + pltpu.roll(lohi[:, W:2 * W], nch - 1, 0)
    kexp = jnp.concatenate([kvb[:, 0:gw]] * R, axis=1)
    vexp = jnp.concatenate([kvb[:, gw:W]] * R, axis=1)
    L = kexp.shape[1]
    cols = _head_to_lane(L, LANES)
    s = _dec_scores(kexp, q_ref[0] * QK_SCALE, cols)
    n_io = _iota((nch, LANES), 0)
    valid = (n_io * A_CMP_STRIDE + (A_CMP_LEN - 1) <= P) & (n_io < ncb)
    p, _, _ = _dec_softmax_rows(s, valid)
    o_ref[0] = _dec_weighted_sum(p, vexp, _lane_to_head(LANES, L))
    grp = _onehot(((_iota((LANES, LANES), 0) & (A_KV - 1)) == _iota((LANES, LANES), 1))
                  & (_iota((LANES, LANES), 0) < R * A_KV))
    pg = _dotx(p, grp, 3)
    ratio = A_SEL_LEN // A_CMP_STRIDE
    back = A_CMP_LEN // A_CMP_STRIDE - 1
    mj, mn = _iota((nrows, nch), 0), _iota((nrows, nch), 1)
    imp = _xdot(_onehot((mn >= mj * ratio - back) & (mn <= mj * ratio + ratio - 1) & (mn < ncb)), pg, 3)
    j_io = _iota((nrows, LANES), 0)
    score = jnp.where(j_io <= qb, jnp.where((j_io == 0) | (j_io == qb), A_FORCED, imp), -1.0)
    score = jnp.where(j_io < nsb, score, -2.0)
    jf = j_io.astype(F32)
    picks = []
    for _ in range(min(A_SEL_TOPK, nsb)):
        best = jnp.max(score, axis=0, keepdims=True)
        pick = jnp.min(jnp.where(score == best, jf, 1e9), axis=0, keepdims=True)
        picks.append(pick)
        score = jnp.where(jf == pick, -3.0, score)
    idx_ref[0] = jnp.concatenate(picks, axis=0).astype(I32)


def _nsa_sample_cmp(lohi, qp3, P):
    Bd, nch, W2 = lohi.shape
    L = qp3.shape[2]
    lpad = -(-(P + 1) // A_SEL_LEN) * A_SEL_LEN
    nrows = -(-(lpad // A_SEL_LEN) // 8) * 8
    k = min(A_SEL_TOPK, lpad // A_SEL_LEN)
    return pl.pallas_call(
        functools.partial(_nsa_sample_cmp_kernel, P=P, nrows=nrows), grid=(Bd,),
        in_specs=[pl.BlockSpec((1, nch, W2), lambda b: (b, 0, 0)), pl.BlockSpec((1, 1, L), lambda b: (b, 0, 0))],
        out_specs=[pl.BlockSpec((1, 1, L), lambda b: (b, 0, 0)), pl.BlockSpec((1, k, LANES), lambda b: (b, 0, 0))],
        out_shape=[jax.ShapeDtypeStruct((Bd, 1, L), F32), jax.ShapeDtypeStruct((Bd, k, LANES), I32)],
        compiler_params=_cp("parallel"), name="nsa_sample_cmp_select",
    )(lohi, qp3)


def _nsa_sample_sel_kernel(pt_ref, ix_ref, *refs, P, nk):
    blks = refs[:A_KV]
    q_ref, new_ref, ocmp_ref, owin_ref, gl_ref, o_ref, m_ref, l_ref, acc_ref = refs[A_KV:]
    b = pl.program_id(0)
    k = pl.program_id(1)
    R = 4
    gw = A_KV * HEAD_DIM

    @pl.when(k == 0)
    def _():
        m_ref[...] = jnp.full(m_ref.shape, NEG_INF, F32)
        l_ref[...] = jnp.zeros(l_ref.shape, F32)
        acc_ref[...] = jnp.zeros(acc_ref.shape, F32)

    for g in range(A_KV):
        j = ix_ref[b, g * nk + k]
        is_new = j * A_SEL_LEN >= P
        blk = blks[g][...]
        row0 = _iota((A_SEL_LEN, HEAD_DIM), 0) == 0
        knew = new_ref[0, :, g * HEAD_DIM:(g + 1) * HEAD_DIM]
        vnew = new_ref[0, :, gw + g * HEAD_DIM:gw + (g + 1) * HEAD_DIM]
        kg = jnp.where(is_new, jnp.where(row0, knew, 0.0), blk[:, g * HEAD_DIM:(g + 1) * HEAD_DIM])
        vg = jnp.where(is_new, jnp.where(row0, vnew, 0.0), blk[:, gw + g * HEAD_DIM:gw + (g + 1) * HEAD_DIM])
        rows = slice(R * g, R * (g + 1))
        s = _dot3(q_ref[0, rows, :] * QK_SCALE, kg, NT)
        valid = j * A_SEL_LEN + _iota((R, A_SEL_LEN), 1) <= P
        s = jnp.where(valid, s, NEG_INF)
        m_old = m_ref[rows, :]
        m_new = jnp.maximum(m_old, jnp.max(s, axis=-1, keepdims=True))
        alpha = jnp.exp(m_old - m_new)
        e = jnp.where(valid, jnp.exp(s - m_new), 0.0)
        l_ref[rows, :] = alpha * l_ref[rows, :] + jnp.sum(e, axis=-1, keepdims=True)
        acc_ref[rows, :] = alpha * acc_ref[rows, :] + _dot3(e, vg)
        m_ref[rows, :] = m_new

    @pl.when(k == nk - 1)
    def _():
        gates = _sigmoid(gl_ref[0])
        o_sel = acc_ref[...] / jnp.maximum(l_ref[...], TINY)
        o_ref[0] = gates[:, 0:1] * ocmp_ref[0] + gates[:, 1:2] * o_sel + gates[:, 2:3] * owin_ref[0]


def _nsa_sample_sel(cache_sel4, layer, page_table, idx_flat, q3, new3, ocmp3, owin3, gl3, P):
    Bd, H, dh = q3.shape
    nk = idx_flat.shape[1] // A_KV
    half = PAGE // A_SEL_LEN
    last = P // A_SEL_LEN - 1

    def blk_index(g):
        def imap(b, k, pt, ix):
            j = jnp.minimum(ix[b, g * nk + k], last)
            return (layer, pt[b, j // half] * half + j % half, 0, 0)
        return imap

    W = cache_sel4.shape[3]
    per_b = lambda shape: pl.BlockSpec((1,) + shape, lambda b, k, pt, ix: (b, 0, 0))
    grid_spec = pltpu.PrefetchScalarGridSpec(
        num_scalar_prefetch=2, grid=(Bd, nk),
        in_specs=[pl.BlockSpec((None, None, A_SEL_LEN, W), blk_index(g)) for g in range(A_KV)]
        + [per_b((H, dh)), per_b((1, W)), per_b((H, dh)), per_b((H, dh)), per_b((H, 3))],
        out_specs=per_b((H, dh)),
        scratch_shapes=[pltpu.VMEM((H, 1), F32), pltpu.VMEM((H, 1), F32), pltpu.VMEM((H, dh), F32)])
    return pl.pallas_call(
        functools.partial(_nsa_sample_sel_kernel, P=P, nk=nk), grid_spec=grid_spec,
        out_shape=jax.ShapeDtypeStruct((Bd, H, dh), F32),
        compiler_params=_cp("parallel", "arbitrary"), name="nsa_sample_selected",
    )(page_table, idx_flat, *([cache_sel4] * A_KV), q3, new3, ocmp3, owin3, gl3)


def _nsa_sample_win_kernel(buf_ref, new_ref, q_ref, nbuf_ref, o_ref):
    W = buf_ref.shape[1]
    L = buf_ref.shape[2]
    gw = L // 2
    R = 4
    nb = jnp.where(_iota((W, L), 0) == W - 1, new_ref[0], pltpu.roll(buf_ref[0], W - 1, 0))
    nbuf_ref[0] = nb
    kexp = jnp.concatenate([nb[:, 0:gw]] * R, axis=1)
    vexp = jnp.concatenate([nb[:, gw:L]] * R, axis=1)
    s = _dec_scores(kexp, q_ref[0] * QK_SCALE, _head_to_lane(R * gw, LANES))
    p, _, _ = _dec_softmax_rows(s, None)
    o_ref[0] = _dec_weighted_sum(p, vexp, _lane_to_head(LANES, R * gw))


def _nsa_sample_win(buf3, new3, qp3):
    Bd, W, L = buf3.shape
    Lq = qp3.shape[2]
    return pl.pallas_call(
        _nsa_sample_win_kernel, grid=(Bd,),
        in_specs=[pl.BlockSpec((1, W, L), lambda b: (b, 0, 0)), pl.BlockSpec((1, 1, L), lambda b: (b, 0, 0)),
                  pl.BlockSpec((1, 1, Lq), lambda b: (b, 0, 0))],
        out_specs=[pl.BlockSpec((1, W, L), lambda b: (b, 0, 0)), pl.BlockSpec((1, 1, Lq), lambda b: (b, 0, 0))],
        out_shape=[jax.ShapeDtypeStruct((Bd, W, L), F32), jax.ShapeDtypeStruct((Bd, 1, Lq), F32)],
        compiler_params=_cp("parallel"), name="nsa_sample_window",
    )(buf3, new3, qp3)


def _dsa_idx_kernel(pt_ref, *refs, pp):
    pages, qi_ref, wi_ref, o_ref = refs[:pp], refs[pp], refs[pp + 1], refs[pp + 2]
    wi = wi_ref[0] * (B_IDX_HEADS * B_IDX_DIM) ** -0.5
    qi = qi_ref[0]
    for pi in range(pp):
        logits = _dot3(qi, pages[pi][...], NT)
        o_ref[0, pi] = jnp.sum(wi * jnp.maximum(logits, 0.0), axis=0, keepdims=True)


def _dsa_sample_scores(kidx4, layer, page_table, qi3, wi3, pp=8):
    Bd, NP = page_table.shape
    grid_spec = pltpu.PrefetchScalarGridSpec(
        num_scalar_prefetch=1, grid=(Bd, NP // pp),
        in_specs=_page_specs(pp, (None, None, PAGE, B_IDX_DIM), layer, lambda b, s, pi, pt: pt[b, s * pp + pi])
        + [pl.BlockSpec((1, B_IDX_HEADS, B_IDX_DIM), lambda b, s, pt: (b, 0, 0)),
           pl.BlockSpec((1, B_IDX_HEADS, 1), lambda b, s, pt: (b, 0, 0))],
        out_specs=pl.BlockSpec((1, pp, 1, PAGE), lambda b, s, pt: (b, s, 0, 0)))
    return pl.pallas_call(
        functools.partial(_dsa_idx_kernel, pp=pp), grid_spec=grid_spec,
        out_shape=jax.ShapeDtypeStruct((Bd, NP, 1, PAGE), F32),
        compiler_params=_cp("parallel", "arbitrary"), name="dsa_sample_indexer",
    )(page_table, *([kidx4] * pp), qi3, wi3)


def _dsa_select_kernel(sc_ref, qi_ref, wi_ref, kin_ref, o_ref, *, topk):
    Bd, P = sc_ref.shape
    wi = wi_ref[...] * (B_IDX_HEADS * B_IDX_DIM) ** -0.5
    logit = jnp.sum(qi_ref[...] * kin_ref[...], axis=-1, keepdims=True)
    s_new = jnp.sum(wi * jnp.maximum(logit, 0.0), axis=1)
    tail = jnp.where(_iota((Bd, LANES), 1) == 0, s_new, -jnp.inf)
    full = jnp.concatenate([sc_ref[...], tail], axis=1)
    o_ref[...] = jnp.where(_topk_mask(_sortable_key(full), topk), 1.0, 0.0)


def _dsa_sample_select(scores2, qi3, wi3, kinew3, topk):
    Bd, P = scores2.shape
    whole = lambda shape: pl.BlockSpec(shape, lambda i: (0,) * len(shape))
    return pl.pallas_call(
        functools.partial(_dsa_select_kernel, topk=topk), grid=(1,),
        in_specs=[whole(scores2.shape), whole(qi3.shape), whole(wi3.shape), whole(kinew3.shape)],
        out_specs=whole((Bd, P + LANES)), out_shape=jax.ShapeDtypeStruct((Bd, P + LANES), F32),
        compiler_params=_cp("arbitrary"), name="dsa_sample_topk",
    )(scores2, qi3, wi3, kinew3)


def _dsa_sample_attn_kernel(pt_ref, *refs, pp, nsteps):
    pages = refs[:pp]
    mask_ref, q_ref, new_ref, mnew_ref, o_ref, m_ref, l_ref, acc_ref = refs[pp:]
    s_id = pl.program_id(1)
    R = 4
    W = new_ref.shape[2]
    gw = W // 2
    L = R * gw
    cols = _head_to_lane(L, LANES)
    lanes = _lane_to_head(LANES, L)
    q = q_ref[0] * QK_SCALE

    @pl.when(s_id == 0)
    def _():
        m_ref[...] = jnp.full(m_ref.shape, NEG_INF, F32)
        l_ref[...] = jnp.zeros(l_ref.shape, F32)
        acc_ref[...] = jnp.zeros(acc_ref.shape, F32)

    eye = _onehot(_iota((PAGE, PAGE), 0) == _iota((PAGE, PAGE), 1))
    for pi in range(pp):
        kv = pages[pi][...]
        kexp = jnp.concatenate([kv[:, 0:gw]] * R, axis=1)
        vexp = jnp.concatenate([kv[:, gw:W]] * R, axis=1)
        s = _dec_scores(kexp, q, cols)
        picked = jnp.broadcast_to(mask_ref[0, pi], (PAGE, PAGE)).astype(BF16)
        valid = _dotf(eye, picked, NT) > 0.5
        _dec_online_update(s, valid, vexp, lanes, m_ref, l_ref, acc_ref)

    @pl.when(s_id == nsteps - 1)
    def _():
        new = jnp.broadcast_to(new_ref[0], (8, W))
        kexp = jnp.concatenate([new[:, 0:gw]] * R, axis=1)
        vexp = jnp.concatenate([new[:, gw:W]] * R, axis=1)
        s = _dec_scores(kexp, q, cols)
        valid = (_iota((8, LANES), 0) == 0) & (mnew_ref[0] > 0.5)
        _dec_online_update(s, valid, vexp, lanes, m_ref, l_ref, acc_ref)
        o_ref[0] = _dec_finish(l_ref, acc_ref, lanes)


def _dsa_sample_attn(kv4, layer, page_table, mask4, qp3, new3, mnew3, pp=8):
    Bd, NP = page_table.shape
    W = kv4.shape[3]
    L = qp3.shape[2]
    nsteps = NP // pp
    per_b = lambda shape: pl.BlockSpec((1,) + shape, lambda b, s, pt: (b,) + (0,) * len(shape))
    grid_spec = pltpu.PrefetchScalarGridSpec(
        num_scalar_prefetch=1, grid=(Bd, nsteps),
        in_specs=_page_specs(pp, (None, None, PAGE, W), layer, lambda b, s, pi, pt: pt[b, s * pp + pi])
        + [pl.BlockSpec((1, pp, 1, PAGE), lambda b, s, pt: (b, s, 0, 0)), per_b((1, L)), per_b((1, W)), per_b((1, LANES))],
        out_specs=per_b((1, L)),
        scratch_shapes=[pltpu.VMEM((1, LANES), F32), pltpu.VMEM((1, LANES), F32), pltpu.VMEM((1, L), F32)])
    return pl.pallas_call(
        functools.partial(_dsa_sample_attn_kernel, pp=pp, nsteps=nsteps), grid_spec=grid_spec,
        out_shape=jax.ShapeDtypeStruct((Bd, 1, L), F32),
        compiler_params=_cp("parallel", "arbitrary"), name="dsa_sample_attention",
    )(page_table, *([kv4] * pp), mask4, qp3, new3, mnew3)


def _shift_kernel(buf_ref, new_ref, o_ref):
    W, L = buf_ref.shape[1], buf_ref.shape[2]
    o_ref[0] = jnp.where(_iota((W, L), 0) == W - 1, new_ref[0], pltpu.roll(buf_ref[0], W - 1, 0))


def _shift_window(buf3, new3, tl=256):
    Bd, W, L = buf3.shape
    return pl.pallas_call(
        _shift_kernel, grid=(Bd, L // tl),
        in_specs=[pl.BlockSpec((1, W, tl), lambda b, j: (b, 0, j)), pl.BlockSpec((1, 1, tl), lambda b, j: (b, 0, j))],
        out_specs=pl.BlockSpec((1, W, tl), lambda b, j: (b, 0, j)),
        out_shape=jax.ShapeDtypeStruct((Bd, W, L), F32),
        compiler_params=_cp("parallel", "parallel"), name="window_shift",
    )(buf3, new3)


def _dil_sample_kernel(*refs, P):
    ng = len(C_GROUPS)
    buf_refs, new_refs, q_refs, o_ref = refs[0:ng], refs[ng:2 * ng], refs[2 * ng:3 * ng], refs[3 * ng]
    Wd = q_refs[0].shape[2]
    cols = _head_to_lane(Wd, LANES)
    lanes = _lane_to_head(LANES, Wd)
    res = []
    for (w, r), buf_ref, new_ref, q_ref in zip(C_GROUPS, buf_refs, new_refs, q_refs):
        n = buf_ref.shape[1]
        allrows = jnp.concatenate([buf_ref[0], jnp.broadcast_to(new_ref[0], (8, 2 * Wd))], axis=0)
        row = _iota((n + 8, LANES), 0)
        dist = (n - row) * r
        valid = (row <= n) & (dist <= P)
        s = _dec_scores(allrows[:, 0:Wd], q_ref[0] * QK_SCALE, cols)
        p, m, den = _dec_softmax_rows(s, valid)
        res.append((_dec_weighted_sum(p, allrows[:, Wd:2 * Wd], lanes), m, den))
    m_all = functools.reduce(jnp.maximum, [m for _, m, _ in res])
    wts = [den * jnp.exp(m - m_all) for _, m, den in res]
    tot = functools.reduce(lambda a, b: a + b, wts)
    out = None
    for wt, (o, _, _) in zip(wts, res):
        a_exp = _dotx(jnp.broadcast_to(wt / tot, (8, LANES)), lanes, 3)[0:1]
        out = a_exp * o if out is None else out + a_exp * o
    o_ref[0] = out


def _dil_sample(bufs3, news3, qs3, P):
    Bd = qs3[0].shape[0]
    Wd = qs3[0].shape[2]
    in_specs, args = [], []
    for (w, r), buf in zip(C_GROUPS, bufs3):
        n = buf.shape[1] // r
        args.append(buf.reshape(Bd, n, r * 2 * Wd))
        in_specs.append(pl.BlockSpec((1, n, 2 * Wd), lambda b: (b, 0, 0)))
    for new in news3:
        args.append(new)
        in_specs.append(pl.BlockSpec((1, 1, 2 * Wd), lambda b: (b, 0, 0)))
    for q in qs3:
        args.append(q)
        in_specs.append(pl.BlockSpec((1, 1, Wd), lambda b: (b, 0, 0)))
    return pl.pallas_call(
        functools.partial(_dil_sample_kernel, P=P), grid=(Bd,),
        in_specs=in_specs, out_specs=pl.BlockSpec((1, 1, Wd), lambda b: (b, 0, 0)),
        out_shape=jax.ShapeDtypeStruct((Bd, 1, Wd), F32),
        compiler_params=_cp("parallel"), name="dilated_sample",
    )(*args)


def _fox_sample_kernel(pt_ref, *refs, pp, nsteps):
    pages, lfs = refs[:pp], refs[pp:2 * pp]
    q_ref, new_ref, lfnew_ref, o_ref, m_ref, l_ref, acc_ref, carry_ref = refs[2 * pp:]
    s_id = pl.program_id(1)
    W = new_ref.shape[2]
    L = W // 2
    cols = _head_to_lane(L, LANES)
    lanes = _lane_to_head(LANES, L)
    q = q_ref[0] * QK_SCALE

    @pl.when(s_id == 0)
    def _():
        m_ref[...] = jnp.full(m_ref.shape, NEG_INF, F32)
        l_ref[...] = jnp.zeros(l_ref.shape, F32)
        acc_ref[...] = jnp.zeros(acc_ref.shape, F32)
        new = jnp.broadcast_to(new_ref[0], (8, W))
        s = _dec_scores(new[:, 0:L], q, cols)
        _dec_online_update(s, _iota((8, LANES), 0) == 0, new[:, L:W], lanes, m_ref, l_ref, acc_ref)
        carry_ref[...] = lfnew_ref[0]

    later = _onehot(_iota((PAGE, PAGE), 1) > _iota((PAGE, PAGE), 0))
    ones = jnp.ones((8, PAGE), BF16)
    for pi in range(pp):
        kv = pages[pi][...]
        lf_t = lfs[pi][...]
        lf_pad = jnp.concatenate([lf_t, jnp.zeros((LANES - lf_t.shape[0], PAGE), F32)], axis=0)
        carry = carry_ref[...]
        bias = _xdot(later, lf_pad, 3, NT) + carry
        carry_ref[...] = carry + _xdot(ones, lf_pad, 3, NT)[0:1]
        s = _dec_scores(kv[:, 0:L], q, cols) + bias
        _dec_online_update(s, None, kv[:, L:W], lanes, m_ref, l_ref, acc_ref)

    @pl.when(s_id == nsteps - 1)
    def _():
        o_ref[0] = _dec_finish(l_ref, acc_ref, lanes)


def _fox_sample(kv4, lft4, layer, page_table, q3, new3, lfnew3, pp=4):
    Bd, NP = page_table.shape
    W = kv4.shape[3]
    H = lft4.shape[2]
    L = q3.shape[2]
    nsteps = NP // pp
    rev = lambda b, s, pi, pt: pt[b, NP - 1 - (s * pp + pi)]
    per_b = lambda shape: pl.BlockSpec((1,) + shape, lambda b, s, pt: (b,) + (0,) * len(shape))
    grid_spec = pltpu.PrefetchScalarGridSpec(
        num_scalar_prefetch=1, grid=(Bd, nsteps),
        in_specs=_page_specs(pp, (None, None, PAGE, W), layer, rev) + _page_specs(pp, (None, None, H, PAGE), layer, rev)
        + [per_b((1, L)), per_b((1, W)), per_b((1, LANES))],
        out_specs=per_b((1, L)),
        scratch_shapes=[pltpu.VMEM((1, LANES), F32), pltpu.VMEM((1, LANES), F32), pltpu.VMEM((1, L), F32),
                        pltpu.VMEM((1, LANES), F32)])
    return pl.pallas_call(
        functools.partial(_fox_sample_kernel, pp=pp, nsteps=nsteps), grid_spec=grid_spec,
        out_shape=jax.ShapeDtypeStruct((Bd, 1, L), F32),
        compiler_params=_cp("parallel", "arbitrary"), name="fox_sample",
    )(page_table, *([kv4] * pp), *([lft4] * pp), q3, new3, lfnew3)


def _pad_cols(w, pieces):
    out = []
    for s, wd, pw in pieces:
        out.append(w[:, s:s + wd])
        if pw > wd:
            out.append(jnp.zeros((w.shape[0], pw - wd), w.dtype))
    return jnp.concatenate(out, axis=1)


def _group_major(x2, G, R):
    Bd = x2.shape[0]
    return x2.reshape(Bd, G, R, HEAD_DIM).transpose(0, 2, 1, 3).reshape(Bd, G * R * HEAD_DIM)


def _head_major(x2, G, R):
    Bd = x2.shape[0]
    return x2.reshape(Bd, R, G, HEAD_DIM).transpose(0, 2, 1, 3).reshape(Bd, G * R * HEAD_DIM)


PASSES = 3


def kernel(x_prompt, x_sample, cache_a_cmp, cache_a_sel, cache_a_win, cache_b_kv, cache_b_kidx, cache_c_win0, cache_c_win1, cache_c_win2, cache_d_kv, cache_d_logf, page_table, c_prompt, c_sample, ada_w, ada_b, norm_g, final_g, a_w_in, a_w_out, a_cmp_wk, a_cmp_wv, b_w_in, b_w_out, c_w_in, c_w_out, d_w_in, d_w_out, d_f_bias):
    B, T, D = x_prompt.shape
    Bd = x_sample.shape[0]
    depth = ada_w.shape[0]
    NP = page_table.shape[1]
    P = NP * PAGE
    pool = cache_b_kidx.shape[1]
    assert x_sample.shape[1] == 1 and D == 1024 and P % A_SEL_LEN == 0
    assert cache_a_win.shape[2] == A_WINDOW and all(b.shape[2] == w for b, (w, _) in zip((cache_c_win0, cache_c_win1, cache_c_win2), C_GROUPS))
    H = D // HEAD_DIM
    tm_p = 256
    tpb = T // tm_p

    mod = _mod_all(jnp.concatenate([c_prompt, c_sample], axis=0), ada_w, ada_b)
    tab_p = _rope_tables(jnp.arange(T, dtype=I32))
    tab_s = _rope_tables(jnp.full((Bd,), P, I32))
    xp = x_prompt.reshape(B * T, D)
    xs = x_sample.reshape(Bd, D)
    c_bufs = (cache_c_win0, cache_c_win1, cache_c_win2)
    st = {}
    yp = ys = None
    for i in range(depth):
        kind, li = i % 4, i // 4
        shift_p, scale_p, gate_p = [mod[i, :B, j * D:(j + 1) * D].reshape(B, 1, D) for j in range(3)]
        shift_s, scale_s, gate_s = [mod[i, B:, j * D:(j + 1) * D].reshape(1, Bd, D) for j in range(3)]
        proj_p = lambda w, segs: _proj(xp, scale_p, shift_p, norm_g[i], w, segs, tab_p, tpb, tm_p)
        proj_s = lambda w, segs: _proj(xs, scale_s, shift_s, norm_g[i], w, segs, tab_s, 1, Bd)
        if kind == 0:
            w = _pad_cols(a_w_in[li], [(0, 2608, 2688), (2608, 1024, 1024)])
            segs = [(0, 1024, ("n", "r")), (1024, 512, ("n",)), (1536, 512, ("rk",)), (2048, 512, ("rk",)),
                    (2560, 128, ("n",)), (2688, 1024, ("n",))]
            wcat = _cmp_weights(a_cmp_wk[li], a_cmp_wv[li], A_KV)
            q, qr, cmp_, sel, win, gl, z = proj_p(w, segs)
            kvb = _nsa_compress_prompt(cmp_.reshape(B, T, 512), wcat)
            o = _nsa_prompt(q.reshape(B, T, D), qr.reshape(B, T, D), kvb, sel.reshape(B, T, 512),
                            win.reshape(B, T, 512), gl.reshape(B, T, LANES), PASSES)
            op, zp, wo = o.reshape(B * T, D), z, a_w_out[li]
            st.setdefault("a_cmp_p", []).append(cmp_.reshape(B, T, 2, A_KV, HEAD_DIM))
            st.setdefault("a_sel_p", []).append(sel.reshape(B, T, 2, A_KV, HEAD_DIM))
            st.setdefault("a_win_p", []).append(win.reshape(B, T, 2, A_KV, HEAD_DIM)[:, -min(A_WINDOW, T):])

            q, qr, cmp_, sel, win, gl, z = proj_s(w, segs)
            lohi = _nsa_compress_sample(cache_a_cmp.reshape(cache_a_cmp.shape[0], pool, PAGE, 512), li, page_table, wcat)
            ocmp, idx = _nsa_sample_cmp(lohi, _group_major(q, A_KV, 4).reshape(Bd, 1, D), P)
            nwin, owin = _nsa_sample_win(cache_a_win[li].reshape(Bd, A_WINDOW, 512), win.reshape(Bd, 1, 512),
                                         _group_major(qr, A_KV, 4).reshape(Bd, 1, D))
            idx_flat = idx[:, :, :A_KV].transpose(0, 2, 1).reshape(Bd, -1)
            o = _nsa_sample_sel(cache_a_sel.reshape(cache_a_sel.shape[0], pool * (PAGE // A_SEL_LEN), A_SEL_LEN, 512),
                                li, page_table, idx_flat, qr.reshape(Bd, H, HEAD_DIM), sel.reshape(Bd, 1, 512),
                                _head_major(ocmp.reshape(Bd, D), A_KV, 4).reshape(Bd, H, HEAD_DIM),
                                _head_major(owin.reshape(Bd, D), A_KV, 4).reshape(Bd, H, HEAD_DIM),
                                gl[:, :3 * H].reshape(Bd, H, 3), P)
            os_, zs = o.reshape(Bd, D), z
            st.setdefault("a_cmp_s", []).append(cmp_.reshape(Bd, 1, 2, A_KV, HEAD_DIM))
            st.setdefault("a_sel_s", []).append(sel.reshape(Bd, 1, 2, A_KV, HEAD_DIM))
            st.setdefault("a_win_s", []).append(nwin.reshape(Bd, A_WINDOW, 2, A_KV, HEAD_DIM))
        elif kind == 1:
            w = _pad_cols(b_w_in[li], [(0, 2112, 2176), (2112, 8, 128), (2120, 1024, 1024)])
            segs = [(0, 1024, ("r",)), (1024, 512, ("rk",)), (1536, 512, ("r",)), (2048, 128, ("r",)),
                    (2176, 128, ("n",)), (2304, 1024, ("n",))]
            q, kv, qi, ki, wi, z = proj_p(w, segs)
            o = _dsa_prompt(q.reshape(B, T, D), qi.reshape(B, T, 512), wi.reshape(B, T, LANES), kv.reshape(B, T, 512),
                            ki.reshape(B, T, LANES), min(B_TOPK_MAX, T // 4), PASSES)
            op, zp, wo = o.reshape(B * T, D), z, b_w_out[li]
            st.setdefault("b_kv_p", []).append(kv.reshape(B, T, 2, A_KV, HEAD_DIM))
            st.setdefault("b_kidx_p", []).append(ki[:, :B_IDX_DIM].reshape(B, T, B_IDX_DIM))

            q, kv, qi, ki, wi, z = proj_s(w, segs)
            qi3 = qi.reshape(Bd, B_IDX_HEADS, B_IDX_DIM)
            wi3 = wi[:, :B_IDX_HEADS].reshape(Bd, B_IDX_HEADS, 1)
            sc = _dsa_sample_scores(cache_b_kidx, li, page_table, qi3, wi3)
            mask = _dsa_sample_select(sc.reshape(Bd, P), qi3, wi3, ki[:, :B_IDX_DIM].reshape(Bd, 1, B_IDX_DIM),
                                      min(B_TOPK_MAX, (P + 1) // 4))
            o = _dsa_sample_attn(cache_b_kv.reshape(cache_b_kv.shape[0], pool, PAGE, 512), li, page_table,
                                 mask[:, :P].reshape(Bd, NP, 1, PAGE), _group_major(q, A_KV, 4).reshape(Bd, 1, D),
                                 kv.reshape(Bd, 1, 512), jnp.broadcast_to(mask[:, P:P + 1], (Bd, LANES)).reshape(Bd, 1, LANES))
            os_, zs = _head_major(o.reshape(Bd, D), A_KV, 4), z
            st.setdefault("b_kv_s", []).append(kv.reshape(Bd, 1, 2, A_KV, HEAD_DIM))
            st.setdefault("b_kidx_s", []).append(ki[:, :B_IDX_DIM].reshape(Bd, 1, B_IDX_DIM))
        elif kind == 2:
            w = c_w_in[li]
            Wd = C_HEADS * HEAD_DIM
            segs = []
            for g in range(len(C_GROUPS)):
                segs += [(3 * g * Wd, Wd, ("r",)), ((3 * g + 1) * Wd, 2 * Wd, ("rk",))]
            segs.append((9 * Wd, Wd, ("n",)))
            res = proj_p(w, segs)
            qs, kvs, z = res[0:6:2], res[1:6:2], res[6]
            o = _dil_prompt([t.reshape(B, T, Wd) for t in qs], [t.reshape(B, T, 2 * Wd) for t in kvs], PASSES)
            op, zp, wo = o.reshape(B * T, Wd), z, c_w_out[li]
            for g, (wg, _) in enumerate(C_GROUPS):
                st.setdefault("c_win%d_p" % g, []).append(kvs[g].reshape(B, T, 2, C_HEADS, HEAD_DIM)[:, -min(wg, T):])

            res = proj_s(w, segs)
            qs, kvs, z = res[0:6:2], res[1:6:2], res[6]
            bufs3 = [b[li].reshape(Bd, b.shape[2], 2 * Wd) for b in c_bufs]
            news3 = [t.reshape(Bd, 1, 2 * Wd) for t in kvs]
            o = _dil_sample(bufs3, news3, [t.reshape(Bd, 1, Wd) for t in qs], P)
            os_, zs = o.reshape(Bd, Wd), z
            for g in range(len(C_GROUPS)):
                nb = _shift_window(bufs3[g], news3[g])
                st.setdefault("c_win%d_s" % g, []).append(nb.reshape(Bd, nb.shape[1], 2, C_HEADS, HEAD_DIM))
        else:
            w = _pad_cols(d_w_in[li], [(0, 3088, 3200), (3088, 1024, 1024)])
            segs = [(0, 1024, ("n",)), (1024, 2048, ("n",)), (3072, 128, ("n",)), (3200, 1024, ("n",))]
            fb = jnp.pad(d_f_bias[li], (0, LANES - H)).reshape(1, LANES)
            q, kv, f, z = proj_p(w, segs)
            lf, csum = _fox_logf(f.reshape(B, T, LANES), fb)
            o = _fox_prompt(q.reshape(B, T, D), kv.reshape(B, T, 2 * D), csum, csum[:, :, :H].transpose(0, 2, 1), PASSES)
            op, zp, wo = o.reshape(B * T, D), z, d_w_out[li]
            st.setdefault("d_kv_p", []).append(kv.reshape(B, T, 2, H, HEAD_DIM))
            st.setdefault("d_logf_p", []).append(lf[:, :, :H])

            q, kv, f, z = proj_s(w, segs)
            lf = _fox_logf(f.reshape(1, Bd, LANES), fb)[0].reshape(Bd, 1, LANES)
            lf_new = jnp.where(jnp.arange(LANES) < H, lf, 0.0)
            o = _fox_sample(cache_d_kv.reshape(cache_d_kv.shape[0], pool, PAGE, 2 * D), cache_d_logf.transpose(0, 1, 3, 2),
                            li, page_table, q.reshape(Bd, 1, D), kv.reshape(Bd, 1, 2 * D), lf_new)
            os_, zs = o.reshape(Bd, D), z
            st.setdefault("d_kv_s", []).append(kv.reshape(Bd, 1, 2, H, HEAD_DIM))
            st.setdefault("d_logf_s", []).append(lf[:, :, :H].reshape(Bd, 1, H))
        fg = final_g if i == depth - 1 else None
        xp, yp = _gated_out(op, zp, wo, xp, gate_p, tpb, tm_p, fg)
        xs, ys = _gated_out(os_, zs, wo, xs, gate_s, 1, Bd, fg)
    ns = {n: jnp.stack(v) for n, v in st.items()}
    names = ("a_cmp_p", "a_cmp_s", "a_sel_p", "a_sel_s", "a_win_p", "a_win_s", "b_kv_p", "b_kv_s", "b_kidx_p", "b_kidx_s",
             "c_win0_p", "c_win0_s", "c_win1_p", "c_win1_s", "c_win2_p", "c_win2_s", "d_kv_p", "d_kv_s", "d_logf_p", "d_logf_s")
    return (yp.reshape(B, T, D), ys.reshape(Bd, 1, D)) + tuple(ns[n] for n in names)
```

```python
import functools

import jax
import jax.numpy as jnp
from jax import lax
from jax.experimental import pallas as pl
from jax.experimental.pallas import tpu as pltpu

F32 = jnp.float32
BF16 = jnp.bfloat16
I32 = jnp.int32

HEAD_DIM = 64
ROPE_DIMS = HEAD_DIM // 4
ROPE_HALF = ROPE_DIMS // 2
ROPE_THETA = 500000.0
NORM_EPS = 1e-6
NEG_INF = -1e30
TINY = 1e-30
PAGE = 128
LANES = 128
INT_MIN = -2 ** 31

A_KV = 4
A_CMP_STRIDE = 16
A_CMP_LEN = 32
A_SEL_LEN = 64
A_SEL_TOPK = 16
A_WINDOW = 512
A_FORCED = 1e4
B_IDX_HEADS = 8
B_IDX_DIM = 64
B_TOPK_MAX = 256
C_GROUPS = ((128, 1), (512, 4), (2048, 16))
C_HEADS = 8
QK_SCALE = HEAD_DIM ** -0.5

NN = ((1,), (0,))
NT = ((1,), (1,))

VMEM_LIMIT = 56 * 1024 * 1024
PROJ_COLS_PER_CALL = 2048

PASSES_SELECTIVE = 3
PASSES_SMOOTH = 1


def _cp(*sem):
    return pltpu.CompilerParams(dimension_semantics=sem, vmem_limit_bytes=VMEM_LIMIT)


def _dotf(a, b, dims=NN):
    return lax.dot_general(a, b, (dims, ((), ())), preferred_element_type=F32)


def _split(a, terms):
    out = []
    for _ in range(terms - 1):
        h = a.astype(BF16)
        out.append(h)
        a = a - h.astype(F32)
    out.append(a.astype(BF16))
    return out


def _dot3(a, b, dims=NN):
    ah, al = _split(a, 2)
    bh, bl = _split(b, 2)
    return _dotf(ah, bh, dims) + (_dotf(ah, bl, dims) + _dotf(al, bh, dims))


def _dotp(a, b, dims, passes):
    if passes == 1:
        return _dotf(a.astype(BF16), b.astype(BF16), dims)
    return _dot3(a, b, dims)


def _dotx(a, m01, terms, dims=NN):
    acc = None
    for t in _split(a, terms):
        d = _dotf(t, m01, dims)
        acc = d if acc is None else acc + d
    return acc


def _iota(shape, axis):
    return lax.broadcasted_iota(I32, shape, axis)


def _onehot(cond):
    return jnp.where(cond, 1.0, 0.0).astype(BF16)


def _sigmoid(x):
    return 1.0 / (1.0 + jnp.exp(-x))


def _sortable_key(x):
    x = jnp.where(x == 0.0, 0.0, x)
    b = lax.bitcast_convert_type(x, I32)
    return jnp.where(b < 0, b ^ 0x7FFFFFFF, b)


def _count_ge(key, cand):
    return jnp.sum(jnp.where(key >= cand, 1.0, 0.0), axis=-1, keepdims=True)


def _kth_largest_key(key, k):
    base = jnp.where(_count_ge(key, 0) >= k, 0, INT_MIN).astype(I32)

    def body(it, base):
        cand = base | jnp.left_shift(jnp.int32(1), 30 - it)
        return jnp.where(_count_ge(key, cand) >= k, cand, base)

    return lax.fori_loop(0, 31, body, base)


def _topk_mask(key, k):
    n = key.shape[1]
    thr = _kth_largest_key(key, k)
    gt = key > thr
    eq = key == thr
    need = k - jnp.sum(jnp.where(gt, 1.0, 0.0), axis=-1, keepdims=True)
    before = _onehot(_iota((LANES, LANES), 0) < _iota((LANES, LANES), 1))
    run = jnp.zeros_like(need)
    out = []
    for c in range(n // LANES):
        sl = slice(c * LANES, (c + 1) * LANES)
        eqf = jnp.where(eq[:, sl], 1.0, 0.0)
        prior = _dotf(eqf.astype(BF16), before) + run
        out.append(gt[:, sl] | (eq[:, sl] & (prior < need)))
        run = run + jnp.sum(eqf, axis=-1, keepdims=True)
    return jnp.concatenate(out, axis=1)


def _flash(q, kt_ref, vt_ref, krow, vrow, c_lo, c_hi, tk, mask_fn, passes, bias_fn=None):
    M = q.shape[0]

    def body(c, carry):
        m, l, acc = carry
        off = pl.multiple_of(c * tk, tk)
        kt = kt_ref[0, krow:krow + HEAD_DIM, pl.ds(off, tk)]
        vt = vt_ref[0, vrow:vrow + HEAD_DIM, pl.ds(off, tk)]
        s = _dotp(q, kt, NN, passes)
        if bias_fn is not None:
            s = s + bias_fn(c)
        mask = mask_fn(c)
        s = jnp.where(mask, s, NEG_INF)
        m_new = jnp.maximum(m, jnp.max(s, axis=-1, keepdims=True))
        alpha = jnp.exp(m - m_new)
        e = jnp.where(mask, jnp.exp(s - m_new), 0.0)
        l = alpha * l + jnp.sum(e, axis=-1, keepdims=True)
        acc = alpha * acc + _dotp(e, vt, NT, passes)
        return m_new, l, acc

    init = (jnp.full((M, 1), NEG_INF, F32), jnp.zeros((M, 1), F32), jnp.zeros((M, HEAD_DIM), F32))
    m, l, acc = lax.fori_loop(c_lo, c_hi, body, init)
    return acc / jnp.maximum(l, TINY), m, l


def _stack_heads(ref, first_head, n):
    return jnp.concatenate(
        [ref[0, :, (first_head + r) * HEAD_DIM:(first_head + r + 1) * HEAD_DIM] for r in range(n)], axis=0)


def _group_block_diag(qt, n_heads, n_groups):
    per = n_heads // n_groups
    shift = per.bit_length() - 1
    keep = (_iota(qt.shape, 0) >> shift) == (_iota(qt.shape, 1) >> 6)
    return jnp.where(keep, qt, 0.0), keep


def _fold_groups(x, keep, n_groups):
    x = jnp.where(keep, x, 0.0)
    return functools.reduce(lambda a, b: a + b, [x[:, g * HEAD_DIM:(g + 1) * HEAD_DIM] for g in range(n_groups)])


def _mod_kernel(c_ref, w_ref, b_ref, o_ref):
    c = c_ref[...]
    o_ref[0] = _dot3(c * _sigmoid(c), w_ref[0]) + b_ref[0]


def _mod_all(c_all, ada_w, ada_b):
    L, D, D3 = ada_w.shape
    NC = c_all.shape[0]
    tn = 1024
    return pl.pallas_call(
        _mod_kernel, grid=(L, D3 // tn),
        in_specs=[pl.BlockSpec((NC, D), lambda l, j: (0, 0)),
                  pl.BlockSpec((1, D, tn), lambda l, j: (l, 0, j)),
                  pl.BlockSpec((1, 1, tn), lambda l, j: (l, 0, j))],
        out_specs=pl.BlockSpec((1, NC, tn), lambda l, j: (l, 0, j)),
        out_shape=jax.ShapeDtypeStruct((L, NC, D3), F32),
        compiler_params=_cp("parallel", "parallel"), name="adaln_mod",
    )(c_all, ada_w, ada_b.reshape(L, 1, D3))


def _proj_kernel(*refs, rsegs, tsegs):
    x_ref, sc_ref, sh_ref, g_ref = refs[0:4]
    pos = 4
    if rsegs:
        wh_ref, wl_ref, cos_ref, sn_ref, sp_ref = refs[pos:pos + 5]
        pos += 5
    if tsegs:
        wth_ref, wtl_ref, cost_ref, sint_ref = refs[pos:pos + 4]
        pos += 4
    out_refs = refs[pos:]
    x = x_ref[...]
    y = x * lax.rsqrt(jnp.mean(x * x, axis=-1, keepdims=True) + NORM_EPS) * g_ref[...]
    h = y * (1.0 + sc_ref[0]) + sh_ref[0]
    hh, hl = _split(h, 2)
    oi = 0
    for start, width, modes in rsegs:
        wh = wh_ref[:, start:start + width]
        wl = wl_ref[:, start:start + width]
        u = _dotf(hh, wh) + (_dotf(hh, wl) + _dotf(hl, wh))
        for mode in modes:
            o_ref = out_refs[oi]
            oi += 1
            n_rope = {"n": 0, "r": width, "rk": width // 2}[mode]
            for c in range(width // LANES):
                sl = slice(c * LANES, (c + 1) * LANES)
                uc = u[:, sl]
                if c * LANES < n_rope:
                    uc = (uc * cos_ref[...] + pltpu.roll(uc, LANES - ROPE_HALF, 1) * sn_ref[...]
                          + pltpu.roll(uc, ROPE_HALF, 1) * sp_ref[...])
                o_ref[:, sl] = uc
    for start, width, mode in tsegs:
        wth = wth_ref[start:start + width, :]
        wtl = wtl_ref[start:start + width, :]
        ut = _dotf(wth, hh, NT) + (_dotf(wth, hl, NT) + _dotf(wtl, hh, NT))
        o_ref = out_refs[oi]
        oi += 1
        o_ref[0] = ut
        n_rope = {"t": 0, "tr": width, "trk": width // 2}[mode]
        for hd in range(n_rope // HEAD_DIM):
            r0 = hd * HEAD_DIM
            x1 = ut[r0:r0 + ROPE_HALF]
            x2 = ut[r0 + ROPE_HALF:r0 + ROPE_DIMS]
            o_ref[0, r0:r0 + ROPE_HALF, :] = x1 * cost_ref[...] - x2 * sint_ref[...]
            o_ref[0, r0 + ROPE_HALF:r0 + ROPE_DIMS, :] = x2 * cost_ref[...] + x1 * sint_ref[...]


def _proj(x2, scale, shift, g, w, segs, tables, ttables, tiles_per_batch, tm):
    R, D = x2.shape
    NB, RB = scale.shape[0], scale.shape[1]
    Tt = R // NB
    groups, cur, cur_w = [], [], 0
    for seg in segs:
        wd = -(-seg[2] // LANES) * LANES
        if cur and cur_w + wd > PROJ_COLS_PER_CALL:
            groups.append(cur)
            cur, cur_w = [], 0
        cur.append(seg)
        cur_w += wd
    groups.append(cur)
    results = {}
    for grp in groups:
        rsegs, tsegs, rcols, tcols = [], [], [], []
        out_shapes, out_specs, keys = [], [], []
        rpos = tpos = 0
        for si, (kind, c0, wd, spec) in enumerate(grp):
            if kind == "rm":
                pw = -(-wd // LANES) * LANES
                rcols.append(w[:, c0:c0 + wd])
                if pw > wd:
                    rcols.append(jnp.zeros((D, pw - wd), F32))
                rsegs.append((rpos, pw, spec))
                rpos += pw
                for mi in range(len(spec)):
                    out_shapes.append(jax.ShapeDtypeStruct((R, pw), F32))
                    out_specs.append(pl.BlockSpec((tm, pw), lambda i: (i, 0)))
                    keys.append((id(grp), si, mi))
        for si, (kind, c0, wd, spec) in enumerate(grp):
            if kind == "t":
                tcols.append(w[:, c0:c0 + wd])
                tsegs.append((tpos, wd, spec))
                tpos += wd
                out_shapes.append(jax.ShapeDtypeStruct((NB, wd, Tt), F32))
                out_specs.append(pl.BlockSpec((1, wd, tm), lambda i: (i // tiles_per_batch, 0, i % tiles_per_batch)))
                keys.append((id(grp), si, 0))
        mod_spec = pl.BlockSpec((1, RB, D), lambda i: (i // tiles_per_batch, 0, 0))
        in_specs = [pl.BlockSpec((tm, D), lambda i: (i, 0)), mod_spec, mod_spec, pl.BlockSpec((1, D), lambda i: (0, 0))]
        args = [x2, scale, shift, g.reshape(1, D)]
        if rsegs:
            wr = jnp.concatenate(rcols, axis=1)
            wh = wr.astype(BF16)
            wl = (wr - wh.astype(F32)).astype(BF16)
            tab_spec = pl.BlockSpec((tm, LANES), lambda i: (i % tiles_per_batch, 0))
            in_specs += [pl.BlockSpec((D, rpos), lambda i: (0, 0))] * 2 + [tab_spec] * 3
            args += [wh, wl, *tables]
        if tsegs:
            wt = jnp.concatenate(tcols, axis=1).T
            wth = wt.astype(BF16)
            wtl = (wt - wth.astype(F32)).astype(BF16)
            ttab_spec = pl.BlockSpec((ROPE_HALF, tm), lambda i: (0, i % tiles_per_batch))
            in_specs += [pl.BlockSpec((tpos, D), lambda i: (0, 0))] * 2 + [ttab_spec] * 2
            args += [wth, wtl, *ttables]
        res = pl.pallas_call(
            functools.partial(_proj_kernel, rsegs=tuple(rsegs), tsegs=tuple(tsegs)), grid=(R // tm,),
            in_specs=in_specs, out_specs=out_specs, out_shape=out_shapes,
            compiler_params=_cp("parallel"), name="norm_mod_proj",
        )(*args)
        for key, r in zip(keys, res):
            results[key] = r
    outs = []
    for grp in groups:
        for si, (kind, c0, wd, spec) in enumerate(grp):
            for mi in range(len(spec) if kind == "rm" else 1):
                outs.append(results[(id(grp), si, mi)])
    return outs


def _rope_angles(pos):
    inv = ROPE_THETA ** (-(jnp.arange(ROPE_HALF, dtype=F32) / ROPE_HALF))
    ang = pos.astype(F32)[:, None] * inv[None, :]
    return jnp.cos(ang), jnp.sin(ang)


def _rope_tables(pos):
    cos, sin = _rope_angles(pos)
    R = pos.shape[0]
    one = jnp.ones((R, HEAD_DIM - ROPE_DIMS), F32)
    zero = jnp.zeros((R, HEAD_DIM - ROPE_DIMS), F32)
    zh = jnp.zeros((R, ROPE_HALF), F32)
    cos_h = jnp.concatenate([cos, cos, one], axis=1)
    sn_h = jnp.concatenate([-sin, zh, zero], axis=1)
    sp_h = jnp.concatenate([zh, sin, zero], axis=1)
    rep = LANES // HEAD_DIM
    return tuple(jnp.tile(t, (1, rep)) for t in (cos_h, sn_h, sp_h))


def _rope_tables_t(pos):
    cos, sin = _rope_angles(pos)
    return cos.T, sin.T


def _out_kernel(o_ref, z_ref, wh_ref, wl_ref, x_ref, gate_ref, *rest, final):
    z = z_ref[...]
    y = o_ref[...] * (z * _sigmoid(z))
    yh, yl = _split(y, 2)
    wh, wl = wh_ref[...], wl_ref[...]
    r = _dotf(yh, wh) + (_dotf(yh, wl) + _dotf(yl, wh))
    xn = x_ref[...] + gate_ref[0] * r
    if final:
        fg_ref, xo_ref, yo_ref = rest
        yo_ref[...] = xn * lax.rsqrt(jnp.mean(xn * xn, axis=-1, keepdims=True) + NORM_EPS) * fg_ref[...]
    else:
        (xo_ref,) = rest
    xo_ref[...] = xn


def _gated_out(o2, z2, w_out, x2, gate, tiles_per_batch, tm, final_g=None):
    R, W = o2.shape
    D = x2.shape[1]
    RB = gate.shape[1]
    wh = w_out.astype(BF16)
    wl = (w_out - wh.astype(F32)).astype(BF16)
    final = final_g is not None
    in_specs = [pl.BlockSpec((tm, W), lambda i: (i, 0)), pl.BlockSpec((tm, W), lambda i: (i, 0)),
                pl.BlockSpec((W, D), lambda i: (0, 0)), pl.BlockSpec((W, D), lambda i: (0, 0)),
                pl.BlockSpec((tm, D), lambda i: (i, 0)),
                pl.BlockSpec((1, RB, D), lambda i: (i // tiles_per_batch, 0, 0))]
    args = [o2, z2, wh, wl, x2, gate]
    out_shape = [jax.ShapeDtypeStruct((R, D), F32)]
    out_specs = [pl.BlockSpec((tm, D), lambda i: (i, 0))]
    if final:
        in_specs.append(pl.BlockSpec((1, D), lambda i: (0, 0)))
        args.append(final_g.reshape(1, D))
        out_shape.append(jax.ShapeDtypeStruct((R, D), F32))
        out_specs.append(pl.BlockSpec((tm, D), lambda i: (i, 0)))
    res = pl.pallas_call(
        functools.partial(_out_kernel, final=final), grid=(R // tm,),
        in_specs=in_specs, out_specs=out_specs, out_shape=out_shape,
        compiler_params=_cp("parallel"), name="gated_out_proj",
    )(*args)
    return res if final else (res[0], None)


def _cmp_taps(wk, wv, groups):
    def table(w):
        t = jnp.tile(w.T, (1, LANES // A_CMP_STRIDE))
        return jnp.tile(t, (groups, 1))
    S = A_CMP_STRIDE
    lo = jnp.concatenate([table(wk[:S]), table(wv[:S])], axis=0)
    hi = jnp.concatenate([table(wk[S:]), table(wv[S:])], axis=0)
    return lo, hi


def _chunk_sum_matrix(first_col):
    shift = A_CMP_STRIDE.bit_length() - 1
    return _onehot((_iota((LANES, LANES), 0) >> shift) + first_col == _iota((LANES, LANES), 1))


def _cmp_kernel(x_ref, wlo_ref, whi_ref, o_ref, *, T):
    rows = x_ref.shape[1]
    per = LANES // A_CMP_STRIDE
    lo = jnp.zeros((rows, LANES), F32)
    hi = jnp.zeros((rows, LANES), F32)
    for c in range(T // LANES):
        xc = x_ref[0, :, c * LANES:(c + 1) * LANES]
        m = _chunk_sum_matrix(c * per)
        lo = lo + _dotx(xc * wlo_ref[...], m, 3)
        hi = hi + _dotx(xc * whi_ref[...], m, 3)
    o_ref[0] = lo + pltpu.roll(hi, LANES - 1, 1)


def _nsa_compress_prompt(cmp_t, wlo, whi):
    B, rows, T = cmp_t.shape
    assert T // A_CMP_STRIDE <= LANES
    return pl.pallas_call(
        functools.partial(_cmp_kernel, T=T), grid=(B,),
        in_specs=[pl.BlockSpec((1, rows, T), lambda b: (b, 0, 0)), pl.BlockSpec((rows, LANES), lambda b: (0, 0)),
                  pl.BlockSpec((rows, LANES), lambda b: (0, 0))],
        out_specs=pl.BlockSpec((1, rows, LANES), lambda b: (b, 0, 0)),
        out_shape=jax.ShapeDtypeStruct((B, rows, LANES), F32),
        compiler_params=_cp("parallel"), name="nsa_compress",
    )(cmp_t, wlo, whi)


def _nsa_prompt_kernel(q_ref, qr_ref, kvb_ref, sel_ref, win_ref, gl_ref, o_ref, *, tq, tk, T, passes):
    i = pl.program_id(1)
    t0 = i * tq
    R = 4
    nb = kvb_ref.shape[2]
    ncb = T // A_CMP_STRIDE - 1
    nsb = T // A_SEL_LEN
    gw = A_KV * HEAD_DIM
    gates = _sigmoid(gl_ref[0])

    n_io = _iota((R * tq, nb), 1)
    t_c = (_iota((R * tq, nb), 0) & (tq - 1)) + t0
    cmask = (n_io * A_CMP_STRIDE + (A_CMP_LEN - 1) <= t_c) & (n_io < ncb)

    ratio = A_SEL_LEN // A_CMP_STRIDE
    back = A_CMP_LEN // A_CMP_STRIDE - 1
    mn, mj = _iota((nb, LANES), 0), _iota((nb, LANES), 1)
    imp_mat = _onehot((mn >= mj * ratio - back) & (mn <= mj * ratio + ratio - 1) & (mn < ncb) & (mj < nsb))

    j_io = _iota((tq, LANES), 1)
    qb = (_iota((tq, LANES), 0) + t0) >> 6
    allowed = (j_io <= qb) & (j_io < nsb)
    forced = (j_io == 0) | (j_io == qb)

    trow = (_iota((R * tq, tk), 0) & (tq - 1)) + t0
    kcol = _iota((R * tq, tk), 1)
    c_hi = (t0 + tq + tk - 1) // tk
    w_lo = jnp.maximum(t0 - (A_WINDOW - 1), 0) // tk

    for g in range(A_KV):
        q4 = _stack_heads(q_ref, R * g, R) * QK_SCALE
        kb = kvb_ref[0, g * HEAD_DIM:(g + 1) * HEAD_DIM, :]
        vb = kvb_ref[0, gw + g * HEAD_DIM:gw + (g + 1) * HEAD_DIM, :]
        s = jnp.where(cmask, _dot3(q4, kb, NN), NEG_INF)
        m = jnp.max(s, axis=-1, keepdims=True)
        e = jnp.where(cmask, jnp.exp(s - m), 0.0)
        p = e / jnp.maximum(jnp.sum(e, axis=-1, keepdims=True), TINY)
        o_cmp = _dot3(p, vb, NT)
        pg = p[0:tq] + p[tq:2 * tq] + p[2 * tq:3 * tq] + p[3 * tq:4 * tq]
        imp = _dotx(pg, imp_mat, 3)
        score = jnp.where(allowed, jnp.where(forced, A_FORCED, imp), -1.0)
        score = jnp.where(j_io < nsb, score, -2.0)
        rank = jnp.zeros((tq, LANES), F32)
        for j2 in range(nsb):
            col = score[:, j2:j2 + 1]
            rank = rank + jnp.where((col > score) | ((col == score) & (j2 < j_io)), 1.0, 0.0)
        sel = _onehot((rank < min(A_SEL_TOPK, nsb)) & allowed)
        sel4 = jnp.concatenate([sel] * R, axis=0)

        def sel_mask(c):
            blk = (_iota((LANES, tk), 1) + c * tk) >> 6
            hit = _dotf(sel4, _onehot(blk == _iota((LANES, tk), 0)))
            return (hit > 0.5) & (kcol + c * tk <= trow)

        def win_mask(c):
            d = trow - (kcol + c * tk)
            return (d >= 0) & (d < A_WINDOW)

        q4r = _stack_heads(qr_ref, R * g, R) * QK_SCALE
        o_sel, _, _ = _flash(q4r, sel_ref, sel_ref, g * HEAD_DIM, gw + g * HEAD_DIM, 0, c_hi, tk, sel_mask, passes)
        o_win, _, _ = _flash(q4r, win_ref, win_ref, g * HEAD_DIM, gw + g * HEAD_DIM, w_lo, c_hi, tk, win_mask, passes)
        outs = []
        for r in range(R):
            h = R * g + r
            rows = slice(r * tq, (r + 1) * tq)
            outs.append(gates[:, 3 * h:3 * h + 1] * o_cmp[rows] + gates[:, 3 * h + 1:3 * h + 2] * o_sel[rows]
                        + gates[:, 3 * h + 2:3 * h + 3] * o_win[rows])
        o_ref[0, :, g * R * HEAD_DIM:(g + 1) * R * HEAD_DIM] = jnp.concatenate(outs, axis=1)


def _nsa_prompt(q3, qr3, kvb, sel_t, win_t, gl3, passes, tq=128, tk=512):
    B, T, D = q3.shape
    full = lambda a: pl.BlockSpec((1,) + a.shape[1:], lambda b, i: (b, 0, 0))
    return pl.pallas_call(
        functools.partial(_nsa_prompt_kernel, tq=tq, tk=tk, T=T, passes=passes), grid=(B, T // tq),
        in_specs=[pl.BlockSpec((1, tq, D), lambda b, i: (b, i, 0)), pl.BlockSpec((1, tq, D), lambda b, i: (b, i, 0)),
                  full(kvb), full(sel_t), full(win_t), pl.BlockSpec((1, tq, LANES), lambda b, i: (b, i, 0))],
        out_specs=pl.BlockSpec((1, tq, D), lambda b, i: (b, i, 0)),
        out_shape=jax.ShapeDtypeStruct((B, T, D), F32),
        compiler_params=_cp("parallel", "arbitrary"), name="nsa_prompt",
    )(q3, qr3, kvb, sel_t, win_t, gl3)


def _dsa_prompt_kernel(q_ref, qi_ref, wi_ref, kv_ref, ki_ref, o_ref, ch_ref, *, tq, tk, T, topk, passes):
    i = pl.program_id(1)
    t0 = i * tq
    R = 4
    gw = kv_ref.shape[1] // 2
    wi = wi_ref[0] * (B_IDX_HEADS * B_IDX_DIM) ** -0.5
    ki = ki_ref[0]
    score = jnp.zeros((tq, T), F32)
    for h in range(B_IDX_HEADS):
        logits = _dot3(qi_ref[0, :, h * B_IDX_DIM:(h + 1) * B_IDX_DIM], ki, NN)
        score = score + wi[:, h:h + 1] * jnp.maximum(logits, 0.0)
    causal = _iota((tq, T), 1) <= _iota((tq, T), 0) + t0
    score = jnp.where(causal, score, NEG_INF)
    ch_ref[...] = jnp.where(_topk_mask(_sortable_key(score), topk) & causal, 1.0, 0.0)
    c_hi = (t0 + tq + tk - 1) // tk

    def mask_fn(c):
        picked = ch_ref[:, pl.ds(pl.multiple_of(c * tk, tk), tk)]
        return jnp.concatenate([picked] * R, axis=0) > 0.5

    for g in range(gw // HEAD_DIM):
        q4 = _stack_heads(q_ref, R * g, R) * QK_SCALE
        o, _, _ = _flash(q4, kv_ref, kv_ref, g * HEAD_DIM, gw + g * HEAD_DIM, 0, c_hi, tk, mask_fn, passes)
        o_ref[0, :, g * R * HEAD_DIM:(g + 1) * R * HEAD_DIM] = jnp.concatenate(
            [o[r * tq:(r + 1) * tq] for r in range(R)], axis=1)


def _dsa_prompt(q3, qi3, wi3, kv_t, ki_t, topk, passes, tq=128, tk=512):
    B, T, D = q3.shape
    full = lambda a: pl.BlockSpec((1,) + a.shape[1:], lambda b, i: (b, 0, 0))
    blk = lambda w: pl.BlockSpec((1, tq, w), lambda b, i: (b, i, 0))
    return pl.pallas_call(
        functools.partial(_dsa_prompt_kernel, tq=tq, tk=tk, T=T, topk=topk, passes=passes), grid=(B, T // tq),
        in_specs=[blk(D), blk(qi3.shape[2]), blk(LANES), full(kv_t), full(ki_t)],
        out_specs=blk(D), out_shape=jax.ShapeDtypeStruct((B, T, D), F32),
        scratch_shapes=[pltpu.VMEM((tq, T), F32)],
        compiler_params=_cp("parallel", "arbitrary"), name="dsa_prompt",
    )(q3, qi3, wi3, kv_t, ki_t)


def _dil_prompt_kernel(*refs, tq, tk, T, passes):
    ng = len(C_GROUPS)
    q_refs, k_refs, v_refs, o_ref = refs[0:ng], refs[ng:2 * ng], refs[2 * ng:3 * ng], refs[3 * ng]
    i = pl.program_id(2)
    t0 = i * tq
    trow = _iota((tq, tk), 0) + t0
    kcol = _iota((tq, tk), 1)
    c_hi = (t0 + tq + tk - 1) // tk
    outs = []
    for hh in range(LANES // HEAD_DIM):
        lane = hh * HEAD_DIM
        res = []
        for (w, r), q_ref, k_ref, v_ref in zip(C_GROUPS, q_refs, k_refs, v_refs):
            def mask_fn(c, w=w, r=r):
                d = trow - (kcol + c * tk)
                return (d >= 0) & (d <= w) & ((d & (r - 1)) == 0)

            c_lo = jnp.maximum(t0 - w, 0) // tk
            q = q_ref[0, :, lane:lane + HEAD_DIM] * QK_SCALE
            res.append(_flash(q, k_ref, v_ref, lane, lane, c_lo, c_hi, tk, mask_fn, passes))
        m_all = functools.reduce(jnp.maximum, [m for _, m, _ in res])
        wts = [den * jnp.exp(m - m_all) for _, m, den in res]
        tot = functools.reduce(lambda a, b: a + b, wts)
        outs.append(functools.reduce(lambda a, b: a + b, [(wt / tot) * o for wt, (o, _, _) in zip(wts, res)]))
    o_ref[0] = jnp.concatenate(outs, axis=1)


def _dil_prompt(qs, kvs_t, passes, tq=256, tk=512):
    B, T, W = qs[0].shape
    hp = W // LANES
    qspec = pl.BlockSpec((1, tq, LANES), lambda b, h, i: (b, i, h))
    kspec = pl.BlockSpec((1, LANES, T), lambda b, h, i: (b, h, 0))
    vspec = pl.BlockSpec((1, LANES, T), lambda b, h, i: (b, hp + h, 0))
    ng = len(C_GROUPS)
    return pl.pallas_call(
        functools.partial(_dil_prompt_kernel, tq=tq, tk=tk, T=T, passes=passes), grid=(B, hp, T // tq),
        in_specs=[qspec] * ng + [kspec] * ng + [vspec] * ng,
        out_specs=qspec, out_shape=jax.ShapeDtypeStruct((B, T, W), F32),
        compiler_params=_cp("parallel", "parallel", "arbitrary"), name="dilated_prompt",
    )(*qs, *kvs_t, *kvs_t)


def _logf_kernel(f_ref, b_ref, lf_ref, c_ref, *, T, tc):
    x = f_ref[0] + b_ref[...]
    lf = jnp.minimum(x, 0.0) - jnp.log(1.0 + jnp.exp(-jnp.abs(x)))
    lf_ref[0] = lf
    parts = _split(lf, 3)
    for c in range(T // tc):
        upto = _onehot(_iota((T, tc), 0) <= _iota((T, tc), 1) + c * tc)
        c_ref[0, :, c * tc:(c + 1) * tc] = functools.reduce(lambda a, b: a + b, [_dotf(p, upto) for p in parts])


def _fox_logf(f_t, bias_col):
    B, H, T = f_t.shape
    spec = pl.BlockSpec((1, H, T), lambda b: (b, 0, 0))
    return pl.pallas_call(
        functools.partial(_logf_kernel, T=T, tc=min(256, T)), grid=(B,),
        in_specs=[spec, pl.BlockSpec((H, 1), lambda b: (0, 0))],
        out_specs=[spec, spec], out_shape=[jax.ShapeDtypeStruct((B, H, T), F32)] * 2,
        compiler_params=_cp("parallel"), name="fox_logf_cumsum",
    )(f_t, bias_col)


def _fox_prompt_kernel(q_ref, k_ref, v_ref, cc_ref, cr_ref, o_ref, *, tq, tk, T, passes):
    hp = pl.program_id(1)
    i = pl.program_id(2)
    t0 = i * tq
    trow = _iota((tq, tk), 0) + t0
    kcol = _iota((tq, tk), 1)
    c_hi = (t0 + tq + tk - 1) // tk
    nh = cc_ref.shape[2]
    outs = []
    for hh in range(LANES // HEAD_DIM):
        h = hp * (LANES // HEAD_DIM) + hh
        lane = hh * HEAD_DIM
        c_col = jnp.sum(jnp.where(_iota((tq, nh), 1) == h, cc_ref[0], 0.0), axis=-1, keepdims=True)

        def bias_fn(c, h=h, c_col=c_col):
            off = pl.multiple_of(c * tk, tk)
            return c_col - cr_ref[0, pl.ds(h, 1), pl.ds(off, tk)]

        def mask_fn(c):
            return kcol + c * tk <= trow

        q = q_ref[0, :, lane:lane + HEAD_DIM] * QK_SCALE
        o, _, _ = _flash(q, k_ref, v_ref, lane, lane, 0, c_hi, tk, mask_fn, passes, bias_fn)
        outs.append(o)
    o_ref[0] = jnp.concatenate(outs, axis=1)


def _fox_prompt(q3, kv_t, c_col, c_row, passes, tq=256, tk=512):
    B, T, W = q3.shape
    hp = W // LANES
    nh = c_row.shape[1]
    return pl.pallas_call(
        functools.partial(_fox_prompt_kernel, tq=tq, tk=tk, T=T, passes=passes), grid=(B, hp, T // tq),
        in_specs=[pl.BlockSpec((1, tq, LANES), lambda b, h, i: (b, i, h)),
                  pl.BlockSpec((1, LANES, T), lambda b, h, i: (b, h, 0)),
                  pl.BlockSpec((1, LANES, T), lambda b, h, i: (b, hp + h, 0)),
                  pl.BlockSpec((1, tq, nh), lambda b, h, i: (b, i, 0)),
                  pl.BlockSpec((1, nh, T), lambda b, h, i: (b, 0, 0))],
        out_specs=pl.BlockSpec((1, tq, LANES), lambda b, h, i: (b, i, h)),
        out_shape=jax.ShapeDtypeStruct((B, T, W), F32),
        compiler_params=_cp("parallel", "parallel", "arbitrary"), name="fox_prompt",
    )(q3, kv_t, kv_t, c_col, c_row)


def _page_specs(n, block, layer, index_fn, lead=()):
    specs = []
    for pi in range(n):
        def imap(b, s, *pf, pi=pi):
            return (layer, index_fn(b, s, pi, *pf)) + lead + (0,) * (len(block) - 2 - len(lead))
        specs.append(pl.BlockSpec(block, imap))
    return specs


def _tdec_init(m_ref, l_ref, acc_ref):
    m_ref[...] = jnp.full(m_ref.shape, NEG_INF, F32)
    l_ref[...] = jnp.zeros(l_ref.shape, F32)
    acc_ref[...] = jnp.zeros(acc_ref.shape, F32)


def _tdec_update(kt, vt, qexp, bias, valid, m_ref, l_ref, acc_ref, r):
    G = kt.shape[0] // HEAD_DIM
    s = jnp.sum((kt * qexp).reshape(G, HEAD_DIM, LANES), axis=1)
    if bias is not None:
        s = s + bias
    if valid is not None:
        s = jnp.where(valid, s, NEG_INF)
    m_old = m_ref[r]
    m_new = jnp.maximum(m_old, jnp.max(s, axis=-1, keepdims=True))
    alpha = jnp.exp(m_old - m_new)
    e = jnp.exp(s - m_new)
    if valid is not None:
        e = jnp.where(valid, e, 0.0)
    l_ref[r] = alpha * l_ref[r] + e
    acc_ref[r] = acc_ref[r] * alpha[:, :, None] + vt.reshape(G, HEAD_DIM, LANES) * e[:, None, :]
    m_ref[r] = m_new


def _tdec_finish(l_ref, acc_ref, r):
    den = jnp.maximum(jnp.sum(l_ref[r], axis=-1, keepdims=True), TINY)
    o = jnp.sum(acc_ref[r], axis=-1, keepdims=True) / den[:, :, None]
    return jnp.broadcast_to(o, acc_ref.shape[1:])


def _tdec_scratch(R, G):
    return [pltpu.VMEM((R, G, 1), F32), pltpu.VMEM((R, G, LANES), F32), pltpu.VMEM((R, G, HEAD_DIM, LANES), F32)]


def _scmp_kernel(pt_ref, *refs, pp):
    pages, wlo_ref, whi_ref, o_ref = refs[:pp], refs[pp], refs[pp + 1], refs[pp + 2]
    rows = wlo_ref.shape[0]
    per = PAGE // A_CMP_STRIDE
    lo = jnp.zeros((rows, LANES), F32)
    hi = jnp.zeros((rows, LANES), F32)
    for pi in range(pp):
        x = pages[pi][...]
        m = _chunk_sum_matrix(pi * per)
        lo = lo + _dotx(x * wlo_ref[...], m, 3)
        hi = hi + _dotx(x * whi_ref[...], m, 3)
    o_ref[0, 0:rows, :] = lo
    o_ref[0, rows:2 * rows, :] = hi


def _nsa_compress_sample(cache_t, layer, page_table, wlo, whi):
    Bd, NP = page_table.shape
    rows = cache_t.shape[2]
    pp = LANES * A_CMP_STRIDE // PAGE
    grid_spec = pltpu.PrefetchScalarGridSpec(
        num_scalar_prefetch=1, grid=(Bd, NP // pp),
        in_specs=_page_specs(pp, (None, None, rows, PAGE), layer, lambda b, s, pi, pt: pt[b, s * pp + pi])
        + [pl.BlockSpec((rows, LANES), lambda b, s, pt: (0, 0))] * 2,
        out_specs=pl.BlockSpec((1, 2 * rows, LANES), lambda b, s, pt: (b, 0, s)))
    return pl.pallas_call(
        functools.partial(_scmp_kernel, pp=pp), grid_spec=grid_spec,
        out_shape=jax.ShapeDtypeStruct((Bd, 2 * rows, NP * PAGE // A_CMP_STRIDE), F32),
        compiler_params=_cp("parallel", "arbitrary"), name="nsa_compress_paged",
    )(page_table, *([cache_t] * pp), wlo, whi)


def _nsa_sample_cmp_kernel(lohi_ref, qt_ref, o_ref, idx_ref, *, P, ncols):
    rows = lohi_ref.shape[1] // 2
    gw = rows // 2
    nch = lohi_ref.shape[2]
    H = qt_ref.shape[1]
    R = H // A_KV
    lpad = -(-(P + 1) // A_SEL_LEN) * A_SEL_LEN
    ncb = lpad // A_CMP_STRIDE - 1
    nsb = lpad // A_SEL_LEN
    qb = P // A_SEL_LEN
    lohi = lohi_ref[0]
    kvb = lohi[0:rows] + pltpu.roll(lohi[rows:2 * rows], nch - 1, 1)
    qbd, keep = _group_block_diag(qt_ref[0] * QK_SCALE, H, A_KV)
    n_io = _iota((H, nch), 1)
    valid = (n_io * A_CMP_STRIDE + (A_CMP_LEN - 1) <= P) & (n_io < ncb)
    s = jnp.where(valid, _dot3(qbd, kvb[0:gw], NN), NEG_INF)
    m = jnp.max(s, axis=-1, keepdims=True)
    e = jnp.where(valid, jnp.exp(s - m), 0.0)
    p = e / jnp.maximum(jnp.sum(e, axis=-1, keepdims=True), TINY)
    o_ref[0] = _fold_groups(_dot3(p, kvb[gw:rows], NT), keep, A_KV)
    ratio = A_SEL_LEN // A_CMP_STRIDE
    back = A_CMP_LEN // A_CMP_STRIDE - 1
    mn, mj = _iota((nch, ncols), 0), _iota((nch, ncols), 1)
    imp_mat = _onehot((mn >= mj * ratio - back) & (mn <= mj * ratio + ratio - 1) & (mn < ncb))
    j_io = _iota((8, ncols), 1)
    jf = j_io.astype(F32)
    lane = _iota((8, LANES), 1)
    row = _iota((8, LANES), 0)
    out = jnp.zeros((8, LANES), F32)
    for g in range(A_KV):
        pg = jnp.sum(p[R * g:R * (g + 1)], axis=0, keepdims=True)
        imp = _dotx(jnp.broadcast_to(pg, (8, nch)), imp_mat, 3)
        score = jnp.where(j_io <= qb, jnp.where((j_io == 0) | (j_io == qb), A_FORCED, imp), -1.0)
        score = jnp.where(j_io < nsb, score, -2.0)
        for k in range(min(A_SEL_TOPK, nsb)):
            best = jnp.max(score, axis=-1, keepdims=True)
            pick = jnp.min(jnp.where(score == best, jf, 1e9), axis=-1, keepdims=True)
            out = jnp.where((lane == k) & (row == g), pick, out)
            score = jnp.where(jf == pick, -3.0, score)
    idx_ref[0] = out.astype(I32)


def _nsa_sample_cmp(lohi, qt3, P):
    Bd, rows2, nch = lohi.shape
    H, gw = qt3.shape[1], qt3.shape[2]
    nsb = -(-(P + 1) // A_SEL_LEN)
    ncols = -(-nsb // LANES) * LANES
    return pl.pallas_call(
        functools.partial(_nsa_sample_cmp_kernel, P=P, ncols=ncols), grid=(Bd,),
        in_specs=[pl.BlockSpec((1, rows2, nch), lambda b: (b, 0, 0)), pl.BlockSpec((1, H, gw), lambda b: (b, 0, 0))],
        out_specs=[pl.BlockSpec((1, H, HEAD_DIM), lambda b: (b, 0, 0)), pl.BlockSpec((1, 8, LANES), lambda b: (b, 0, 0))],
        out_shape=[jax.ShapeDtypeStruct((Bd, H, HEAD_DIM), F32), jax.ShapeDtypeStruct((Bd, 8, LANES), I32)],
        compiler_params=_cp("parallel"), name="nsa_sample_cmp_select",
    )(lohi, qt3)


def _nsa_sample_sel_kernel(pt_ref, ix_ref, *refs, P, nk):
    kblks, vblks = refs[0:A_KV], refs[A_KV:2 * A_KV]
    q_ref, kn_ref, vn_ref, ocmp_ref, owin_ref, gl_ref, o_ref, m_ref, l_ref, acc_ref = refs[2 * A_KV:]
    b = pl.program_id(0)
    k = pl.program_id(1)
    H = q_ref.shape[1]
    R = H // A_KV
    qb = P // A_SEL_LEN
    half_shift = A_SEL_LEN.bit_length() - 1

    @pl.when(k == 0)
    def _():
        m_ref[...] = jnp.full(m_ref.shape, NEG_INF, F32)
        l_ref[...] = jnp.zeros(l_ref.shape, F32)
        acc_ref[...] = jnp.zeros(acc_ref.shape, F32)

    for g in range(A_KV):
        j = ix_ref[b, g * nk + k]
        rows = slice(R * g, R * (g + 1))
        s = _dot3(q_ref[0, rows, :] * QK_SCALE, kblks[g][...], NN)
        valid = ((_iota((R, PAGE), 1) >> half_shift) == (j & (PAGE // A_SEL_LEN - 1))) & (j < qb)
        s = jnp.where(valid, s, NEG_INF)
        m_old = m_ref[rows, :]
        m_new = jnp.maximum(m_old, jnp.max(s, axis=-1, keepdims=True))
        alpha = jnp.exp(m_old - m_new)
        e = jnp.where(valid, jnp.exp(s - m_new), 0.0)
        l_ref[rows, :] = alpha * l_ref[rows, :] + jnp.sum(e, axis=-1, keepdims=True)
        acc_ref[rows, :] = alpha * acc_ref[rows, :] + _dot3(e, vblks[g][...], NT)
        m_ref[rows, :] = m_new

    @pl.when(k == nk - 1)
    def _():
        hrow = _iota((H, 1), 0)
        has_new = jnp.zeros((H, 1), jnp.bool_)
        for g in range(A_KV):
            hit = ix_ref[b, g * nk] == qb
            for kk in range(1, nk):
                hit = hit | (ix_ref[b, g * nk + kk] == qb)
            has_new = has_new | (((hrow >> (R.bit_length() - 1)) == g) & hit)
        s_new = jnp.sum(q_ref[0] * kn_ref[0], axis=-1, keepdims=True) * QK_SCALE
        m_old = m_ref[...]
        m_new = jnp.where(has_new, jnp.maximum(m_old, s_new), m_old)
        alpha = jnp.exp(m_old - m_new)
        e_new = jnp.where(has_new, jnp.exp(s_new - m_new), 0.0)
        den = alpha * l_ref[...] + e_new
        o_sel = (alpha * acc_ref[...] + e_new * vn_ref[0]) / jnp.maximum(den, TINY)
        gates = _sigmoid(gl_ref[0])
        o_ref[0] = gates[:, 0:1] * ocmp_ref[0] + gates[:, 1:2] * o_sel + gates[:, 2:3] * owin_ref[0]


def _nsa_sample_sel(cache_t6, layer, page_table, idx_flat, q3, kn3, vn3, ocmp3, owin3, gl3, P):
    Bd, H, dh = q3.shape
    nk = idx_flat.shape[1] // A_KV
    per = PAGE // A_SEL_LEN
    last = P // A_SEL_LEN - 1

    def blk_index(c, g):
        def imap(b, k, pt, ix):
            j = jnp.minimum(ix[b, g * nk + k], last)
            return (layer, pt[b, j // per], c, g, 0, 0)
        return imap

    per_b = lambda shape: pl.BlockSpec((1,) + shape, lambda b, k, pt, ix: (b, 0, 0))
    blk = (None, None, None, None, HEAD_DIM, PAGE)
    grid_spec = pltpu.PrefetchScalarGridSpec(
        num_scalar_prefetch=2, grid=(Bd, nk),
        in_specs=[pl.BlockSpec(blk, blk_index(0, g)) for g in range(A_KV)]
        + [pl.BlockSpec(blk, blk_index(1, g)) for g in range(A_KV)]
        + [per_b((H, dh))] * 5 + [per_b((H, 3))],
        out_specs=per_b((H, dh)),
        scratch_shapes=[pltpu.VMEM((H, 1), F32), pltpu.VMEM((H, 1), F32), pltpu.VMEM((H, dh), F32)])
    return pl.pallas_call(
        functools.partial(_nsa_sample_sel_kernel, P=P, nk=nk), grid_spec=grid_spec,
        out_shape=jax.ShapeDtypeStruct((Bd, H, dh), F32),
        compiler_params=_cp("parallel", "arbitrary"), name="nsa_sample_selected",
    )(page_table, idx_flat, *([cache_t6] * (2 * A_KV)), q3, kn3, vn3, ocmp3, owin3, gl3)


def _slide_lanes(buf, new_cols):
    W = buf.shape[-1]
    axis = buf.ndim - 1
    new = jnp.concatenate([new_cols] * (W // LANES), axis=axis)
    return jnp.where(_iota(buf.shape, axis) == W - 1, new, pltpu.roll(buf, W - 1, axis))


def _nsa_sample_win_kernel(buf_ref, new_ref, qt_ref, nbuf_ref, o_ref):
    rows = buf_ref.shape[1]
    gw = rows // 2
    H = qt_ref.shape[1]
    nb = _slide_lanes(buf_ref[0], new_ref[0])
    nbuf_ref[0] = nb
    qbd, keep = _group_block_diag(qt_ref[0] * QK_SCALE, H, A_KV)
    s = _dot3(qbd, nb[0:gw], NN)
    e = jnp.exp(s - jnp.max(s, axis=-1, keepdims=True))
    p = e / jnp.maximum(jnp.sum(e, axis=-1, keepdims=True), TINY)
    o_ref[0] = _fold_groups(_dot3(p, nb[gw:rows], NT), keep, A_KV)


def _nsa_sample_win(buf_t, new_cols, qt3):
    Bd, rows, W = buf_t.shape
    H, gw = qt3.shape[1], qt3.shape[2]
    return pl.pallas_call(
        _nsa_sample_win_kernel, grid=(Bd,),
        in_specs=[pl.BlockSpec((1, rows, W), lambda b: (b, 0, 0)), pl.BlockSpec((1, rows, LANES), lambda b: (b, 0, 0)),
                  pl.BlockSpec((1, H, gw), lambda b: (b, 0, 0))],
        out_specs=[pl.BlockSpec((1, rows, W), lambda b: (b, 0, 0)), pl.BlockSpec((1, H, HEAD_DIM), lambda b: (b, 0, 0))],
        out_shape=[jax.ShapeDtypeStruct((Bd, rows, W), F32), jax.ShapeDtypeStruct((Bd, H, HEAD_DIM), F32)],
        compiler_params=_cp("parallel"), name="nsa_sample_window",
    )(buf_t, new_cols, qt3)


def _dsa_idx_kernel(pt_ref, *refs, pp):
    pages, qi_ref, wi_ref, o_ref = refs[:pp], refs[pp], refs[pp + 1], refs[pp + 2]
    wi = wi_ref[0] * (B_IDX_HEADS * B_IDX_DIM) ** -0.5
    qi = qi_ref[0]
    for pi in range(pp):
        logits = _dot3(qi, pages[pi][...], NN)
        o_ref[0, pi] = jnp.sum(wi * jnp.maximum(logits, 0.0), axis=0, keepdims=True)


def _dsa_sample_scores(kidx_t, layer, page_table, qi3, wi3, pp=8):
    Bd, NP = page_table.shape
    grid_spec = pltpu.PrefetchScalarGridSpec(
        num_scalar_prefetch=1, grid=(Bd, NP // pp),
        in_specs=_page_specs(pp, (None, None, B_IDX_DIM, PAGE), layer, lambda b, s, pi, pt: pt[b, s * pp + pi])
        + [pl.BlockSpec((1, B_IDX_HEADS, B_IDX_DIM), lambda b, s, pt: (b, 0, 0)),
           pl.BlockSpec((1, B_IDX_HEADS, 1), lambda b, s, pt: (b, 0, 0))],
        out_specs=pl.BlockSpec((1, pp, 1, PAGE), lambda b, s, pt: (b, s, 0, 0)))
    return pl.pallas_call(
        functools.partial(_dsa_idx_kernel, pp=pp), grid_spec=grid_spec,
        out_shape=jax.ShapeDtypeStruct((Bd, NP, 1, PAGE), F32),
        compiler_params=_cp("parallel", "arbitrary"), name="dsa_sample_indexer",
    )(page_table, *([kidx_t] * pp), qi3, wi3)


def _dsa_select_kernel(sc_ref, qi_ref, wi_ref, kin_ref, o_ref, *, topk):
    Bd, P = sc_ref.shape
    wi = wi_ref[...] * (B_IDX_HEADS * B_IDX_DIM) ** -0.5
    logit = jnp.sum(qi_ref[...] * kin_ref[...], axis=-1, keepdims=True)
    s_new = jnp.sum(wi * jnp.maximum(logit, 0.0), axis=1)
    tail = jnp.where(_iota((Bd, LANES), 1) == 0, s_new, -jnp.inf)
    full = jnp.concatenate([sc_ref[...], tail], axis=1)
    o_ref[...] = jnp.where(_topk_mask(_sortable_key(full), topk), 1.0, 0.0)


def _dsa_sample_select(scores2, qi3, wi3, kinew3, topk):
    Bd, P = scores2.shape
    whole = lambda shape: pl.BlockSpec(shape, lambda i: (0,) * len(shape))
    return pl.pallas_call(
        functools.partial(_dsa_select_kernel, topk=topk), grid=(1,),
        in_specs=[whole(scores2.shape), whole(qi3.shape), whole(wi3.shape), whole(kinew3.shape)],
        out_specs=whole((Bd, P + LANES)), out_shape=jax.ShapeDtypeStruct((Bd, P + LANES), F32),
        compiler_params=_cp("arbitrary"), name="dsa_sample_topk",
    )(scores2, qi3, wi3, kinew3)


def _dsa_sample_attn_kernel(pt_ref, *refs, pp, nsteps):
    pages = refs[:pp]
    mask_ref, q_ref, new_ref, mnew_ref, o_ref, m_ref, l_ref, acc_ref = refs[pp:]
    s_id = pl.program_id(1)
    R = q_ref.shape[1]
    gw = new_ref.shape[1] // 2

    @pl.when(s_id == 0)
    def _():
        _tdec_init(m_ref, l_ref, acc_ref)

    for pi in range(pp):
        kv = pages[pi][...]
        valid = mask_ref[0, pi] > 0.5
        for r in range(R):
            _tdec_update(kv[0:gw], kv[gw:2 * gw], q_ref[0, r] * QK_SCALE, None, valid, m_ref, l_ref, acc_ref, r)

    @pl.when(s_id == nsteps - 1)
    def _():
        new = new_ref[0]
        valid = (_iota((1, LANES), 1) == 0) & (mnew_ref[0] > 0.5)
        for r in range(R):
            _tdec_update(new[0:gw], new[gw:2 * gw], q_ref[0, r] * QK_SCALE, None, valid, m_ref, l_ref, acc_ref, r)
            o_ref[0, r] = _tdec_finish(l_ref, acc_ref, r)


def _dsa_sample_attn(kv_t, layer, page_table, mask4, qexp, new_cols, mnew3, pp=8):
    Bd, NP = page_table.shape
    rows = kv_t.shape[2]
    R, gw = qexp.shape[1], qexp.shape[2]
    G = gw // HEAD_DIM
    nsteps = NP // pp
    per_b = lambda shape: pl.BlockSpec((1,) + shape, lambda b, s, pt: (b,) + (0,) * len(shape))
    grid_spec = pltpu.PrefetchScalarGridSpec(
        num_scalar_prefetch=1, grid=(Bd, nsteps),
        in_specs=_page_specs(pp, (None, None, rows, PAGE), layer, lambda b, s, pi, pt: pt[b, s * pp + pi])
        + [pl.BlockSpec((1, pp, 1, PAGE), lambda b, s, pt: (b, s, 0, 0)), per_b((R, gw, LANES)), per_b((rows, LANES)),
           per_b((1, LANES))],
        out_specs=per_b((R, G, HEAD_DIM, LANES)),
        scratch_shapes=_tdec_scratch(R, G))
    return pl.pallas_call(
        functools.partial(_dsa_sample_attn_kernel, pp=pp, nsteps=nsteps), grid_spec=grid_spec,
        out_shape=jax.ShapeDtypeStruct((Bd, R, G, HEAD_DIM, LANES), F32),
        compiler_params=_cp("parallel", "arbitrary"), name="dsa_sample_attention",
    )(page_table, *([kv_t] * pp), mask4, qexp, new_cols, mnew3)


def _dil_sample_kernel(*refs, P):
    ng = len(C_GROUPS)
    buf_refs, new_refs, q_refs = refs[0:ng], refs[ng:2 * ng], refs[2 * ng:3 * ng]
    nbuf_refs, o_ref = refs[3 * ng:4 * ng], refs[4 * ng]
    res = []
    for (w, r), buf_ref, new_ref, q_ref, nbuf_ref in zip(C_GROUPS, buf_refs, new_refs, q_refs, nbuf_refs):
        buf = buf_ref[...]
        new = new_ref[...]
        W = buf.shape[2]
        nbuf_ref[...] = _slide_lanes(buf, new)
        q = q_ref[...] * QK_SCALE
        qw = jnp.concatenate([q] * (W // LANES), axis=1)
        s_old = jnp.sum(buf[0] * qw, axis=0, keepdims=True)
        dist = W - _iota((1, W), 1)
        valid = ((dist & (r - 1)) == 0) & (dist <= P)
        s_old = jnp.where(valid, s_old, NEG_INF)
        s_new = jnp.sum(new[0] * q, axis=0, keepdims=True)
        m = jnp.maximum(jnp.max(s_old, axis=-1, keepdims=True), s_new)
        e_old = jnp.where(valid, jnp.exp(s_old - m[:, 0:1]), 0.0)
        e_new = jnp.exp(s_new - m)
        den = jnp.sum(e_old, axis=-1, keepdims=True) + e_new
        o = (jnp.sum(buf[1] * e_old, axis=-1, keepdims=True) + new[1] * e_new) / jnp.maximum(den, TINY)
        res.append((o, m, den))
    m_all = functools.reduce(jnp.maximum, [m for _, m, _ in res])
    wts = [den * jnp.exp(m - m_all) for _, m, den in res]
    tot = functools.reduce(lambda a, b: a + b, wts)
    o_ref[...] = functools.reduce(lambda a, b: a + b, [(wt / tot) * o for wt, (o, _, _) in zip(wts, res)])


def _dil_sample(bufs_t, layer, news_cols, qs_cols, P):
    Bd, H = qs_cols[0].shape[0], qs_cols[0].shape[1]
    in_specs, out_specs, out_shape = [], [], []
    for buf in bufs_t:
        W = buf.shape[5]
        in_specs.append(pl.BlockSpec((None, None, 2, None, HEAD_DIM, W), lambda b, h: (layer, b, 0, h, 0, 0)))
        out_specs.append(pl.BlockSpec((None, 2, None, HEAD_DIM, W), lambda b, h: (b, 0, h, 0, 0)))
        out_shape.append(jax.ShapeDtypeStruct((Bd, 2, H, HEAD_DIM, W), F32))
    in_specs += [pl.BlockSpec((None, 2, None, HEAD_DIM, LANES), lambda b, h: (b, 0, h, 0, 0))] * len(bufs_t)
    in_specs += [pl.BlockSpec((None, None, HEAD_DIM, LANES), lambda b, h: (b, h, 0, 0))] * len(bufs_t)
    out_specs.append(pl.BlockSpec((None, None, HEAD_DIM, LANES), lambda b, h: (b, h, 0, 0)))
    out_shape.append(jax.ShapeDtypeStruct((Bd, H, HEAD_DIM, LANES), F32))
    return pl.pallas_call(
        functools.partial(_dil_sample_kernel, P=P), grid=(Bd, H),
        in_specs=in_specs, out_specs=out_specs, out_shape=out_shape,
        compiler_params=_cp("parallel", "parallel"), name="dilated_sample",
    )(*bufs_t, *news_cols, *qs_cols)


def _fox_sample_kernel(pt_ref, *refs, pp, nsteps):
    pages, lfs = refs[:pp], refs[pp:2 * pp]
    q_ref, new_ref, lfnew_ref, o_ref, m_ref, l_ref, acc_ref, carry_ref = refs[2 * pp:]
    s_id = pl.program_id(1)
    L = new_ref.shape[1] // 2
    q = q_ref[0, 0] * QK_SCALE

    @pl.when(s_id == 0)
    def _():
        _tdec_init(m_ref, l_ref, acc_ref)
        new = new_ref[0]
        _tdec_update(new[0:L], new[L:2 * L], q, None, _iota((1, LANES), 1) == 0, m_ref, l_ref, acc_ref, 0)
        carry_ref[...] = lfnew_ref[0]

    later = _onehot(_iota((PAGE, PAGE), 0) > _iota((PAGE, PAGE), 1))
    for pi in range(pp):
        kv = pages[pi][...]
        lf = lfs[pi][...]
        carry = carry_ref[...]
        bias = _dotx(lf, later, 3) + carry
        carry_ref[...] = carry + jnp.sum(lf, axis=-1, keepdims=True)
        _tdec_update(kv[0:L], kv[L:2 * L], q, bias, None, m_ref, l_ref, acc_ref, 0)

    @pl.when(s_id == nsteps - 1)
    def _():
        o_ref[0, 0] = _tdec_finish(l_ref, acc_ref, 0)


def _fox_sample(kv_t, lf_t, layer, page_table, qexp, new_cols, lfnew3, pp=4):
    Bd, NP = page_table.shape
    rows = kv_t.shape[2]
    H = lf_t.shape[2]
    nsteps = NP // pp
    rev = lambda b, s, pi, pt: pt[b, NP - 1 - (s * pp + pi)]
    per_b = lambda shape: pl.BlockSpec((1,) + shape, lambda b, s, pt: (b,) + (0,) * len(shape))
    grid_spec = pltpu.PrefetchScalarGridSpec(
        num_scalar_prefetch=1, grid=(Bd, nsteps),
        in_specs=_page_specs(pp, (None, None, rows, PAGE), layer, rev) + _page_specs(pp, (None, None, H, PAGE), layer, rev)
        + [per_b((1, rows // 2, LANES)), per_b((rows, LANES)), per_b((H, LANES))],
        out_specs=per_b((1, H, HEAD_DIM, LANES)),
        scratch_shapes=_tdec_scratch(1, H) + [pltpu.VMEM((H, LANES), F32)])
    return pl.pallas_call(
        functools.partial(_fox_sample_kernel, pp=pp, nsteps=nsteps), grid_spec=grid_spec,
        out_shape=jax.ShapeDtypeStruct((Bd, 1, H, HEAD_DIM, LANES), F32),
        compiler_params=_cp("parallel", "arbitrary"), name="fox_sample",
    )(page_table, *([kv_t] * pp), *([lf_t] * pp), qexp, new_cols, lfnew3)


def _cols(x, lead):
    Bd = x.shape[0]
    return jnp.broadcast_to(x.reshape((Bd,) + lead + (1,)), (Bd,) + lead + (LANES,))


def _rows_minor(cache):
    n = cache.ndim
    return jnp.moveaxis(cache, n - 4, n - 1)


def _rows_major(x):
    n = x.ndim
    return jnp.moveaxis(x, n - 1, n - 4)


def kernel(x_prompt, x_sample, cache_a_cmp, cache_a_sel, cache_a_win, cache_b_kv, cache_b_kidx, cache_c_win0, cache_c_win1, cache_c_win2, cache_d_kv, cache_d_logf, page_table, c_prompt, c_sample, ada_w, ada_b, norm_g, final_g, a_w_in, a_w_out, a_cmp_wk, a_cmp_wv, b_w_in, b_w_out, c_w_in, c_w_out, d_w_in, d_w_out, d_f_bias):
    B, T, D = x_prompt.shape
    Bd = x_sample.shape[0]
    depth = ada_w.shape[0]
    NP = page_table.shape[1]
    P = NP * PAGE
    pool = cache_b_kidx.shape[1]
    assert x_sample.shape[1] == 1 and D == 1024 and P % A_SEL_LEN == 0
    assert cache_a_win.shape[2] == A_WINDOW
    assert all(b.shape[2] == w for b, (w, _) in zip((cache_c_win0, cache_c_win1, cache_c_win2), C_GROUPS))
    H = D // HEAD_DIM
    R = H // A_KV
    tm_p = 256
    tpb = T // tm_p

    mod = _mod_all(jnp.concatenate([c_prompt, c_sample], axis=0), ada_w, ada_b)
    pos_p = jnp.arange(T, dtype=I32)
    pos_s = jnp.full((Bd,), P, I32)
    tab_p, ttab_p = _rope_tables(pos_p), _rope_tables_t(pos_p)
    tab_s = _rope_tables(pos_s)
    xp = x_prompt.reshape(B * T, D)
    xs = x_sample.reshape(Bd, D)
    c_bufs = (cache_c_win0, cache_c_win1, cache_c_win2)
    st = {}
    put = lambda name, val: st.setdefault(name, []).append(val)
    rows_of = lambda t, c, g: _rows_major(t.reshape(t.shape[0], c, g, HEAD_DIM, t.shape[2]))
    yp = ys = None
    for i in range(depth):
        kind, li = i % 4, i // 4
        shift_p, scale_p, gate_p = [mod[i, :B, j * D:(j + 1) * D].reshape(B, 1, D) for j in range(3)]
        shift_s, scale_s, gate_s = [mod[i, B:, j * D:(j + 1) * D].reshape(1, Bd, D) for j in range(3)]
        proj_p = lambda w, segs: _proj(xp, scale_p, shift_p, norm_g[i], w, segs, tab_p, ttab_p, tpb, tm_p)
        proj_s = lambda w, segs: _proj(xs, scale_s, shift_s, norm_g[i], w, segs, tab_s, None, 1, Bd)
        if kind == 0:
            w = a_w_in[li]
            gw = A_KV * HEAD_DIM
            q, qr, gl, z, cmp_t, sel_t, win_t = proj_p(w, [
                ("rm", 0, D, ("n", "r")), ("rm", D + 6 * gw, 3 * H, ("n",)), ("rm", D + 6 * gw + 3 * H, D, ("n",)),
                ("t", D, 2 * gw, "t"), ("t", D + 2 * gw, 2 * gw, "trk"), ("t", D + 4 * gw, 2 * gw, "trk")])
            wlo, whi = _cmp_taps(a_cmp_wk[li], a_cmp_wv[li], A_KV)
            kvb = _nsa_compress_prompt(cmp_t, wlo, whi)
            o = _nsa_prompt(q.reshape(B, T, D), qr.reshape(B, T, D), kvb, sel_t, win_t, gl.reshape(B, T, LANES),
                            PASSES_SELECTIVE)
            op, zp, wo = o.reshape(B * T, D), z, a_w_out[li]
            put("a_cmp_p", rows_of(cmp_t, 2, A_KV))
            put("a_sel_p", rows_of(sel_t, 2, A_KV))
            put("a_win_p", rows_of(win_t[:, :, T - min(A_WINDOW, T):], 2, A_KV))

            q, qr, cmp_, sel, win, gl, z = proj_s(w, [
                ("rm", 0, D, ("n", "r")), ("rm", D, 2 * gw, ("n",)), ("rm", D + 2 * gw, 2 * gw, ("rk",)),
                ("rm", D + 4 * gw, 2 * gw, ("rk",)), ("rm", D + 6 * gw, 3 * H, ("n",)), ("rm", D + 6 * gw + 3 * H, D, ("n",))])
            cmp_t6 = _rows_minor(cache_a_cmp)
            lohi = _nsa_compress_sample(cmp_t6.reshape(cmp_t6.shape[0], pool, 2 * gw, PAGE), li, page_table, wlo, whi)
            tile_g = lambda t: jnp.tile(t.reshape(Bd, H, HEAD_DIM), (1, 1, A_KV))
            ocmp, idx = _nsa_sample_cmp(lohi, tile_g(q), P)
            win_t6 = _rows_minor(cache_a_win)
            nwin, owin = _nsa_sample_win(win_t6[li].reshape(Bd, 2 * gw, A_WINDOW), _cols(win, (2 * gw,)), tile_g(qr))
            nk = min(A_SEL_TOPK, -(-(P + 1) // A_SEL_LEN))
            per_head = lambda t: jnp.repeat(t.reshape(Bd, A_KV, 1, HEAD_DIM), R, axis=2).reshape(Bd, H, HEAD_DIM)
            o = _nsa_sample_sel(_rows_minor(cache_a_sel), li, page_table, idx[:, :A_KV, :nk].reshape(Bd, A_KV * nk),
                                qr.reshape(Bd, H, HEAD_DIM), per_head(sel[:, :gw]), per_head(sel[:, gw:]),
                                ocmp, owin, gl[:, :3 * H].reshape(Bd, H, 3), P)
            os_, zs = o.reshape(Bd, D), z
            put("a_cmp_s", cmp_.reshape(Bd, 1, 2, A_KV, HEAD_DIM))
            put("a_sel_s", sel.reshape(Bd, 1, 2, A_KV, HEAD_DIM))
            put("a_win_s", _rows_major(nwin.reshape(Bd, 2, A_KV, HEAD_DIM, A_WINDOW)))
        elif kind == 1:
            w = b_w_in[li]
            gw = A_KV * HEAD_DIM
            c_qi = D + 2 * gw
            c_ki = c_qi + B_IDX_HEADS * B_IDX_DIM
            c_wi = c_ki + B_IDX_DIM
            c_z = c_wi + B_IDX_HEADS
            q, qi, wi, z, kv_t, ki_t = proj_p(w, [
                ("rm", 0, D, ("r",)), ("rm", c_qi, c_ki - c_qi, ("r",)), ("rm", c_wi, B_IDX_HEADS, ("n",)), ("rm", c_z, D, ("n",)),
                ("t", D, 2 * gw, "trk"), ("t", c_ki, B_IDX_DIM, "tr")])
            o = _dsa_prompt(q.reshape(B, T, D), qi.reshape(B, T, c_ki - c_qi), wi.reshape(B, T, LANES), kv_t, ki_t,
                            min(B_TOPK_MAX, T // 4), PASSES_SMOOTH)
            op, zp, wo = o.reshape(B * T, D), z, b_w_out[li]
            put("b_kv_p", rows_of(kv_t, 2, A_KV))
            put("b_kidx_p", jnp.swapaxes(ki_t, 1, 2))

            q, kv, qi, ki, wi, z = proj_s(w, [
                ("rm", 0, D, ("r",)), ("rm", D, 2 * gw, ("rk",)), ("rm", c_qi, c_ki - c_qi, ("r",)), ("rm", c_ki, B_IDX_DIM, ("r",)),
                ("rm", c_wi, B_IDX_HEADS, ("n",)), ("rm", c_z, D, ("n",))])
            qi3 = qi.reshape(Bd, B_IDX_HEADS, B_IDX_DIM)
            wi3 = wi[:, :B_IDX_HEADS].reshape(Bd, B_IDX_HEADS, 1)
            ki_new = ki[:, :B_IDX_DIM]
            sc = _dsa_sample_scores(jnp.swapaxes(cache_b_kidx, 2, 3), li, page_table, qi3, wi3)
            mask = _dsa_sample_select(sc.reshape(Bd, P), qi3, wi3, ki_new.reshape(Bd, 1, B_IDX_DIM),
                                      min(B_TOPK_MAX, (P + 1) // 4))
            kv_t6 = _rows_minor(cache_b_kv)
            qexp = _cols(q.reshape(Bd, A_KV, R, HEAD_DIM).transpose(0, 2, 1, 3).reshape(Bd, R, gw), (R, gw))
            o = _dsa_sample_attn(kv_t6.reshape(kv_t6.shape[0], pool, 2 * gw, PAGE), li, page_table,
                                 mask[:, :P].reshape(Bd, NP, 1, PAGE), qexp, _cols(kv, (2 * gw,)),
                                 jnp.broadcast_to(mask[:, P:P + 1], (Bd, LANES)).reshape(Bd, 1, LANES))
            os_, zs = o[..., 0].transpose(0, 2, 1, 3).reshape(Bd, D), z
            put("b_kv_s", kv.reshape(Bd, 1, 2, A_KV, HEAD_DIM))
            put("b_kidx_s", ki_new.reshape(Bd, 1, B_IDX_DIM))
        elif kind == 2:
            w = c_w_in[li]
            Wd = C_HEADS * HEAD_DIM
            ng = len(C_GROUPS)
            res = proj_p(w, [("rm", 3 * g * Wd, Wd, ("r",)) for g in range(ng)] + [("rm", 3 * ng * Wd, Wd, ("n",))]
                         + [("t", (3 * g + 1) * Wd, 2 * Wd, "trk") for g in range(ng)])
            qs, z, kvs_t = res[0:ng], res[ng], res[ng + 1:]
            o = _dil_prompt([t.reshape(B, T, Wd) for t in qs], kvs_t, PASSES_SMOOTH)
            op, zp, wo = o.reshape(B * T, Wd), z, c_w_out[li]
            for g, (wg, _) in enumerate(C_GROUPS):
                put("c_win%d_p" % g, rows_of(kvs_t[g][:, :, T - min(wg, T):], 2, C_HEADS))

            res = proj_s(w, [seg for g in range(ng) for seg in (("rm", 3 * g * Wd, Wd, ("r",)), ("rm", (3 * g + 1) * Wd, 2 * Wd, ("rk",)))]
                         + [("rm", 3 * ng * Wd, Wd, ("n",))])
            qs, kvs, z = res[0:2 * ng:2], res[1:2 * ng:2], res[2 * ng]
            outs = _dil_sample([_rows_minor(b) for b in c_bufs], li, [_cols(t, (2, C_HEADS, HEAD_DIM)) for t in kvs],
                               [_cols(t, (C_HEADS, HEAD_DIM)) for t in qs], P)
            os_, zs = outs[ng][..., 0].reshape(Bd, Wd), z
            for g in range(ng):
                put("c_win%d_s" % g, _rows_major(outs[g]))
        else:
            w = d_w_in[li]
            q, z, kv_t, f_t = proj_p(w, [("rm", 0, D, ("n",)), ("rm", 3 * D + H, D, ("n",)), ("t", D, 2 * D, "t"), ("t", 3 * D, H, "t")])
            fb = d_f_bias[li].reshape(H, 1)
            lf_t, c_t = _fox_logf(f_t, fb)
            o = _fox_prompt(q.reshape(B, T, D), kv_t, jnp.swapaxes(c_t, 1, 2), c_t, PASSES_SMOOTH)
            op, zp, wo = o.reshape(B * T, D), z, d_w_out[li]
            put("d_kv_p", rows_of(kv_t, 2, H))
            put("d_logf_p", jnp.swapaxes(lf_t, 1, 2))

            q, kv, f, z = proj_s(w, [("rm", 0, D, ("n",)), ("rm", D, 2 * D, ("n",)), ("rm", 3 * D, H, ("n",)), ("rm", 3 * D + H, D, ("n",))])
            lf = _fox_logf(f[:, :H].T.reshape(1, H, Bd), fb)[0][0].T
            kv_t6 = _rows_minor(cache_d_kv)
            o = _fox_sample(kv_t6.reshape(kv_t6.shape[0], pool, 2 * D, PAGE), jnp.swapaxes(cache_d_logf, 2, 3), li, page_table,
                            _cols(q, (1, D)), _cols(kv, (2 * D,)), _cols(lf, (H,)))
            os_, zs = o[..., 0].reshape(Bd, D), z
            put("d_kv_s", kv.reshape(Bd, 1, 2, H, HEAD_DIM))
            put("d_logf_s", lf.reshape(Bd, 1, H))
        fg = final_g if i == depth - 1 else None
        xp, yp = _gated_out(op, zp, wo, xp, gate_p, tpb, tm_p, fg)
        xs, ys = _gated_out(os_, zs, wo, xs, gate_s, 1, Bd, fg)
    ns = {n: jnp.stack(v) for n, v in st.items()}
    names = ("a_cmp_p", "a_cmp_s", "a_sel_p", "a_sel_s", "a_win_p", "a_win_s", "b_kv_p", "b_kv_s", "b_kidx_p", "b_kidx_s",
             "c_win0_p", "c_win0_s", "c_win1_p", "c_win1_s", "c_win2_p", "c_win2_s", "d_kv_p", "d_kv_s", "d_logf_p", "d_logf_s")
    return (yp.reshape(B, T, D), ys.reshape(Bd, 1, D)) + tuple(ns[n] for n in names)
```

```python
import functools

import jax
import jax.numpy as jnp
from jax import lax
from jax.experimental import pallas as pl
from jax.experimental.pallas import tpu as pltpu

F32 = jnp.float32
BF16 = jnp.bfloat16
I32 = jnp.int32

HEAD_DIM = 64
ROPE_DIMS = HEAD_DIM // 4
ROPE_HALF = ROPE_DIMS // 2
ROPE_THETA = 500000.0
NORM_EPS = 1e-6
NEG_INF = -1e30
TINY = 1e-30
PAGE = 128
LANES = 128
INT_MIN = -2 ** 31

A_KV = 4
A_CMP_STRIDE = 16
A_CMP_LEN = 32
A_SEL_LEN = 64
A_SEL_TOPK = 16
A_WINDOW = 512
A_FORCED = 1e4
B_IDX_HEADS = 8
B_IDX_DIM = 64
B_TOPK_MAX = 256
C_GROUPS = ((128, 1), (512, 4), (2048, 16))
C_HEADS = 8
QK_SCALE = HEAD_DIM ** -0.5

NN = ((1,), (0,))
NT = ((1,), (1,))

VMEM_LIMIT = 56 * 1024 * 1024
PROJ_COLS_PER_CALL = 2048

PASSES_SELECTIVE = 3
PASSES_SMOOTH = 1
PASSES_NSA_FLASH = 1
PASSES_PROMPT_PROJ = 1
PASSES_PROMPT_INDEXER = 1


def _cp(*sem):
    return pltpu.CompilerParams(dimension_semantics=sem, vmem_limit_bytes=VMEM_LIMIT)


def _dotf(a, b, dims=NN):
    return lax.dot_general(a, b, (dims, ((), ())), preferred_element_type=F32)


def _split(a, terms):
    out = []
    for _ in range(terms - 1):
        h = a.astype(BF16)
        out.append(h)
        a = a - h.astype(F32)
    out.append(a.astype(BF16))
    return out


def _dot3(a, b, dims=NN):
    ah, al = _split(a, 2)
    bh, bl = _split(b, 2)
    return _dotf(ah, bh, dims) + (_dotf(ah, bl, dims) + _dotf(al, bh, dims))


def _dotp(a, b, dims, passes):
    if passes == 1:
        return _dotf(a.astype(BF16), b.astype(BF16), dims)
    return _dot3(a, b, dims)


def _dotx(a, m01, terms, dims=NN):
    acc = None
    for t in _split(a, terms):
        d = _dotf(t, m01, dims)
        acc = d if acc is None else acc + d
    return acc


def _iota(shape, axis):
    return lax.broadcasted_iota(I32, shape, axis)


def _onehot(cond):
    return jnp.where(cond, 1.0, 0.0).astype(BF16)


def _sigmoid(x):
    return 1.0 / (1.0 + jnp.exp(-x))


def _sortable_key(x):
    x = jnp.where(x == 0.0, 0.0, x)
    b = lax.bitcast_convert_type(x, I32)
    return jnp.where(b < 0, b ^ 0x7FFFFFFF, b)


def _count_ge(key, cand):
    return jnp.sum(jnp.where(key >= cand, 1.0, 0.0), axis=-1, keepdims=True)


def _kth_largest_key(key, k):
    base = jnp.where(_count_ge(key, 0) >= k, 0, INT_MIN).astype(I32)

    def body(it, base):
        cand = base | jnp.left_shift(jnp.int32(1), 30 - it)
        return jnp.where(_count_ge(key, cand) >= k, cand, base)

    return lax.fori_loop(0, 31, body, base)


def _topk_mask(key, k):
    n = key.shape[1]
    thr = _kth_largest_key(key, k)
    gt = key > thr
    eq = key == thr
    need = k - jnp.sum(jnp.where(gt, 1.0, 0.0), axis=-1, keepdims=True)
    before = _onehot(_iota((LANES, LANES), 0) < _iota((LANES, LANES), 1))
    run = jnp.zeros_like(need)
    out = []
    for c in range(n // LANES):
        sl = slice(c * LANES, (c + 1) * LANES)
        eqf = jnp.where(eq[:, sl], 1.0, 0.0)
        prior = _dotf(eqf.astype(BF16), before) + run
        out.append(gt[:, sl] | (eq[:, sl] & (prior < need)))
        run = run + jnp.sum(eqf, axis=-1, keepdims=True)
    return jnp.concatenate(out, axis=1)


def _flash(q, kt_ref, vt_ref, krow, vrow, c_lo, c_hi, tk, mask_fn, passes, bias_fn=None, c_free=None):
    M = q.shape[0]

    def step(c, carry, masked):
        m, l, acc = carry
        off = pl.multiple_of(c * tk, tk)
        kt = kt_ref[0, krow:krow + HEAD_DIM, pl.ds(off, tk)]
        vt = vt_ref[0, vrow:vrow + HEAD_DIM, pl.ds(off, tk)]
        s = _dotp(q, kt, NN, passes)
        if bias_fn is not None:
            s = s + bias_fn(c)
        if masked:
            s = mask_fn(c, s)
        m_new = jnp.maximum(m, jnp.max(s, axis=-1, keepdims=True))
        alpha = jnp.exp(m - m_new)
        e = jnp.exp(s - m_new)
        l = alpha * l + jnp.sum(e, axis=-1, keepdims=True)
        acc = alpha * acc + _dotp(e, vt, NT, passes)
        return m_new, l, acc

    carry = (jnp.full((M, 1), 0.1 * NEG_INF, F32), jnp.zeros((M, 1), F32), jnp.zeros((M, HEAD_DIM), F32))
    if c_free is not None:
        carry = lax.fori_loop(c_lo, c_free, functools.partial(step, masked=False), carry)
        c_lo = c_free
    m, l, acc = lax.fori_loop(c_lo, c_hi, functools.partial(step, masked=True), carry)
    return acc / jnp.maximum(l, TINY), m, l


def _stack_heads(ref, first_head, n):
    return jnp.concatenate(
        [ref[0, :, (first_head + r) * HEAD_DIM:(first_head + r + 1) * HEAD_DIM] for r in range(n)], axis=0)


def _group_block_diag(qt, n_heads, n_groups):
    per = n_heads // n_groups
    shift = per.bit_length() - 1
    keep = (_iota(qt.shape, 0) >> shift) == (_iota(qt.shape, 1) >> 6)
    return jnp.where(keep, qt, 0.0), keep


def _fold_groups(x, keep, n_groups):
    x = jnp.where(keep, x, 0.0)
    return functools.reduce(lambda a, b: a + b, [x[:, g * HEAD_DIM:(g + 1) * HEAD_DIM] for g in range(n_groups)])


def _mod_kernel(c_ref, w_ref, b_ref, o_ref):
    c = c_ref[...]
    o_ref[0] = _dot3(c * _sigmoid(c), w_ref[0]) + b_ref[0]


def _mod_all(c_all, ada_w, ada_b):
    L, D, D3 = ada_w.shape
    NC = c_all.shape[0]
    tn = 1024
    return pl.pallas_call(
        _mod_kernel, grid=(L, D3 // tn),
        in_specs=[pl.BlockSpec((NC, D), lambda l, j: (0, 0)),
                  pl.BlockSpec((1, D, tn), lambda l, j: (l, 0, j)),
                  pl.BlockSpec((1, 1, tn), lambda l, j: (l, 0, j))],
        out_specs=pl.BlockSpec((1, NC, tn), lambda l, j: (l, 0, j)),
        out_shape=jax.ShapeDtypeStruct((L, NC, D3), F32),
        compiler_params=_cp("parallel", "parallel"), name="adaln_mod",
    )(c_all, ada_w, ada_b.reshape(L, 1, D3))


def _proj_kernel(*refs, rsegs, tsegs, passes):
    x_ref, sc_ref, sh_ref, g_ref = refs[0:4]
    pos = 4
    if rsegs:
        wh_ref, wl_ref, cos_ref, sn_ref, sp_ref = refs[pos:pos + 5]
        pos += 5
    if tsegs:
        wth_ref, wtl_ref, cost_ref, sint_ref = refs[pos:pos + 4]
        pos += 4
    out_refs = refs[pos:]
    x = x_ref[...]
    y = x * lax.rsqrt(jnp.mean(x * x, axis=-1, keepdims=True) + NORM_EPS) * g_ref[...]
    h = y * (1.0 + sc_ref[0]) + sh_ref[0]
    hh, hl = _split(h, 2)
    oi = 0
    for start, width, modes in rsegs:
        wh = wh_ref[:, start:start + width]
        u = _dotf(hh, wh)
        if passes == 3:
            u = u + (_dotf(hh, wl_ref[:, start:start + width]) + _dotf(hl, wh))
        for mode in modes:
            o_ref = out_refs[oi]
            oi += 1
            n_rope = {"n": 0, "r": width, "rk": width // 2}[mode]
            for c in range(width // LANES):
                sl = slice(c * LANES, (c + 1) * LANES)
                uc = u[:, sl]
                if c * LANES < n_rope:
                    uc = (uc * cos_ref[...] + pltpu.roll(uc, LANES - ROPE_HALF, 1) * sn_ref[...]
                          + pltpu.roll(uc, ROPE_HALF, 1) * sp_ref[...])
                o_ref[:, sl] = uc
    for start, width, mode in tsegs:
        wth = wth_ref[start:start + width, :]
        ut = _dotf(wth, hh, NT)
        if passes == 3:
            ut = ut + (_dotf(wth, hl, NT) + _dotf(wtl_ref[start:start + width, :], hh, NT))
        o_ref = out_refs[oi]
        oi += 1
        o_ref[0] = ut
        n_rope = {"t": 0, "tr": width, "trk": width // 2}[mode]
        for hd in range(n_rope // HEAD_DIM):
            r0 = hd * HEAD_DIM
            x1 = ut[r0:r0 + ROPE_HALF]
            x2 = ut[r0 + ROPE_HALF:r0 + ROPE_DIMS]
            o_ref[0, r0:r0 + ROPE_HALF, :] = x1 * cost_ref[...] - x2 * sint_ref[...]
            o_ref[0, r0 + ROPE_HALF:r0 + ROPE_DIMS, :] = x2 * cost_ref[...] + x1 * sint_ref[...]


def _proj(x2, scale, shift, g, w, segs, tables, ttables, tiles_per_batch, tm, passes):
    R, D = x2.shape
    NB, RB = scale.shape[0], scale.shape[1]
    Tt = R // NB
    groups, cur, cur_w = [], [], 0
    for seg in segs:
        wd = -(-seg[2] // LANES) * LANES
        if cur and cur_w + wd > PROJ_COLS_PER_CALL:
            groups.append(cur)
            cur, cur_w = [], 0
        cur.append(seg)
        cur_w += wd
    groups.append(cur)
    results = {}
    for grp in groups:
        rsegs, tsegs, rcols, tcols = [], [], [], []
        out_shapes, out_specs, keys = [], [], []
        rpos = tpos = 0
        for si, (kind, c0, wd, spec) in enumerate(grp):
            if kind == "rm":
                pw = -(-wd // LANES) * LANES
                rcols.append(w[:, c0:c0 + wd])
                if pw > wd:
                    rcols.append(jnp.zeros((D, pw - wd), F32))
                rsegs.append((rpos, pw, spec))
                rpos += pw
                for mi in range(len(spec)):
                    out_shapes.append(jax.ShapeDtypeStruct((R, pw), F32))
                    out_specs.append(pl.BlockSpec((tm, pw), lambda i: (i, 0)))
                    keys.append((id(grp), si, mi))
        for si, (kind, c0, wd, spec) in enumerate(grp):
            if kind == "t":
                tcols.append(w[:, c0:c0 + wd])
                tsegs.append((tpos, wd, spec))
                tpos += wd
                out_shapes.append(jax.ShapeDtypeStruct((NB, wd, Tt), F32))
                out_specs.append(pl.BlockSpec((1, wd, tm), lambda i: (i // tiles_per_batch, 0, i % tiles_per_batch)))
                keys.append((id(grp), si, 0))
        mod_spec = pl.BlockSpec((1, RB, D), lambda i: (i // tiles_per_batch, 0, 0))
        in_specs = [pl.BlockSpec((tm, D), lambda i: (i, 0)), mod_spec, mod_spec, pl.BlockSpec((1, D), lambda i: (0, 0))]
        args = [x2, scale, shift, g.reshape(1, D)]
        if rsegs:
            wr = jnp.concatenate(rcols, axis=1)
            wh = wr.astype(BF16)
            wl = (wr - wh.astype(F32)).astype(BF16)
            tab_spec = pl.BlockSpec((tm, LANES), lambda i: (i % tiles_per_batch, 0))
            in_specs += [pl.BlockSpec((D, rpos), lambda i: (0, 0))] * 2 + [tab_spec] * 3
            args += [wh, wl, *tables]
        if tsegs:
            wt = jnp.concatenate(tcols, axis=1).T
            wth = wt.astype(BF16)
            wtl = (wt - wth.astype(F32)).astype(BF16)
            ttab_spec = pl.BlockSpec((ROPE_HALF, tm), lambda i: (0, i % tiles_per_batch))
            in_specs += [pl.BlockSpec((tpos, D), lambda i: (0, 0))] * 2 + [ttab_spec] * 2
            args += [wth, wtl, *ttables]
        res = pl.pallas_call(
            functools.partial(_proj_kernel, rsegs=tuple(rsegs), tsegs=tuple(tsegs), passes=passes), grid=(R // tm,),
            in_specs=in_specs, out_specs=out_specs, out_shape=out_shapes,
            compiler_params=_cp("parallel"), name="norm_mod_proj",
        )(*args)
        for key, r in zip(keys, res):
            results[key] = r
    outs = []
    for grp in groups:
        for si, (kind, c0, wd, spec) in enumerate(grp):
            for mi in range(len(spec) if kind == "rm" else 1):
                outs.append(results[(id(grp), si, mi)])
    return outs


def _rope_angles(pos):
    inv = ROPE_THETA ** (-(jnp.arange(ROPE_HALF, dtype=F32) / ROPE_HALF))
    ang = pos.astype(F32)[:, None] * inv[None, :]
    return jnp.cos(ang), jnp.sin(ang)


def _rope_tables(pos):
    cos, sin = _rope_angles(pos)
    R = pos.shape[0]
    one = jnp.ones((R, HEAD_DIM - ROPE_DIMS), F32)
    zero = jnp.zeros((R, HEAD_DIM - ROPE_DIMS), F32)
    zh = jnp.zeros((R, ROPE_HALF), F32)
    cos_h = jnp.concatenate([cos, cos, one], axis=1)
    sn_h = jnp.concatenate([-sin, zh, zero], axis=1)
    sp_h = jnp.concatenate([zh, sin, zero], axis=1)
    rep = LANES // HEAD_DIM
    return tuple(jnp.tile(t, (1, rep)) for t in (cos_h, sn_h, sp_h))


def _rope_tables_t(pos):
    cos, sin = _rope_angles(pos)
    return cos.T, sin.T


def _out_kernel(o_ref, z_ref, wh_ref, wl_ref, x_ref, gate_ref, *rest, final, passes):
    z = z_ref[...]
    y = o_ref[...] * (z * _sigmoid(z))
    yh, yl = _split(y, 2)
    wh = wh_ref[...]
    r = _dotf(yh, wh)
    if passes == 3:
        r = r + (_dotf(yh, wl_ref[...]) + _dotf(yl, wh))
    xn = x_ref[...] + gate_ref[0] * r
    if final:
        fg_ref, xo_ref, yo_ref = rest
        yo_ref[...] = xn * lax.rsqrt(jnp.mean(xn * xn, axis=-1, keepdims=True) + NORM_EPS) * fg_ref[...]
    else:
        (xo_ref,) = rest
    xo_ref[...] = xn


def _gated_out(o2, z2, w_out, x2, gate, tiles_per_batch, tm, passes, final_g=None):
    R, W = o2.shape
    D = x2.shape[1]
    RB = gate.shape[1]
    wh = w_out.astype(BF16)
    wl = (w_out - wh.astype(F32)).astype(BF16)
    final = final_g is not None
    in_specs = [pl.BlockSpec((tm, W), lambda i: (i, 0)), pl.BlockSpec((tm, W), lambda i: (i, 0)),
                pl.BlockSpec((W, D), lambda i: (0, 0)), pl.BlockSpec((W, D), lambda i: (0, 0)),
                pl.BlockSpec((tm, D), lambda i: (i, 0)),
                pl.BlockSpec((1, RB, D), lambda i: (i // tiles_per_batch, 0, 0))]
    args = [o2, z2, wh, wl, x2, gate]
    out_shape = [jax.ShapeDtypeStruct((R, D), F32)]
    out_specs = [pl.BlockSpec((tm, D), lambda i: (i, 0))]
    if final:
        in_specs.append(pl.BlockSpec((1, D), lambda i: (0, 0)))
        args.append(final_g.reshape(1, D))
        out_shape.append(jax.ShapeDtypeStruct((R, D), F32))
        out_specs.append(pl.BlockSpec((tm, D), lambda i: (i, 0)))
    res = pl.pallas_call(
        functools.partial(_out_kernel, final=final, passes=passes), grid=(R // tm,),
        in_specs=in_specs, out_specs=out_specs, out_shape=out_shape,
        compiler_params=_cp("parallel"), name="gated_out_proj",
    )(*args)
    return res if final else (res[0], None)


def _cmp_taps(wk, wv, groups):
    def table(w):
        t = jnp.tile(w.T, (1, LANES // A_CMP_STRIDE))
        return jnp.tile(t, (groups, 1))
    S = A_CMP_STRIDE
    lo = jnp.concatenate([table(wk[:S]), table(wv[:S])], axis=0)
    hi = jnp.concatenate([table(wk[S:]), table(wv[S:])], axis=0)
    return lo, hi


def _chunk_sum_matrix(first_col):
    shift = A_CMP_STRIDE.bit_length() - 1
    return _onehot((_iota((LANES, LANES), 0) >> shift) + first_col == _iota((LANES, LANES), 1))


def _cmp_kernel(x_ref, wlo_ref, whi_ref, o_ref, *, T):
    rows = x_ref.shape[1]
    per = LANES // A_CMP_STRIDE
    lo = jnp.zeros((rows, LANES), F32)
    hi = jnp.zeros((rows, LANES), F32)
    for c in range(T // LANES):
        xc = x_ref[0, :, c * LANES:(c + 1) * LANES]
        m = _chunk_sum_matrix(c * per)
        lo = lo + _dotx(xc * wlo_ref[...], m, 3)
        hi = hi + _dotx(xc * whi_ref[...], m, 3)
    o_ref[0] = lo + pltpu.roll(hi, LANES - 1, 1)


def _nsa_compress_prompt(cmp_t, wlo, whi):
    B, rows, T = cmp_t.shape
    assert T // A_CMP_STRIDE <= LANES
    return pl.pallas_call(
        functools.partial(_cmp_kernel, T=T), grid=(B,),
        in_specs=[pl.BlockSpec((1, rows, T), lambda b: (b, 0, 0)), pl.BlockSpec((rows, LANES), lambda b: (0, 0)),
                  pl.BlockSpec((rows, LANES), lambda b: (0, 0))],
        out_specs=pl.BlockSpec((1, rows, LANES), lambda b: (b, 0, 0)),
        out_shape=jax.ShapeDtypeStruct((B, rows, LANES), F32),
        compiler_params=_cp("parallel"), name="nsa_compress",
    )(cmp_t, wlo, whi)


def _nsa_prompt_kernel(q_ref, qr_ref, kvb_ref, sel_ref, win_ref, gl_ref, o_ref, *, tq, tk, T, flash_passes):
    i = pl.program_id(1)
    t0 = i * tq
    R = 4
    nb = kvb_ref.shape[2]
    ncb = T // A_CMP_STRIDE - 1
    nsb = T // A_SEL_LEN
    gw = A_KV * HEAD_DIM
    gates = _sigmoid(gl_ref[0])

    n_io = _iota((R * tq, nb), 1)
    t_c = (_iota((R * tq, nb), 0) & (tq - 1)) + t0
    cmask = (n_io * A_CMP_STRIDE + (A_CMP_LEN - 1) <= t_c) & (n_io < ncb)

    ratio = A_SEL_LEN // A_CMP_STRIDE
    back = A_CMP_LEN // A_CMP_STRIDE - 1
    mn, mj = _iota((nb, LANES), 0), _iota((nb, LANES), 1)
    imp_mat = _onehot((mn >= mj * ratio - back) & (mn <= mj * ratio + ratio - 1) & (mn < ncb) & (mj < nsb))

    j_io = _iota((tq, LANES), 1)
    qb = (_iota((tq, LANES), 0) + t0) >> 6
    allowed = (j_io <= qb) & (j_io < nsb)
    forced = (j_io == 0) | (j_io == qb)
    nsr = -(-nsb // 8) * 8
    jt_io = _iota((nsr, tq), 0)
    allowed_t = (jt_io <= ((_iota((nsr, tq), 1) + t0) >> 6)) & (jt_io < nsb)

    dist = (_iota((R * tq, tk), 0) & (tq - 1)) + t0 - _iota((R * tq, tk), 1)
    c_hi = (t0 + tq + tk - 1) // tk
    w_lo = jnp.maximum(t0 - (A_WINDOW - 1), 0) // tk

    for g in range(A_KV):
        q4 = _stack_heads(q_ref, R * g, R) * QK_SCALE
        kb = kvb_ref[0, g * HEAD_DIM:(g + 1) * HEAD_DIM, :]
        vb = kvb_ref[0, gw + g * HEAD_DIM:gw + (g + 1) * HEAD_DIM, :]
        s = jnp.where(cmask, _dot3(q4, kb, NN), NEG_INF)
        m = jnp.max(s, axis=-1, keepdims=True)
        e = jnp.where(cmask, jnp.exp(s - m), 0.0)
        p = e / jnp.maximum(jnp.sum(e, axis=-1, keepdims=True), TINY)
        o_cmp = _dot3(p, vb, NT)
        pg = p[0:tq] + p[tq:2 * tq] + p[2 * tq:3 * tq] + p[3 * tq:4 * tq]
        imp = _dotx(pg, imp_mat, 3)
        score = jnp.where(allowed, jnp.where(forced, A_FORCED, imp), -1.0)
        score = jnp.where(j_io < nsb, score, -2.0)
        score_t = score.T[0:nsr]
        rank = jnp.zeros((nsr, tq), F32)
        for j2 in range(nsb):
            other = score_t[j2:j2 + 1, :]
            rank = rank + jnp.where((other > score_t) | ((other == score_t) & (j2 < jt_io)), 1.0, 0.0)
        sel_t = jnp.where((rank < min(A_SEL_TOPK, nsb)) & allowed_t, 1.0, 0.0)
        sel = jnp.concatenate([sel_t, jnp.zeros((LANES - nsr, tq), F32)], axis=0).T.astype(BF16)
        sel4 = jnp.concatenate([sel] * R, axis=0)

        def sel_mask(c, s):
            blk = (_iota((LANES, tk), 1) + c * tk) >> 6
            hit = _dotf(sel4, _onehot(blk == _iota((LANES, tk), 0)))
            return jnp.where((hit > 0.5) & (dist >= c * tk), s, NEG_INF)

        def win_mask(c, s):
            return jnp.where(((dist - c * tk) & ~(A_WINDOW - 1)) == 0, s, NEG_INF)

        q4r = _stack_heads(qr_ref, R * g, R) * QK_SCALE
        o_sel, _, _ = _flash(q4r, sel_ref, sel_ref, g * HEAD_DIM, gw + g * HEAD_DIM, 0, c_hi, tk, sel_mask, flash_passes)
        o_win, _, _ = _flash(q4r, win_ref, win_ref, g * HEAD_DIM, gw + g * HEAD_DIM, w_lo, c_hi, tk, win_mask, flash_passes)
        outs = []
        for r in range(R):
            h = R * g + r
            rows = slice(r * tq, (r + 1) * tq)
            outs.append(gates[:, 3 * h:3 * h + 1] * o_cmp[rows] + gates[:, 3 * h + 1:3 * h + 2] * o_sel[rows]
                        + gates[:, 3 * h + 2:3 * h + 3] * o_win[rows])
        o_ref[0, :, g * R * HEAD_DIM:(g + 1) * R * HEAD_DIM] = jnp.concatenate(outs, axis=1)


def _nsa_prompt(q3, qr3, kvb, sel_t, win_t, gl3, flash_passes, tq=128, tk=512):
    B, T, D = q3.shape
    full = lambda a: pl.BlockSpec((1,) + a.shape[1:], lambda b, i: (b, 0, 0))
    return pl.pallas_call(
        functools.partial(_nsa_prompt_kernel, tq=tq, tk=tk, T=T, flash_passes=flash_passes), grid=(B, T // tq),
        in_specs=[pl.BlockSpec((1, tq, D), lambda b, i: (b, i, 0)), pl.BlockSpec((1, tq, D), lambda b, i: (b, i, 0)),
                  full(kvb), full(sel_t), full(win_t), pl.BlockSpec((1, tq, LANES), lambda b, i: (b, i, 0))],
        out_specs=pl.BlockSpec((1, tq, D), lambda b, i: (b, i, 0)),
        out_shape=jax.ShapeDtypeStruct((B, T, D), F32),
        compiler_params=_cp("parallel", "arbitrary"), name="nsa_prompt",
    )(q3, qr3, kvb, sel_t, win_t, gl3)


def _dsa_prompt_kernel(q_ref, qi_ref, wi_ref, kv_ref, ki_ref, o_ref, ch_ref, *, tq, tk, T, topk, passes, idx_passes):
    i = pl.program_id(1)
    t0 = i * tq
    R = 4
    gw = kv_ref.shape[1] // 2
    c_hi = (t0 + tq + tk - 1) // tk
    assert tk >= topk

    def select_keys(width):
        wi = wi_ref[0] * (B_IDX_HEADS * B_IDX_DIM) ** -0.5
        ki = ki_ref[0, :, 0:width]
        score = jnp.zeros((tq, width), F32)
        for h in range(B_IDX_HEADS):
            logits = _dotp(qi_ref[0, :, h * B_IDX_DIM:(h + 1) * B_IDX_DIM], ki, NN, idx_passes)
            score = score + wi[:, h:h + 1] * jnp.maximum(logits, 0.0)
        causal = _iota((tq, width), 1) <= _iota((tq, width), 0) + t0
        score = jnp.where(causal, score, NEG_INF)
        ch_ref[:, 0:width] = jnp.where(_topk_mask(_sortable_key(score), topk) & causal, 0.0, NEG_INF)

    for nc in range(1, T // tk + 1):
        pl.when(c_hi == nc)(functools.partial(select_keys, nc * tk))

    def mask_fn(c, s):
        picked = ch_ref[:, pl.ds(pl.multiple_of(c * tk, tk), tk)]
        return s + jnp.concatenate([picked] * R, axis=0)

    for g in range(gw // HEAD_DIM):
        q4 = _stack_heads(q_ref, R * g, R) * QK_SCALE
        o, _, _ = _flash(q4, kv_ref, kv_ref, g * HEAD_DIM, gw + g * HEAD_DIM, 0, c_hi, tk, mask_fn, passes)
        o_ref[0, :, g * R * HEAD_DIM:(g + 1) * R * HEAD_DIM] = jnp.concatenate(
            [o[r * tq:(r + 1) * tq] for r in range(R)], axis=1)


def _dsa_prompt(q3, qi3, wi3, kv_t, ki_t, topk, passes, idx_passes, tq=128, tk=512):
    B, T, D = q3.shape
    full = lambda a: pl.BlockSpec((1,) + a.shape[1:], lambda b, i: (b, 0, 0))
    blk = lambda w: pl.BlockSpec((1, tq, w), lambda b, i: (b, i, 0))
    return pl.pallas_call(
        functools.partial(_dsa_prompt_kernel, tq=tq, tk=tk, T=T, topk=topk, passes=passes, idx_passes=idx_passes),
        grid=(B, T // tq),
        in_specs=[blk(D), blk(qi3.shape[2]), blk(LANES), full(kv_t), full(ki_t)],
        out_specs=blk(D), out_shape=jax.ShapeDtypeStruct((B, T, D), F32),
        scratch_shapes=[pltpu.VMEM((tq, T), F32)],
        compiler_params=_cp("parallel", "arbitrary"), name="dsa_prompt",
    )(q3, qi3, wi3, kv_t, ki_t)


def _dil_prompt_kernel(*refs, tq, tk, T, passes):
    ng = len(C_GROUPS)
    q_refs, k_refs, v_refs, o_ref = refs[0:ng], refs[ng:2 * ng], refs[2 * ng:3 * ng], refs[3 * ng]
    i = pl.program_id(2)
    t0 = i * tq
    dist = _iota((tq, tk), 0) + t0 - _iota((tq, tk), 1)
    c_hi = (t0 + tq + tk - 1) // tk
    outs = []
    for hh in range(LANES // HEAD_DIM):
        lane = hh * HEAD_DIM
        res = []
        for (w, r), q_ref, k_ref, v_ref in zip(C_GROUPS, q_refs, k_refs, v_refs):
            def mask_fn(c, s, w=w, r=r):
                d = dist - c * tk
                ok = (d & (INT_MIN | (r - 1))) == 0
                if w < T - 1:
                    ok = ok & (d <= w)
                return jnp.where(ok, s, NEG_INF)

            c_lo = jnp.maximum(t0 - w, 0) // tk
            q = q_ref[0, :, lane:lane + HEAD_DIM] * QK_SCALE
            res.append(_flash(q, k_ref, v_ref, lane, lane, c_lo, c_hi, tk, mask_fn, passes))
        m_all = functools.reduce(jnp.maximum, [m for _, m, _ in res])
        wts = [den * jnp.exp(m - m_all) for _, m, den in res]
        tot = functools.reduce(lambda a, b: a + b, wts)
        outs.append(functools.reduce(lambda a, b: a + b, [(wt / tot) * o for wt, (o, _, _) in zip(wts, res)]))
    o_ref[0] = jnp.concatenate(outs, axis=1)


def _dil_prompt(qs, kvs_t, passes, tq=256, tk=512):
    B, T, W = qs[0].shape
    hp = W // LANES
    qspec = pl.BlockSpec((1, tq, LANES), lambda b, h, i: (b, i, h))
    kspec = pl.BlockSpec((1, LANES, T), lambda b, h, i: (b, h, 0))
    vspec = pl.BlockSpec((1, LANES, T), lambda b, h, i: (b, hp + h, 0))
    ng = len(C_GROUPS)
    return pl.pallas_call(
        functools.partial(_dil_prompt_kernel, tq=tq, tk=tk, T=T, passes=passes), grid=(B, hp, T // tq),
        in_specs=[qspec] * ng + [kspec] * ng + [vspec] * ng,
        out_specs=qspec, out_shape=jax.ShapeDtypeStruct((B, T, W), F32),
        compiler_params=_cp("parallel", "parallel", "arbitrary"), name="dilated_prompt",
    )(*qs, *kvs_t, *kvs_t)


def _logf_kernel(f_ref, b_ref, lf_ref, c_ref, *, T, tc):
    x = f_ref[0] + b_ref[...]
    lf = jnp.minimum(x, 0.0) - jnp.log(1.0 + jnp.exp(-jnp.abs(x)))
    lf_ref[0] = lf
    parts = _split(lf, 3)
    for c in range(T // tc):
        upto = _onehot(_iota((T, tc), 0) <= _iota((T, tc), 1) + c * tc)
        c_ref[0, :, c * tc:(c + 1) * tc] = functools.reduce(lambda a, b: a + b, [_dotf(p, upto) for p in parts])


def _fox_logf(f_t, bias_col):
    B, H, T = f_t.shape
    spec = pl.BlockSpec((1, H, T), lambda b: (b, 0, 0))
    return pl.pallas_call(
        functools.partial(_logf_kernel, T=T, tc=min(256, T)), grid=(B,),
        in_specs=[spec, pl.BlockSpec((H, 1), lambda b: (0, 0))],
        out_specs=[spec, spec], out_shape=[jax.ShapeDtypeStruct((B, H, T), F32)] * 2,
        compiler_params=_cp("parallel"), name="fox_logf_cumsum",
    )(f_t, bias_col)


def _fox_prompt_kernel(q_ref, k_ref, v_ref, cc_ref, cr_ref, o_ref, *, tq, tk, T, passes):
    hp = pl.program_id(1)
    i = pl.program_id(2)
    t0 = i * tq
    dist = _iota((tq, tk), 0) + t0 - _iota((tq, tk), 1)
    c_hi = (t0 + tq + tk - 1) // tk
    c_free = (t0 + 1) // tk
    nh = cc_ref.shape[2]
    outs = []
    for hh in range(LANES // HEAD_DIM):
        h = hp * (LANES // HEAD_DIM) + hh
        lane = hh * HEAD_DIM
        c_col = jnp.sum(jnp.where(_iota((tq, nh), 1) == h, cc_ref[0], 0.0), axis=-1, keepdims=True)

        def bias_fn(c, h=h, c_col=c_col):
            off = pl.multiple_of(c * tk, tk)
            return c_col - cr_ref[0, pl.ds(h, 1), pl.ds(off, tk)]

        def mask_fn(c, s):
            return jnp.where(dist >= c * tk, s, NEG_INF)

        q = q_ref[0, :, lane:lane + HEAD_DIM] * QK_SCALE
        o, _, _ = _flash(q, k_ref, v_ref, lane, lane, 0, c_hi, tk, mask_fn, passes, bias_fn, c_free)
        outs.append(o)
    o_ref[0] = jnp.concatenate(outs, axis=1)


def _fox_prompt(q3, kv_t, c_col, c_row, passes, tq=256, tk=512):
    B, T, W = q3.shape
    hp = W // LANES
    nh = c_row.shape[1]
    return pl.pallas_call(
        functools.partial(_fox_prompt_kernel, tq=tq, tk=tk, T=T, passes=passes), grid=(B, hp, T // tq),
        in_specs=[pl.BlockSpec((1, tq, LANES), lambda b, h, i: (b, i, h)),
                  pl.BlockSpec((1, LANES, T), lambda b, h, i: (b, h, 0)),
                  pl.BlockSpec((1, LANES, T), lambda b, h, i: (b, hp + h, 0)),
                  pl.BlockSpec((1, tq, nh), lambda b, h, i: (b, i, 0)),
                  pl.BlockSpec((1, nh, T), lambda b, h, i: (b, 0, 0))],
        out_specs=pl.BlockSpec((1, tq, LANES), lambda b, h, i: (b, i, h)),
        out_shape=jax.ShapeDtypeStruct((B, T, W), F32),
        compiler_params=_cp("parallel", "parallel", "arbitrary"), name="fox_prompt",
    )(q3, kv_t, kv_t, c_col, c_row)


def _page_specs(n, block, layer, index_fn, lead=()):
    specs = []
    for pi in range(n):
        def imap(b, s, *pf, pi=pi):
            return (layer, index_fn(b, s, pi, *pf)) + lead + (0,) * (len(block) - 2 - len(lead))
        specs.append(pl.BlockSpec(block, imap))
    return specs


def _tdec_init(m_ref, l_ref, acc_ref):
    m_ref[...] = jnp.full(m_ref.shape, NEG_INF, F32)
    l_ref[...] = jnp.zeros(l_ref.shape, F32)
    acc_ref[...] = jnp.zeros(acc_ref.shape, F32)


def _tdec_update(kt, vt, qexp, bias, valid, m_ref, l_ref, acc_ref, r):
    G = kt.shape[0] // HEAD_DIM
    s = jnp.sum((kt * qexp).reshape(G, HEAD_DIM, LANES), axis=1)
    if bias is not None:
        s = s + bias
    if valid is not None:
        s = jnp.where(valid, s, NEG_INF)
    m_old = m_ref[r]
    m_new = jnp.maximum(m_old, jnp.max(s, axis=-1, keepdims=True))
    alpha = jnp.exp(m_old - m_new)
    e = jnp.exp(s - m_new)
    if valid is not None:
        e = jnp.where(valid, e, 0.0)
    l_ref[r] = alpha * l_ref[r] + e
    acc_ref[r] = acc_ref[r] * alpha[:, :, None] + vt.reshape(G, HEAD_DIM, LANES) * e[:, None, :]
    m_ref[r] = m_new


def _tdec_finish(l_ref, acc_ref, r):
    den = jnp.maximum(jnp.sum(l_ref[r], axis=-1, keepdims=True), TINY)
    o = jnp.sum(acc_ref[r], axis=-1, keepdims=True) / den[:, :, None]
    return jnp.broadcast_to(o, acc_ref.shape[1:])


def _tdec_scratch(R, G):
    return [pltpu.VMEM((R, G, 1), F32), pltpu.VMEM((R, G, LANES), F32), pltpu.VMEM((R, G, HEAD_DIM, LANES), F32)]


def _scmp_kernel(pt_ref, *refs, pp):
    pages, wlo_ref, whi_ref, o_ref = refs[:pp], refs[pp], refs[pp + 1], refs[pp + 2]
    rows = wlo_ref.shape[0]
    per = PAGE // A_CMP_STRIDE
    lo = jnp.zeros((rows, LANES), F32)
    hi = jnp.zeros((rows, LANES), F32)
    for pi in range(pp):
        x = pages[pi][...]
        m = _chunk_sum_matrix(pi * per)
        lo = lo + _dotx(x * wlo_ref[...], m, 3)
        hi = hi + _dotx(x * whi_ref[...], m, 3)
    o_ref[0, 0:rows, :] = lo
    o_ref[0, rows:2 * rows, :] = hi


def _nsa_compress_sample(cache_t, layer, page_table, wlo, whi):
    Bd, NP = page_table.shape
    rows = cache_t.shape[2]
    pp = LANES * A_CMP_STRIDE // PAGE
    grid_spec = pltpu.PrefetchScalarGridSpec(
        num_scalar_prefetch=1, grid=(Bd, NP // pp),
        in_specs=_page_specs(pp, (None, None, rows, PAGE), layer, lambda b, s, pi, pt: pt[b, s * pp + pi])
        + [pl.BlockSpec((rows, LANES), lambda b, s, pt: (0, 0))] * 2,
        out_specs=pl.BlockSpec((1, 2 * rows, LANES), lambda b, s, pt: (b, 0, s)))
    return pl.pallas_call(
        functools.partial(_scmp_kernel, pp=pp), grid_spec=grid_spec,
        out_shape=jax.ShapeDtypeStruct((Bd, 2 * rows, NP * PAGE // A_CMP_STRIDE), F32),
        compiler_params=_cp("parallel", "arbitrary"), name="nsa_compress_paged",
    )(page_table, *([cache_t] * pp), wlo, whi)


def _nsa_sample_cmp_kernel(lohi_ref, qt_ref, o_ref, idx_ref, *, P, ncols):
    rows = lohi_ref.shape[1] // 2
    gw = rows // 2
    nch = lohi_ref.shape[2]
    H = qt_ref.shape[1]
    R = H // A_KV
    lpad = -(-(P + 1) // A_SEL_LEN) * A_SEL_LEN
    ncb = lpad // A_CMP_STRIDE - 1
    nsb = lpad // A_SEL_LEN
    qb = P // A_SEL_LEN
    lohi = lohi_ref[0]
    kvb = lohi[0:rows] + pltpu.roll(lohi[rows:2 * rows], nch - 1, 1)
    qbd, keep = _group_block_diag(qt_ref[0] * QK_SCALE, H, A_KV)
    n_io = _iota((H, nch), 1)
    valid = (n_io * A_CMP_STRIDE + (A_CMP_LEN - 1) <= P) & (n_io < ncb)
    s = jnp.where(valid, _dot3(qbd, kvb[0:gw], NN), NEG_INF)
    m = jnp.max(s, axis=-1, keepdims=True)
    e = jnp.where(valid, jnp.exp(s - m), 0.0)
    p = e / jnp.maximum(jnp.sum(e, axis=-1, keepdims=True), TINY)
    o_ref[0] = _fold_groups(_dot3(p, kvb[gw:rows], NT), keep, A_KV)
    ratio = A_SEL_LEN // A_CMP_STRIDE
    back = A_CMP_LEN // A_CMP_STRIDE - 1
    mn, mj = _iota((nch, ncols), 0), _iota((nch, ncols), 1)
    imp_mat = _onehot((mn >= mj * ratio - back) & (mn <= mj * ratio + ratio - 1) & (mn < ncb))
    j_io = _iota((8, ncols), 1)
    jf = j_io.astype(F32)
    lane = _iota((8, LANES), 1)
    row = _iota((8, LANES), 0)
    out = jnp.zeros((8, LANES), F32)
    for g in range(A_KV):
        pg = jnp.sum(p[R * g:R * (g + 1)], axis=0, keepdims=True)
        imp = _dotx(jnp.broadcast_to(pg, (8, nch)), imp_mat, 3)
        score = jnp.where(j_io <= qb, jnp.where((j_io == 0) | (j_io == qb), A_FORCED, imp), -1.0)
        score = jnp.where(j_io < nsb, score, -2.0)
        for k in range(min(A_SEL_TOPK, nsb)):
            best = jnp.max(score, axis=-1, keepdims=True)
            pick = jnp.min(jnp.where(score == best, jf, 1e9), axis=-1, keepdims=True)
            out = jnp.where((lane == k) & (row == g), pick, out)
            score = jnp.where(jf == pick, -3.0, score)
    idx_ref[0] = out.astype(I32)


def _nsa_sample_cmp(lohi, qt3, P):
    Bd, rows2, nch = lohi.shape
    H, gw = qt3.shape[1], qt3.shape[2]
    nsb = -(-(P + 1) // A_SEL_LEN)
    ncols = -(-nsb // LANES) * LANES
    return pl.pallas_call(
        functools.partial(_nsa_sample_cmp_kernel, P=P, ncols=ncols), grid=(Bd,),
        in_specs=[pl.BlockSpec((1, rows2, nch), lambda b: (b, 0, 0)), pl.BlockSpec((1, H, gw), lambda b: (b, 0, 0))],
        out_specs=[pl.BlockSpec((1, H, HEAD_DIM), lambda b: (b, 0, 0)), pl.BlockSpec((1, 8, LANES), lambda b: (b, 0, 0))],
        out_shape=[jax.ShapeDtypeStruct((Bd, H, HEAD_DIM), F32), jax.ShapeDtypeStruct((Bd, 8, LANES), I32)],
        compiler_params=_cp("parallel"), name="nsa_sample_cmp_select",
    )(lohi, qt3)


def _nsa_sample_sel_kernel(pt_ref, ix_ref, *refs, P, nk):
    kblks, vblks = refs[0:A_KV], refs[A_KV:2 * A_KV]
    q_ref, new_ref, o_ref, m_ref, l_ref, acc_ref = refs[2 * A_KV:]
    b = pl.program_id(0)
    k = pl.program_id(1)
    R = q_ref.shape[2] // HEAD_DIM
    qb = P // A_SEL_LEN
    half_shift = A_SEL_LEN.bit_length() - 1
    lane = _iota((1, PAGE), 1)

    @pl.when(k == 0)
    def _():
        _tdec_init(m_ref, l_ref, acc_ref)

    for g in range(A_KV):
        j = ix_ref[b, g * nk + k]
        valid = ((lane >> half_shift) == (j & (PAGE // A_SEL_LEN - 1))) & (j < qb)
        kt = jnp.concatenate([kblks[g][...]] * R, axis=0)
        vt = jnp.concatenate([vblks[g][...]] * R, axis=0)
        _tdec_update(kt, vt, q_ref[0, g] * QK_SCALE, None, valid, m_ref, l_ref, acc_ref, g)

    @pl.when(k == nk - 1)
    def _():
        for g in range(A_KV):
            hit = ix_ref[b, g * nk] == qb
            for kk in range(1, nk):
                hit = hit | (ix_ref[b, g * nk + kk] == qb)
            kt = jnp.concatenate([new_ref[0, 0, g]] * R, axis=0)
            vt = jnp.concatenate([new_ref[0, 1, g]] * R, axis=0)
            _tdec_update(kt, vt, q_ref[0, g] * QK_SCALE, None, (lane == 0) & hit, m_ref, l_ref, acc_ref, g)
            o_ref[0, g] = _tdec_finish(l_ref, acc_ref, g)


def _nsa_sample_sel(cache_t6, layer, page_table, idx_flat, qexp, new_cols, P):
    Bd, G, rw = qexp.shape[0], qexp.shape[1], qexp.shape[2]
    R = rw // HEAD_DIM
    nk = idx_flat.shape[1] // A_KV
    per = PAGE // A_SEL_LEN
    last = P // A_SEL_LEN - 1

    def blk_index(c, g):
        def imap(b, k, pt, ix):
            j = jnp.minimum(ix[b, g * nk + k], last)
            return (layer, pt[b, j // per], c, g, 0, 0)
        return imap

    per_b = lambda shape: pl.BlockSpec((1,) + shape, lambda b, k, pt, ix: (b,) + (0,) * len(shape))
    blk = (None, None, None, None, HEAD_DIM, PAGE)
    grid_spec = pltpu.PrefetchScalarGridSpec(
        num_scalar_prefetch=2, grid=(Bd, nk),
        in_specs=[pl.BlockSpec(blk, blk_index(0, g)) for g in range(A_KV)]
        + [pl.BlockSpec(blk, blk_index(1, g)) for g in range(A_KV)]
        + [per_b((G, rw, LANES)), per_b((2, G, HEAD_DIM, LANES))],
        out_specs=per_b((G, R, HEAD_DIM, LANES)),
        scratch_shapes=_tdec_scratch(G, R))
    return pl.pallas_call(
        functools.partial(_nsa_sample_sel_kernel, P=P, nk=nk), grid_spec=grid_spec,
        out_shape=jax.ShapeDtypeStruct((Bd, G, R, HEAD_DIM, LANES), F32),
        compiler_params=_cp("parallel", "arbitrary"), name="nsa_sample_selected",
    )(page_table, idx_flat, *([cache_t6] * (2 * A_KV)), qexp, new_cols)


def _nsa_merge_kernel(gl_ref, ocmp_ref, osel_ref, owin_ref, o_ref):
    gates = _sigmoid(gl_ref[...])
    o_ref[...] = gates[:, :, 0:1] * ocmp_ref[...] + gates[:, :, 1:2] * osel_ref[...] + gates[:, :, 2:3] * owin_ref[...]


def _nsa_sample_merge(gl3, ocmp3, osel3, owin3):
    whole = lambda a: pl.BlockSpec(a.shape, lambda i: (0,) * a.ndim)
    return pl.pallas_call(
        _nsa_merge_kernel, grid=(1,),
        in_specs=[whole(gl3), whole(ocmp3), whole(osel3), whole(owin3)], out_specs=whole(ocmp3),
        out_shape=jax.ShapeDtypeStruct(ocmp3.shape, F32),
        compiler_params=_cp("arbitrary"), name="nsa_sample_merge",
    )(gl3, ocmp3, osel3, owin3)


def _slide_lanes(buf, new_cols):
    W = buf.shape[-1]
    axis = buf.ndim - 1
    new = jnp.concatenate([new_cols] * (W // LANES), axis=axis)
    return jnp.where(_iota(buf.shape, axis) == W - 1, new, pltpu.roll(buf, W - 1, axis))


def _nsa_sample_win_kernel(buf_ref, new_ref, qt_ref, nbuf_ref, o_ref):
    rows = buf_ref.shape[1]
    gw = rows // 2
    H = qt_ref.shape[1]
    nb = _slide_lanes(buf_ref[0], new_ref[0])
    nbuf_ref[0] = nb
    qbd, keep = _group_block_diag(qt_ref[0] * QK_SCALE, H, A_KV)
    s = _dot3(qbd, nb[0:gw], NN)
    e = jnp.exp(s - jnp.max(s, axis=-1, keepdims=True))
    p = e / jnp.maximum(jnp.sum(e, axis=-1, keepdims=True), TINY)
    o_ref[0] = _fold_groups(_dot3(p, nb[gw:rows], NT), keep, A_KV)


def _nsa_sample_win(buf_t, new_cols, qt3):
    Bd, rows, W = buf_t.shape
    H, gw = qt3.shape[1], qt3.shape[2]
    return pl.pallas_call(
        _nsa_sample_win_kernel, grid=(Bd,),
        in_specs=[pl.BlockSpec((1, rows, W), lambda b: (b, 0, 0)), pl.BlockSpec((1, rows, LANES), lambda b: (b, 0, 0)),
                  pl.BlockSpec((1, H, gw), lambda b: (b, 0, 0))],
        out_specs=[pl.BlockSpec((1, rows, W), lambda b: (b, 0, 0)), pl.BlockSpec((1, H, HEAD_DIM), lambda b: (b, 0, 0))],
        out_shape=[jax.ShapeDtypeStruct((Bd, rows, W), F32), jax.ShapeDtypeStruct((Bd, H, HEAD_DIM), F32)],
        compiler_params=_cp("parallel"), name="nsa_sample_window",
    )(buf_t, new_cols, qt3)


def _dsa_idx_kernel(pt_ref, *refs, pp):
    pages, qi_ref, wi_ref, o_ref = refs[:pp], refs[pp], refs[pp + 1], refs[pp + 2]
    wi = wi_ref[0] * (B_IDX_HEADS * B_IDX_DIM) ** -0.5
    qi = qi_ref[0]
    for pi in range(pp):
        logits = _dot3(qi, pages[pi][...], NN)
        o_ref[0, pi] = jnp.sum(wi * jnp.maximum(logits, 0.0), axis=0, keepdims=True)


def _dsa_sample_scores(kidx_t, layer, page_table, qi3, wi3, pp=8):
    Bd, NP = page_table.shape
    grid_spec = pltpu.PrefetchScalarGridSpec(
        num_scalar_prefetch=1, grid=(Bd, NP // pp),
        in_specs=_page_specs(pp, (None, None, B_IDX_DIM, PAGE), layer, lambda b, s, pi, pt: pt[b, s * pp + pi])
        + [pl.BlockSpec((1, B_IDX_HEADS, B_IDX_DIM), lambda b, s, pt: (b, 0, 0)),
           pl.BlockSpec((1, B_IDX_HEADS, 1), lambda b, s, pt: (b, 0, 0))],
        out_specs=pl.BlockSpec((1, pp, 1, PAGE), lambda b, s, pt: (b, s, 0, 0)))
    return pl.pallas_call(
        functools.partial(_dsa_idx_kernel, pp=pp), grid_spec=grid_spec,
        out_shape=jax.ShapeDtypeStruct((Bd, NP, 1, PAGE), F32),
        compiler_params=_cp("parallel", "arbitrary"), name="dsa_sample_indexer",
    )(page_table, *([kidx_t] * pp), qi3, wi3)


def _dsa_select_kernel(sc_ref, qi_ref, wi_ref, kin_ref, o_ref, *, topk):
    Bd, P = sc_ref.shape
    wi = wi_ref[...] * (B_IDX_HEADS * B_IDX_DIM) ** -0.5
    logit = jnp.sum(qi_ref[...] * kin_ref[...], axis=-1, keepdims=True)
    s_new = jnp.sum(wi * jnp.maximum(logit, 0.0), axis=1)
    tail = jnp.where(_iota((Bd, LANES), 1) == 0, s_new, -jnp.inf)
    full = jnp.concatenate([sc_ref[...], tail], axis=1)
    o_ref[...] = jnp.where(_topk_mask(_sortable_key(full), topk), 1.0, 0.0)


def _dsa_sample_select(scores2, qi3, wi3, kinew3, topk):
    Bd, P = scores2.shape
    whole = lambda shape: pl.BlockSpec(shape, lambda i: (0,) * len(shape))
    return pl.pallas_call(
        functools.partial(_dsa_select_kernel, topk=topk), grid=(1,),
        in_specs=[whole(scores2.shape), whole(qi3.shape), whole(wi3.shape), whole(kinew3.shape)],
        out_specs=whole((Bd, P + LANES)), out_shape=jax.ShapeDtypeStruct((Bd, P + LANES), F32),
        compiler_params=_cp("arbitrary"), name="dsa_sample_topk",
    )(scores2, qi3, wi3, kinew3)


def _dsa_sample_attn_kernel(pt_ref, *refs, pp, nsteps):
    pages = refs[:pp]
    mask_ref, q_ref, new_ref, mnew_ref, o_ref, m_ref, l_ref, acc_ref = refs[pp:]
    s_id = pl.program_id(1)
    R = q_ref.shape[1]
    gw = new_ref.shape[1] // 2

    @pl.when(s_id == 0)
    def _():
        _tdec_init(m_ref, l_ref, acc_ref)

    for pi in range(pp):
        kv = pages[pi][...]
        valid = mask_ref[0, pi] > 0.5
        for r in range(R):
            _tdec_update(kv[0:gw], kv[gw:2 * gw], q_ref[0, r] * QK_SCALE, None, valid, m_ref, l_ref, acc_ref, r)

    @pl.when(s_id == nsteps - 1)
    def _():
        new = new_ref[0]
        valid = (_iota((1, LANES), 1) == 0) & (mnew_ref[0] > 0.5)
        for r in range(R):
            _tdec_update(new[0:gw], new[gw:2 * gw], q_ref[0, r] * QK_SCALE, None, valid, m_ref, l_ref, acc_ref, r)
            o_ref[0, r] = _tdec_finish(l_ref, acc_ref, r)


def _dsa_sample_attn(kv_t, layer, page_table, mask4, qexp, new_cols, mnew3, pp=8):
    Bd, NP = page_table.shape
    rows = kv_t.shape[2]
    R, gw = qexp.shape[1], qexp.shape[2]
    G = gw // HEAD_DIM
    nsteps = NP // pp
    per_b = lambda shape: pl.BlockSpec((1,) + shape, lambda b, s, pt: (b,) + (0,) * len(shape))
    grid_spec = pltpu.PrefetchScalarGridSpec(
        num_scalar_prefetch=1, grid=(Bd, nsteps),
        in_specs=_page_specs(pp, (None, None, rows, PAGE), layer, lambda b, s, pi, pt: pt[b, s * pp + pi])
        + [pl.BlockSpec((1, pp, 1, PAGE), lambda b, s, pt: (b, s, 0, 0)), per_b((R, gw, LANES)), per_b((rows, LANES)),
           per_b((1, LANES))],
        out_specs=per_b((R, G, HEAD_DIM, LANES)),
        scratch_shapes=_tdec_scratch(R, G))
    return pl.pallas_call(
        functools.partial(_dsa_sample_attn_kernel, pp=pp, nsteps=nsteps), grid_spec=grid_spec,
        out_shape=jax.ShapeDtypeStruct((Bd, R, G, HEAD_DIM, LANES), F32),
        compiler_params=_cp("parallel", "arbitrary"), name="dsa_sample_attention",
    )(page_table, *([kv_t] * pp), mask4, qexp, new_cols, mnew3)


def _dil_sample_kernel(*refs, P):
    ng = len(C_GROUPS)
    buf_refs, new_refs, q_refs = refs[0:ng], refs[ng:2 * ng], refs[2 * ng:3 * ng]
    nbuf_refs, o_ref = refs[3 * ng:4 * ng], refs[4 * ng]
    res = []
    for (w, r), buf_ref, new_ref, q_ref, nbuf_ref in zip(C_GROUPS, buf_refs, new_refs, q_refs, nbuf_refs):
        buf = buf_ref[...]
        new = new_ref[...]
        W = buf.shape[2]
        nbuf_ref[...] = _slide_lanes(buf, new)
        q = q_ref[...] * QK_SCALE
        qw = jnp.concatenate([q] * (W // LANES), axis=1)
        s_old = jnp.sum(buf[0] * qw, axis=0, keepdims=True)
        dist = W - _iota((1, W), 1)
        valid = ((dist & (r - 1)) == 0) & (dist <= P)
        s_old = jnp.where(valid, s_old, NEG_INF)
        s_new = jnp.sum(new[0] * q, axis=0, keepdims=True)
        m = jnp.maximum(jnp.max(s_old, axis=-1, keepdims=True), s_new)
        e_old = jnp.where(valid, jnp.exp(s_old - m[:, 0:1]), 0.0)
        e_new = jnp.exp(s_new - m)
        den = jnp.sum(e_old, axis=-1, keepdims=True) + e_new
        o = (jnp.sum(buf[1] * e_old, axis=-1, keepdims=True) + new[1] * e_new) / jnp.maximum(den, TINY)
        res.append((o, m, den))
    m_all = functools.reduce(jnp.maximum, [m for _, m, _ in res])
    wts = [den * jnp.exp(m - m_all) for _, m, den in res]
    tot = functools.reduce(lambda a, b: a + b, wts)
    o_ref[...] = functools.reduce(lambda a, b: a + b, [(wt / tot) * o for wt, (o, _, _) in zip(wts, res)])


def _dil_sample(bufs_t, layer, news_cols, qs_cols, P):
    Bd, H = qs_cols[0].shape[0], qs_cols[0].shape[1]
    in_specs, out_specs, out_shape = [], [], []
    for buf in bufs_t:
        W = buf.shape[5]
        in_specs.append(pl.BlockSpec((None, None, 2, None, HEAD_DIM, W), lambda b, h: (layer, b, 0, h, 0, 0)))
        out_specs.append(pl.BlockSpec((None, 2, None, HEAD_DIM, W), lambda b, h: (b, 0, h, 0, 0)))
        out_shape.append(jax.ShapeDtypeStruct((Bd, 2, H, HEAD_DIM, W), F32))
    in_specs += [pl.BlockSpec((None, 2, None, HEAD_DIM, LANES), lambda b, h: (b, 0, h, 0, 0))] * len(bufs_t)
    in_specs += [pl.BlockSpec((None, None, HEAD_DIM, LANES), lambda b, h: (b, h, 0, 0))] * len(bufs_t)
    out_specs.append(pl.BlockSpec((None, None, HEAD_DIM, LANES), lambda b, h: (b, h, 0, 0)))
    out_shape.append(jax.ShapeDtypeStruct((Bd, H, HEAD_DIM, LANES), F32))
    return pl.pallas_call(
        functools.partial(_dil_sample_kernel, P=P), grid=(Bd, H),
        in_specs=in_specs, out_specs=out_specs, out_shape=out_shape,
        compiler_params=_cp("parallel", "parallel"), name="dilated_sample",
    )(*bufs_t, *news_cols, *qs_cols)


def _fox_sample_kernel(pt_ref, *refs, pp, nsteps):
    pages, lfs = refs[:pp], refs[pp:2 * pp]
    q_ref, new_ref, lfnew_ref, o_ref, m_ref, l_ref, acc_ref, carry_ref = refs[2 * pp:]
    s_id = pl.program_id(1)
    L = new_ref.shape[1] // 2
    q = q_ref[0, 0] * QK_SCALE

    @pl.when(s_id == 0)
    def _():
        _tdec_init(m_ref, l_ref, acc_ref)
        new = new_ref[0]
        _tdec_update(new[0:L], new[L:2 * L], q, None, _iota((1, LANES), 1) == 0, m_ref, l_ref, acc_ref, 0)
        carry_ref[...] = lfnew_ref[0]

    later = _onehot(_iota((PAGE, PAGE), 0) > _iota((PAGE, PAGE), 1))
    for pi in range(pp):
        kv = pages[pi][...]
        lf = lfs[pi][...]
        carry = carry_ref[...]
        bias = _dotx(lf, later, 3) + carry
        carry_ref[...] = carry + jnp.sum(lf, axis=-1, keepdims=True)
        _tdec_update(kv[0:L], kv[L:2 * L], q, bias, None, m_ref, l_ref, acc_ref, 0)

    @pl.when(s_id == nsteps - 1)
    def _():
        o_ref[0, 0] = _tdec_finish(l_ref, acc_ref, 0)


def _fox_sample(kv_t, lf_t, layer, page_table, qexp, new_cols, lfnew3, pp=4):
    Bd, NP = page_table.shape
    rows = kv_t.shape[2]
    H = lf_t.shape[2]
    nsteps = NP // pp
    rev = lambda b, s, pi, pt: pt[b, NP - 1 - (s * pp + pi)]
    per_b = lambda shape: pl.BlockSpec((1,) + shape, lambda b, s, pt: (b,) + (0,) * len(shape))
    grid_spec = pltpu.PrefetchScalarGridSpec(
        num_scalar_prefetch=1, grid=(Bd, nsteps),
        in_specs=_page_specs(pp, (None, None, rows, PAGE), layer, rev) + _page_specs(pp, (None, None, H, PAGE), layer, rev)
        + [per_b((1, rows // 2, LANES)), per_b((rows, LANES)), per_b((H, LANES))],
        out_specs=per_b((1, H, HEAD_DIM, LANES)),
        scratch_shapes=_tdec_scratch(1, H) + [pltpu.VMEM((H, LANES), F32)])
    return pl.pallas_call(
        functools.partial(_fox_sample_kernel, pp=pp, nsteps=nsteps), grid_spec=grid_spec,
        out_shape=jax.ShapeDtypeStruct((Bd, 1, H, HEAD_DIM, LANES), F32),
        compiler_params=_cp("parallel", "arbitrary"), name="fox_sample",
    )(page_table, *([kv_t] * pp), *([lf_t] * pp), qexp, new_cols, lfnew3)


def _cols(x, lead):
    Bd = x.shape[0]
    return jnp.broadcast_to(x.reshape((Bd,) + lead + (1,)), (Bd,) + lead + (LANES,))


def _rows_minor(cache):
    n = cache.ndim
    return jnp.moveaxis(cache, n - 4, n - 1)


def _rows_major(x):
    n = x.ndim
    return jnp.moveaxis(x, n - 1, n - 4)


def kernel(x_prompt, x_sample, cache_a_cmp, cache_a_sel, cache_a_win, cache_b_kv, cache_b_kidx, cache_c_win0, cache_c_win1, cache_c_win2, cache_d_kv, cache_d_logf, page_table, c_prompt, c_sample, ada_w, ada_b, norm_g, final_g, a_w_in, a_w_out, a_cmp_wk, a_cmp_wv, b_w_in, b_w_out, c_w_in, c_w_out, d_w_in, d_w_out, d_f_bias):
    B, T, D = x_prompt.shape
    Bd = x_sample.shape[0]
    depth = ada_w.shape[0]
    NP = page_table.shape[1]
    P = NP * PAGE
    pool = cache_b_kidx.shape[1]
    assert x_sample.shape[1] == 1 and D == 1024 and P % A_SEL_LEN == 0
    assert cache_a_win.shape[2] == A_WINDOW
    assert all(b.shape[2] == w for b, (w, _) in zip((cache_c_win0, cache_c_win1, cache_c_win2), C_GROUPS))
    H = D // HEAD_DIM
    R = H // A_KV
    tm_p = 256
    tpb = T // tm_p

    mod = _mod_all(jnp.concatenate([c_prompt, c_sample], axis=0), ada_w, ada_b)
    pos_p = jnp.arange(T, dtype=I32)
    pos_s = jnp.full((Bd,), P, I32)
    tab_p, ttab_p = _rope_tables(pos_p), _rope_tables_t(pos_p)
    tab_s = _rope_tables(pos_s)
    xp = x_prompt.reshape(B * T, D)
    xs = x_sample.reshape(Bd, D)
    c_bufs = (cache_c_win0, cache_c_win1, cache_c_win2)
    st = {}
    put = lambda name, val: st.setdefault(name, []).append(val)
    rows_of = lambda t, c, g: _rows_major(t.reshape(t.shape[0], c, g, HEAD_DIM, t.shape[2]))
    yp = ys = None
    for i in range(depth):
        kind, li = i % 4, i // 4
        shift_p, scale_p, gate_p = [mod[i, :B, j * D:(j + 1) * D].reshape(B, 1, D) for j in range(3)]
        shift_s, scale_s, gate_s = [mod[i, B:, j * D:(j + 1) * D].reshape(1, Bd, D) for j in range(3)]
        proj_p = lambda w, segs: _proj(xp, scale_p, shift_p, norm_g[i], w, segs, tab_p, ttab_p, tpb, tm_p, PASSES_PROMPT_PROJ)
        proj_s = lambda w, segs: _proj(xs, scale_s, shift_s, norm_g[i], w, segs, tab_s, None, 1, Bd, PASSES_SELECTIVE)
        if kind == 0:
            w = a_w_in[li]
            gw = A_KV * HEAD_DIM
            q, qr, gl, z, cmp_t, sel_t, win_t = proj_p(w, [
                ("rm", 0, D, ("n", "r")), ("rm", D + 6 * gw, 3 * H, ("n",)), ("rm", D + 6 * gw + 3 * H, D, ("n",)),
                ("t", D, 2 * gw, "t"), ("t", D + 2 * gw, 2 * gw, "trk"), ("t", D + 4 * gw, 2 * gw, "trk")])
            wlo, whi = _cmp_taps(a_cmp_wk[li], a_cmp_wv[li], A_KV)
            kvb = _nsa_compress_prompt(cmp_t, wlo, whi)
            o = _nsa_prompt(q.reshape(B, T, D), qr.reshape(B, T, D), kvb, sel_t, win_t, gl.reshape(B, T, LANES),
                            PASSES_NSA_FLASH)
            op, zp, wo = o.reshape(B * T, D), z, a_w_out[li]
            put("a_cmp_p", rows_of(cmp_t, 2, A_KV))
            put("a_sel_p", rows_of(sel_t, 2, A_KV))
            put("a_win_p", rows_of(win_t[:, :, T - min(A_WINDOW, T):], 2, A_KV))

            q, qr, cmp_, sel, win, gl, z = proj_s(w, [
                ("rm", 0, D, ("n", "r")), ("rm", D, 2 * gw, ("n",)), ("rm", D + 2 * gw, 2 * gw, ("rk",)),
                ("rm", D + 4 * gw, 2 * gw, ("rk",)), ("rm", D + 6 * gw, 3 * H, ("n",)), ("rm", D + 6 * gw + 3 * H, D, ("n",))])
            cmp_t6 = _rows_minor(cache_a_cmp)
            lohi = _nsa_compress_sample(cmp_t6.reshape(cmp_t6.shape[0], pool, 2 * gw, PAGE), li, page_table, wlo, whi)
            tile_g = lambda t: jnp.tile(t.reshape(Bd, H, HEAD_DIM), (1, 1, A_KV))
            ocmp, idx = _nsa_sample_cmp(lohi, tile_g(q), P)
            win_t6 = _rows_minor(cache_a_win)
            nwin, owin = _nsa_sample_win(win_t6[li].reshape(Bd, 2 * gw, A_WINDOW), _cols(win, (2 * gw,)), tile_g(qr))
            nk = min(A_SEL_TOPK, -(-(P + 1) // A_SEL_LEN))
            osel = _nsa_sample_sel(_rows_minor(cache_a_sel), li, page_table, idx[:, :A_KV, :nk].reshape(Bd, A_KV * nk),
                                   _cols(qr, (A_KV, R * HEAD_DIM)), _cols(sel, (2, A_KV, HEAD_DIM)), P)
            o = _nsa_sample_merge(gl[:, :3 * H].reshape(Bd, H, 3), ocmp, osel[..., 0].reshape(Bd, H, HEAD_DIM), owin)
            os_, zs = o.reshape(Bd, D), z
            put("a_cmp_s", cmp_.reshape(Bd, 1, 2, A_KV, HEAD_DIM))
            put("a_sel_s", sel.reshape(Bd, 1, 2, A_KV, HEAD_DIM))
            put("a_win_s", _rows_major(nwin.reshape(Bd, 2, A_KV, HEAD_DIM, A_WINDOW)))
        elif kind == 1:
            w = b_w_in[li]
            gw = A_KV * HEAD_DIM
            c_qi = D + 2 * gw
            c_ki = c_qi + B_IDX_HEADS * B_IDX_DIM
            c_wi = c_ki + B_IDX_DIM
            c_z = c_wi + B_IDX_HEADS
            q, qi, wi, z, kv_t, ki_t = proj_p(w, [
                ("rm", 0, D, ("r",)), ("rm", c_qi, c_ki - c_qi, ("r",)), ("rm", c_wi, B_IDX_HEADS, ("n",)), ("rm", c_z, D, ("n",)),
                ("t", D, 2 * gw, "trk"), ("t", c_ki, B_IDX_DIM, "tr")])
            o = _dsa_prompt(q.reshape(B, T, D), qi.reshape(B, T, c_ki - c_qi), wi.reshape(B, T, LANES), kv_t, ki_t,
                            min(B_TOPK_MAX, T // 4), PASSES_SMOOTH, PASSES_PROMPT_INDEXER)
            op, zp, wo = o.reshape(B * T, D), z, b_w_out[li]
            put("b_kv_p", rows_of(kv_t, 2, A_KV))
            put("b_kidx_p", jnp.swapaxes(ki_t, 1, 2))

            q, kv, qi, ki, wi, z = proj_s(w, [
                ("rm", 0, D, ("r",)), ("rm", D, 2 * gw, ("rk",)), ("rm", c_qi, c_ki - c_qi, ("r",)), ("rm", c_ki, B_IDX_DIM, ("r",)),
                ("rm", c_wi, B_IDX_HEADS, ("n",)), ("rm", c_z, D, ("n",))])
            qi3 = qi.reshape(Bd, B_IDX_HEADS, B_IDX_DIM)
            wi3 = wi[:, :B_IDX_HEADS].reshape(Bd, B_IDX_HEADS, 1)
            ki_new = ki[:, :B_IDX_DIM]
            sc = _dsa_sample_scores(jnp.swapaxes(cache_b_kidx, 2, 3), li, page_table, qi3, wi3)
            mask = _dsa_sample_select(sc.reshape(Bd, P), qi3, wi3, ki_new.reshape(Bd, 1, B_IDX_DIM),
                                      min(B_TOPK_MAX, (P + 1) // 4))
            kv_t6 = _rows_minor(cache_b_kv)
            qexp = _cols(q.reshape(Bd, A_KV, R, HEAD_DIM).transpose(0, 2, 1, 3).reshape(Bd, R, gw), (R, gw))
            o = _dsa_sample_attn(kv_t6.reshape(kv_t6.shape[0], pool, 2 * gw, PAGE), li, page_table,
                                 mask[:, :P].reshape(Bd, NP, 1, PAGE), qexp, _cols(kv, (2 * gw,)),
                                 jnp.broadcast_to(mask[:, P:P + 1], (Bd, LANES)).reshape(Bd, 1, LANES))
            os_, zs = o[..., 0].transpose(0, 2, 1, 3).reshape(Bd, D), z
            put("b_kv_s", kv.reshape(Bd, 1, 2, A_KV, HEAD_DIM))
            put("b_kidx_s", ki_new.reshape(Bd, 1, B_IDX_DIM))
        elif kind == 2:
            w = c_w_in[li]
            Wd = C_HEADS * HEAD_DIM
            ng = len(C_GROUPS)
            res = proj_p(w, [("rm", 3 * g * Wd, Wd, ("r",)) for g in range(ng)] + [("rm", 3 * ng * Wd, Wd, ("n",))]
                         + [("t", (3 * g + 1) * Wd, 2 * Wd, "trk") for g in range(ng)])
            qs, z, kvs_t = res[0:ng], res[ng], res[ng + 1:]
            o = _dil_prompt([t.reshape(B, T, Wd) for t in qs], kvs_t, PASSES_SMOOTH)
            op, zp, wo = o.reshape(B * T, Wd), z, c_w_out[li]
            for g, (wg, _) in enumerate(C_GROUPS):
                put("c_win%d_p" % g, rows_of(kvs_t[g][:, :, T - min(wg, T):], 2, C_HEADS))

            res = proj_s(w, [seg for g in range(ng) for seg in (("rm", 3 * g * Wd, Wd, ("r",)), ("rm", (3 * g + 1) * Wd, 2 * Wd, ("rk",)))]
                         + [("rm", 3 * ng * Wd, Wd, ("n",))])
            qs, kvs, z = res[0:2 * ng:2], res[1:2 * ng:2], res[2 * ng]
            outs = _dil_sample([_rows_minor(b) for b in c_bufs], li, [_cols(t, (2, C_HEADS, HEAD_DIM)) for t in kvs],
                               [_cols(t, (C_HEADS, HEAD_DIM)) for t in qs], P)
            os_, zs = outs[ng][..., 0].reshape(Bd, Wd), z
            for g in range(ng):
                put("c_win%d_s" % g, _rows_major(outs[g]))
        else:
            w = d_w_in[li]
            q, z, kv_t, f_t = proj_p(w, [("rm", 0, D, ("n",)), ("rm", 3 * D + H, D, ("n",)), ("t", D, 2 * D, "t"), ("t", 3 * D, H, "t")])
            fb = d_f_bias[li].reshape(H, 1)
            lf_t, c_t = _fox_logf(f_t, fb)
            o = _fox_prompt(q.reshape(B, T, D), kv_t, jnp.swapaxes(c_t, 1, 2), c_t, PASSES_SMOOTH)
            op, zp, wo = o.reshape(B * T, D), z, d_w_out[li]
            put("d_kv_p", rows_of(kv_t, 2, H))
            put("d_logf_p", jnp.swapaxes(lf_t, 1, 2))

            q, kv, f, z = proj_s(w, [("rm", 0, D, ("n",)), ("rm", D, 2 * D, ("n",)), ("rm", 3 * D, H, ("n",)), ("rm", 3 * D + H, D, ("n",))])
            lf = _fox_logf(f[:, :H].T.reshape(1, H, Bd), fb)[0][0].T
            kv_t6 = _rows_minor(cache_d_kv)
            o = _fox_sample(kv_t6.reshape(kv_t6.shape[0], pool, 2 * D, PAGE), jnp.swapaxes(cache_d_logf, 2, 3), li, page_table,
                            _cols(q, (1, D)), _cols(kv, (2 * D,)), _cols(lf, (H,)))
            os_, zs = o[..., 0].reshape(Bd, D), z
            put("d_kv_s", kv.reshape(Bd, 1, 2, H, HEAD_DIM))
            put("d_logf_s", lf.reshape(Bd, 1, H))
        fg = final_g if i == depth - 1 else None
        xp, yp = _gated_out(op, zp, wo, xp, gate_p, tpb, tm_p, PASSES_PROMPT_PROJ, fg)
        xs, ys = _gated_out(os_, zs, wo, xs, gate_s, 1, Bd, PASSES_SELECTIVE, fg)
    ns = {n: jnp.stack(v) for n, v in st.items()}
    names = ("a_cmp_p", "a_cmp_s", "a_sel_p", "a_sel_s", "a_win_p", "a_win_s", "b_kv_p", "b_kv_s", "b_kidx_p", "b_kidx_s",
             "c_win0_p", "c_win0_s", "c_win1_p", "c_win1_s", "c_win2_p", "c_win2_s", "d_kv_p", "d_kv_s", "d_logf_p", "d_logf_s")
    return (yp.reshape(B, T, D), ys.reshape(Bd, 1, D)) + tuple(ns[n] for n in names)
```

```python
import functools

import jax
import jax.numpy as jnp
from jax import lax
from jax.experimental import pallas as pl
from jax.experimental.pallas import tpu as pltpu

F32 = jnp.float32
BF16 = jnp.bfloat16
I32 = jnp.int32

HEAD_DIM = 64
ROPE_DIMS = HEAD_DIM // 4
ROPE_HALF = ROPE_DIMS // 2
ROPE_THETA = 500000.0
NORM_EPS = 1e-6
NEG_INF = -1e30
TINY = 1e-30
PAGE = 128
LANES = 128
INT_MIN = -2 ** 31

A_KV = 4
A_CMP_STRIDE = 16
A_CMP_LEN = 32
A_SEL_LEN = 64
A_SEL_TOPK = 16
A_WINDOW = 512
A_FORCED = 1e4
B_IDX_HEADS = 8
B_IDX_DIM = 64
B_TOPK_MAX = 256
C_GROUPS = ((128, 1), (512, 4), (2048, 16))
C_HEADS = 8
QK_SCALE = HEAD_DIM ** -0.5

NN = ((1,), (0,))
NT = ((1,), (1,))

VMEM_LIMIT = 56 * 1024 * 1024
PROJ_COLS_PER_CALL = 2048

PASSES_SELECTIVE = 3
PASSES_SMOOTH = 1
PASSES_NSA_FLASH = 1
PASSES_PROMPT_PROJ = 1
PASSES_PROMPT_INDEXER = 1


def _cp(*sem):
    return pltpu.CompilerParams(dimension_semantics=sem, vmem_limit_bytes=VMEM_LIMIT)


def _dotf(a, b, dims=NN):
    return lax.dot_general(a, b, (dims, ((), ())), preferred_element_type=F32)


def _split(a, terms):
    out = []
    for _ in range(terms - 1):
        h = a.astype(BF16)
        out.append(h)
        a = a - h.astype(F32)
    out.append(a.astype(BF16))
    return out


def _dot3(a, b, dims=NN):
    ah, al = _split(a, 2)
    bh, bl = _split(b, 2)
    return _dotf(ah, bh, dims) + (_dotf(ah, bl, dims) + _dotf(al, bh, dims))


def _dotp(a, b, dims, passes):
    if passes == 1:
        return _dotf(a.astype(BF16), b.astype(BF16), dims)
    return _dot3(a, b, dims)


def _dotx(a, m01, terms, dims=NN):
    acc = None
    for t in _split(a, terms):
        d = _dotf(t, m01, dims)
        acc = d if acc is None else acc + d
    return acc


def _iota(shape, axis):
    return lax.broadcasted_iota(I32, shape, axis)


def _onehot(cond):
    return jnp.where(cond, 1.0, 0.0).astype(BF16)


def _sigmoid(x):
    return 1.0 / (1.0 + jnp.exp(-x))


def _sortable_key(x):
    x = jnp.where(x == 0.0, 0.0, x)
    b = lax.bitcast_convert_type(x, I32)
    return jnp.where(b < 0, b ^ 0x7FFFFFFF, b)


def _count_ge(key, cand):
    return jnp.sum(jnp.where(key >= cand, 1.0, 0.0), axis=-1, keepdims=True)


def _kth_largest_key(key, k):
    base = jnp.where(_count_ge(key, 0) >= k, 0, INT_MIN).astype(I32)

    def body(it, base):
        cand = base | jnp.left_shift(jnp.int32(1), 30 - it)
        return jnp.where(_count_ge(key, cand) >= k, cand, base)

    return lax.fori_loop(0, 31, body, base)


def _topk_mask(key, k):
    n = key.shape[1]
    thr = _kth_largest_key(key, k)
    gt = key > thr
    eq = key == thr
    need = k - jnp.sum(jnp.where(gt, 1.0, 0.0), axis=-1, keepdims=True)
    before = _onehot(_iota((LANES, LANES), 0) < _iota((LANES, LANES), 1))
    run = jnp.zeros_like(need)
    out = []
    for c in range(n // LANES):
        sl = slice(c * LANES, (c + 1) * LANES)
        eqf = jnp.where(eq[:, sl], 1.0, 0.0)
        prior = _dotf(eqf.astype(BF16), before) + run
        out.append(gt[:, sl] | (eq[:, sl] & (prior < need)))
        run = run + jnp.sum(eqf, axis=-1, keepdims=True)
    return jnp.concatenate(out, axis=1)


def _flash_streams(streams, c_lo, c_hi, tk, passes, c_free=None):
    ones_rows = jnp.ones((8, tk), F32)

    def step(c, carry, masked):
        off = pl.multiple_of(c * tk, tk)
        out = []
        for (q, kt_ref, vt_ref, krow, vrow, mask_fn, bias_fn), (m, acc) in zip(streams, carry):
            kt = kt_ref[0, krow:krow + HEAD_DIM, pl.ds(off, tk)]
            vt = jnp.concatenate([vt_ref[0, vrow:vrow + HEAD_DIM, pl.ds(off, tk)], ones_rows], axis=0)
            s = _dotp(q, kt, NN, passes)
            if bias_fn is not None:
                s = s + bias_fn(c)
            if masked:
                s = mask_fn(c, s)
            m_new = jnp.maximum(m, jnp.max(s, axis=-1, keepdims=True))
            e = jnp.exp(s - m_new)
            out.append((m_new, jnp.exp(m - m_new) * acc + _dotp(e, vt, NT, passes)))
        return tuple(out)

    carry = tuple((jnp.full((s[0].shape[0], 1), 0.1 * NEG_INF, F32), jnp.zeros((s[0].shape[0], HEAD_DIM + 8), F32))
                  for s in streams)
    if c_free is not None:
        carry = lax.fori_loop(c_lo, c_free, functools.partial(step, masked=False), carry)
        c_lo = c_free
    carry = lax.fori_loop(c_lo, c_hi, functools.partial(step, masked=True), carry)
    res = []
    for m, acc in carry:
        den = acc[:, HEAD_DIM:HEAD_DIM + 1]
        res.append((acc[:, 0:HEAD_DIM] / jnp.maximum(den, TINY), m, den))
    return res


def _flash(q, kt_ref, vt_ref, krow, vrow, c_lo, c_hi, tk, mask_fn, passes, bias_fn=None, c_free=None):
    return _flash_streams([(q, kt_ref, vt_ref, krow, vrow, mask_fn, bias_fn)], c_lo, c_hi, tk, passes, c_free)[0]


def _stack_heads(ref, first_head, n):
    return jnp.concatenate(
        [ref[0, :, (first_head + r) * HEAD_DIM:(first_head + r + 1) * HEAD_DIM] for r in range(n)], axis=0)


def _group_block_diag(qt, n_heads, n_groups):
    per = n_heads // n_groups
    shift = per.bit_length() - 1
    keep = (_iota(qt.shape, 0) >> shift) == (_iota(qt.shape, 1) >> 6)
    return jnp.where(keep, qt, 0.0), keep


def _fold_groups(x, keep, n_groups):
    x = jnp.where(keep, x, 0.0)
    return functools.reduce(lambda a, b: a + b, [x[:, g * HEAD_DIM:(g + 1) * HEAD_DIM] for g in range(n_groups)])


def _mod_kernel(c_ref, w_ref, b_ref, o_ref):
    c = c_ref[...]
    o_ref[0] = _dot3(c * _sigmoid(c), w_ref[0]) + b_ref[0]


def _mod_all(c_all, ada_w, ada_b):
    L, D, D3 = ada_w.shape
    NC = c_all.shape[0]
    tn = 1024
    return pl.pallas_call(
        _mod_kernel, grid=(L, D3 // tn),
        in_specs=[pl.BlockSpec((NC, D), lambda l, j: (0, 0)),
                  pl.BlockSpec((1, D, tn), lambda l, j: (l, 0, j)),
                  pl.BlockSpec((1, 1, tn), lambda l, j: (l, 0, j))],
        out_specs=pl.BlockSpec((1, NC, tn), lambda l, j: (l, 0, j)),
        out_shape=jax.ShapeDtypeStruct((L, NC, D3), F32),
        compiler_params=_cp("parallel", "parallel"), name="adaln_mod",
    )(c_all, ada_w, ada_b.reshape(L, 1, D3))


def _proj_kernel(*refs, rsegs, tsegs, passes):
    x_ref, sc_ref, sh_ref, g_ref = refs[0:4]
    pos = 4
    if rsegs:
        wh_ref, wl_ref, cos_ref, sn_ref, sp_ref = refs[pos:pos + 5]
        pos += 5
    if tsegs:
        wth_ref, wtl_ref, cost_ref, sint_ref = refs[pos:pos + 4]
        pos += 4
    out_refs = refs[pos:]
    x = x_ref[...]
    y = x * lax.rsqrt(jnp.mean(x * x, axis=-1, keepdims=True) + NORM_EPS) * g_ref[...]
    h = y * (1.0 + sc_ref[0]) + sh_ref[0]
    hh, hl = _split(h, 2)
    oi = 0
    for start, width, modes in rsegs:
        wh = wh_ref[:, start:start + width]
        u = _dotf(hh, wh)
        if passes == 3:
            u = u + (_dotf(hh, wl_ref[:, start:start + width]) + _dotf(hl, wh))
        for mode in modes:
            o_ref = out_refs[oi]
            oi += 1
            n_rope = {"n": 0, "r": width, "rk": width // 2}[mode]
            for c in range(width // LANES):
                sl = slice(c * LANES, (c + 1) * LANES)
                uc = u[:, sl]
                if c * LANES < n_rope:
                    uc = (uc * cos_ref[...] + pltpu.roll(uc, LANES - ROPE_HALF, 1) * sn_ref[...]
                          + pltpu.roll(uc, ROPE_HALF, 1) * sp_ref[...])
                o_ref[:, sl] = uc
    for start, width, mode in tsegs:
        wth = wth_ref[start:start + width, :]
        ut = _dotf(wth, hh, NT)
        if passes == 3:
            ut = ut + (_dotf(wth, hl, NT) + _dotf(wtl_ref[start:start + width, :], hh, NT))
        o_ref = out_refs[oi]
        oi += 1
        o_ref[0] = ut
        n_rope = {"t": 0, "tr": width, "trk": width // 2}[mode]
        for hd in range(n_rope // HEAD_DIM):
            r0 = hd * HEAD_DIM
            x1 = ut[r0:r0 + ROPE_HALF]
            x2 = ut[r0 + ROPE_HALF:r0 + ROPE_DIMS]
            o_ref[0, r0:r0 + ROPE_HALF, :] = x1 * cost_ref[...] - x2 * sint_ref[...]
            o_ref[0, r0 + ROPE_HALF:r0 + ROPE_DIMS, :] = x2 * cost_ref[...] + x1 * sint_ref[...]


def _proj(x2, scale, shift, g, w, segs, tables, ttables, tiles_per_batch, tm, passes):
    R, D = x2.shape
    NB, RB = scale.shape[0], scale.shape[1]
    Tt = R // NB
    groups, cur, cur_w = [], [], 0
    for seg in segs:
        wd = -(-seg[2] // LANES) * LANES
        if cur and cur_w + wd > PROJ_COLS_PER_CALL:
            groups.append(cur)
            cur, cur_w = [], 0
        cur.append(seg)
        cur_w += wd
    groups.append(cur)
    results = {}
    for grp in groups:
        rsegs, tsegs, rcols, tcols = [], [], [], []
        out_shapes, out_specs, keys = [], [], []
        rpos = tpos = 0
        for si, (kind, c0, wd, spec) in enumerate(grp):
            if kind == "rm":
                pw = -(-wd // LANES) * LANES
                rcols.append(w[:, c0:c0 + wd])
                if pw > wd:
                    rcols.append(jnp.zeros((D, pw - wd), F32))
                rsegs.append((rpos, pw, spec))
                rpos += pw
                for mi in range(len(spec)):
                    out_shapes.append(jax.ShapeDtypeStruct((R, pw), F32))
                    out_specs.append(pl.BlockSpec((tm, pw), lambda i: (i, 0)))
                    keys.append((id(grp), si, mi))
        for si, (kind, c0, wd, spec) in enumerate(grp):
            if kind == "t":
                tcols.append(w[:, c0:c0 + wd])
                tsegs.append((tpos, wd, spec))
                tpos += wd
                out_shapes.append(jax.ShapeDtypeStruct((NB, wd, Tt), F32))
                out_specs.append(pl.BlockSpec((1, wd, tm), lambda i: (i // tiles_per_batch, 0, i % tiles_per_batch)))
                keys.append((id(grp), si, 0))
        mod_spec = pl.BlockSpec((1, RB, D), lambda i: (i // tiles_per_batch, 0, 0))
        in_specs = [pl.BlockSpec((tm, D), lambda i: (i, 0)), mod_spec, mod_spec, pl.BlockSpec((1, D), lambda i: (0, 0))]
        args = [x2, scale, shift, g.reshape(1, D)]
        if rsegs:
            wr = jnp.concatenate(rcols, axis=1)
            wh = wr.astype(BF16)
            wl = (wr - wh.astype(F32)).astype(BF16)
            tab_spec = pl.BlockSpec((tm, LANES), lambda i: (i % tiles_per_batch, 0))
            in_specs += [pl.BlockSpec((D, rpos), lambda i: (0, 0))] * 2 + [tab_spec] * 3
            args += [wh, wl, *tables]
        if tsegs:
            wt = jnp.concatenate(tcols, axis=1).T
            wth = wt.astype(BF16)
            wtl = (wt - wth.astype(F32)).astype(BF16)
            ttab_spec = pl.BlockSpec((ROPE_HALF, tm), lambda i: (0, i % tiles_per_batch))
            in_specs += [pl.BlockSpec((tpos, D), lambda i: (0, 0))] * 2 + [ttab_spec] * 2
            args += [wth, wtl, *ttables]
        res = pl.pallas_call(
            functools.partial(_proj_kernel, rsegs=tuple(rsegs), tsegs=tuple(tsegs), passes=passes), grid=(R // tm,),
            in_specs=in_specs, out_specs=out_specs, out_shape=out_shapes,
            compiler_params=_cp("parallel"), name="norm_mod_proj",
        )(*args)
        for key, r in zip(keys, res):
            results[key] = r
    outs = []
    for grp in groups:
        for si, (kind, c0, wd, spec) in enumerate(grp):
            for mi in range(len(spec) if kind == "rm" else 1):
                outs.append(results[(id(grp), si, mi)])
    return outs


def _rope_angles(pos):
    inv = ROPE_THETA ** (-(jnp.arange(ROPE_HALF, dtype=F32) / ROPE_HALF))
    ang = pos.astype(F32)[:, None] * inv[None, :]
    return jnp.cos(ang), jnp.sin(ang)


def _rope_tables(pos):
    cos, sin = _rope_angles(pos)
    R = pos.shape[0]
    one = jnp.ones((R, HEAD_DIM - ROPE_DIMS), F32)
    zero = jnp.zeros((R, HEAD_DIM - ROPE_DIMS), F32)
    zh = jnp.zeros((R, ROPE_HALF), F32)
    cos_h = jnp.concatenate([cos, cos, one], axis=1)
    sn_h = jnp.concatenate([-sin, zh, zero], axis=1)
    sp_h = jnp.concatenate([zh, sin, zero], axis=1)
    rep = LANES // HEAD_DIM
    return tuple(jnp.tile(t, (1, rep)) for t in (cos_h, sn_h, sp_h))


def _rope_tables_t(pos):
    cos, sin = _rope_angles(pos)
    return cos.T, sin.T


def _out_kernel(o_ref, z_ref, wh_ref, wl_ref, x_ref, gate_ref, *rest, final, passes):
    z = z_ref[...]
    y = o_ref[...] * (z * _sigmoid(z))
    yh, yl = _split(y, 2)
    wh = wh_ref[...]
    r = _dotf(yh, wh)
    if passes == 3:
        r = r + (_dotf(yh, wl_ref[...]) + _dotf(yl, wh))
    xn = x_ref[...] + gate_ref[0] * r
    if final:
        fg_ref, xo_ref, yo_ref = rest
        yo_ref[...] = xn * lax.rsqrt(jnp.mean(xn * xn, axis=-1, keepdims=True) + NORM_EPS) * fg_ref[...]
    else:
        (xo_ref,) = rest
    xo_ref[...] = xn


def _gated_out(o2, z2, w_out, x2, gate, tiles_per_batch, tm, passes, final_g=None):
    R, W = o2.shape
    D = x2.shape[1]
    RB = gate.shape[1]
    wh = w_out.astype(BF16)
    wl = (w_out - wh.astype(F32)).astype(BF16)
    final = final_g is not None
    in_specs = [pl.BlockSpec((tm, W), lambda i: (i, 0)), pl.BlockSpec((tm, W), lambda i: (i, 0)),
                pl.BlockSpec((W, D), lambda i: (0, 0)), pl.BlockSpec((W, D), lambda i: (0, 0)),
                pl.BlockSpec((tm, D), lambda i: (i, 0)),
                pl.BlockSpec((1, RB, D), lambda i: (i // tiles_per_batch, 0, 0))]
    args = [o2, z2, wh, wl, x2, gate]
    out_shape = [jax.ShapeDtypeStruct((R, D), F32)]
    out_specs = [pl.BlockSpec((tm, D), lambda i: (i, 0))]
    if final:
        in_specs.append(pl.BlockSpec((1, D), lambda i: (0, 0)))
        args.append(final_g.reshape(1, D))
        out_shape.append(jax.ShapeDtypeStruct((R, D), F32))
        out_specs.append(pl.BlockSpec((tm, D), lambda i: (i, 0)))
    res = pl.pallas_call(
        functools.partial(_out_kernel, final=final, passes=passes), grid=(R // tm,),
        in_specs=in_specs, out_specs=out_specs, out_shape=out_shape,
        compiler_params=_cp("parallel"), name="gated_out_proj",
    )(*args)
    return res if final else (res[0], None)


def _cmp_taps(wk, wv, groups):
    def table(w):
        t = jnp.tile(w.T, (1, LANES // A_CMP_STRIDE))
        return jnp.tile(t, (groups, 1))
    S = A_CMP_STRIDE
    lo = jnp.concatenate([table(wk[:S]), table(wv[:S])], axis=0)
    hi = jnp.concatenate([table(wk[S:]), table(wv[S:])], axis=0)
    return lo, hi


def _chunk_sum_matrix(first_col):
    shift = A_CMP_STRIDE.bit_length() - 1
    return _onehot((_iota((LANES, LANES), 0) >> shift) + first_col == _iota((LANES, LANES), 1))


def _cmp_kernel(x_ref, wlo_ref, whi_ref, o_ref, *, T):
    rows = x_ref.shape[1]
    per = LANES // A_CMP_STRIDE
    lo = jnp.zeros((rows, LANES), F32)
    hi = jnp.zeros((rows, LANES), F32)
    for c in range(T // LANES):
        xc = x_ref[0, :, c * LANES:(c + 1) * LANES]
        m = _chunk_sum_matrix(c * per)
        lo = lo + _dotx(xc * wlo_ref[...], m, 3)
        hi = hi + _dotx(xc * whi_ref[...], m, 3)
    o_ref[0] = lo + pltpu.roll(hi, LANES - 1, 1)


def _nsa_compress_prompt(cmp_t, wlo, whi):
    B, rows, T = cmp_t.shape
    assert T // A_CMP_STRIDE <= LANES
    return pl.pallas_call(
        functools.partial(_cmp_kernel, T=T), grid=(B,),
        in_specs=[pl.BlockSpec((1, rows, T), lambda b: (b, 0, 0)), pl.BlockSpec((rows, LANES), lambda b: (0, 0)),
                  pl.BlockSpec((rows, LANES), lambda b: (0, 0))],
        out_specs=pl.BlockSpec((1, rows, LANES), lambda b: (b, 0, 0)),
        out_shape=jax.ShapeDtypeStruct((B, rows, LANES), F32),
        compiler_params=_cp("parallel"), name="nsa_compress",
    )(cmp_t, wlo, whi)


def _nsa_prompt_kernel(q_ref, qr_ref, kvb_ref, sel_ref, win_ref, gl_ref, o_ref, *, tq, tk, T, flash_passes):
    i = pl.program_id(1)
    t0 = i * tq
    R = 4
    nb = kvb_ref.shape[2]
    ncb = T // A_CMP_STRIDE - 1
    nsb = T // A_SEL_LEN
    gw = A_KV * HEAD_DIM
    gates = _sigmoid(gl_ref[0])

    n_io = _iota((R * tq, nb), 1)
    t_c = (_iota((R * tq, nb), 0) & (tq - 1)) + t0
    cmask = (n_io * A_CMP_STRIDE + (A_CMP_LEN - 1) <= t_c) & (n_io < ncb)

    ratio = A_SEL_LEN // A_CMP_STRIDE
    back = A_CMP_LEN // A_CMP_STRIDE - 1
    mn, mj = _iota((nb, LANES), 0), _iota((nb, LANES), 1)
    imp_mat = _onehot((mn >= mj * ratio - back) & (mn <= mj * ratio + ratio - 1) & (mn < ncb) & (mj < nsb))

    j_io = _iota((tq, LANES), 1)
    qb = (_iota((tq, LANES), 0) + t0) >> 6
    allowed = (j_io <= qb) & (j_io < nsb)
    forced = (j_io == 0) | (j_io == qb)
    nsr = -(-nsb // 8) * 8
    jt_io = _iota((nsr, tq), 0)
    allowed_t = (jt_io <= ((_iota((nsr, tq), 1) + t0) >> 6)) & (jt_io < nsb)

    dist = (_iota((R * tq, tk), 0) & (tq - 1)) + t0 - _iota((R * tq, tk), 1)
    c_hi = (t0 + tq + tk - 1) // tk
    w_lo = jnp.maximum(t0 - (A_WINDOW - 1), 0) // tk

    def win_mask(c, s):
        return jnp.where(((dist - c * tk) & ~(A_WINDOW - 1)) == 0, s, NEG_INF)

    def group_front(g):
        q4 = _stack_heads(q_ref, R * g, R) * QK_SCALE
        kb = kvb_ref[0, g * HEAD_DIM:(g + 1) * HEAD_DIM, :]
        vb = kvb_ref[0, gw + g * HEAD_DIM:gw + (g + 1) * HEAD_DIM, :]
        s = jnp.where(cmask, _dot3(q4, kb, NN), NEG_INF)
        m = jnp.max(s, axis=-1, keepdims=True)
        e = jnp.where(cmask, jnp.exp(s - m), 0.0)
        p = e / jnp.maximum(jnp.sum(e, axis=-1, keepdims=True), TINY)
        o_cmp = _dot3(p, vb, NT)
        pg = p[0:tq] + p[tq:2 * tq] + p[2 * tq:3 * tq] + p[3 * tq:4 * tq]
        imp = _dotx(pg, imp_mat, 3)
        score = jnp.where(allowed, jnp.where(forced, A_FORCED, imp), -1.0)
        score = jnp.where(j_io < nsb, score, -2.0)
        score_t = score.T[0:nsr]
        rank = jnp.zeros((nsr, tq), F32)
        for j2 in range(nsb):
            other = score_t[j2:j2 + 1, :]
            rank = rank + jnp.where((other > score_t) | ((other == score_t) & (j2 < jt_io)), 1.0, 0.0)
        sel_t = jnp.where((rank < min(A_SEL_TOPK, nsb)) & allowed_t, 1.0, 0.0)
        sel = jnp.concatenate([sel_t, jnp.zeros((LANES - nsr, tq), F32)], axis=0).T.astype(BF16)
        sel4 = jnp.concatenate([sel] * R, axis=0)

        def sel_mask(c, s):
            blk = (_iota((LANES, tk), 1) + c * tk) >> 6
            hit = _dotf(sel4, _onehot(blk == _iota((LANES, tk), 0)))
            return jnp.where((hit > 0.5) & (dist >= c * tk), s, NEG_INF)

        return o_cmp, sel_mask

    for g0 in range(0, A_KV, 2):
        pair = (g0, g0 + 1)
        fronts = [group_front(g) for g in pair]
        q4rs = [_stack_heads(qr_ref, R * g, R) * QK_SCALE for g in pair]
        sel_res = _flash_streams(
            [(q4r, sel_ref, sel_ref, g * HEAD_DIM, gw + g * HEAD_DIM, front[1], None) for g, q4r, front in zip(pair, q4rs, fronts)],
            0, c_hi, tk, flash_passes)
        win_res = _flash_streams(
            [(q4r, win_ref, win_ref, g * HEAD_DIM, gw + g * HEAD_DIM, win_mask, None) for g, q4r in zip(pair, q4rs)],
            w_lo, c_hi, tk, flash_passes)
        for g, (o_cmp, _), (o_sel, _, _), (o_win, _, _) in zip(pair, fronts, sel_res, win_res):
            outs = []
            for r in range(R):
                h = R * g + r
                rows = slice(r * tq, (r + 1) * tq)
                outs.append(gates[:, 3 * h:3 * h + 1] * o_cmp[rows] + gates[:, 3 * h + 1:3 * h + 2] * o_sel[rows]
                            + gates[:, 3 * h + 2:3 * h + 3] * o_win[rows])
            o_ref[0, :, g * R * HEAD_DIM:(g + 1) * R * HEAD_DIM] = jnp.concatenate(outs, axis=1)


def _nsa_prompt(q3, qr3, kvb, sel_t, win_t, gl3, flash_passes, tq=128, tk=512):
    B, T, D = q3.shape
    full = lambda a: pl.BlockSpec((1,) + a.shape[1:], lambda b, i: (b, 0, 0))
    return pl.pallas_call(
        functools.partial(_nsa_prompt_kernel, tq=tq, tk=tk, T=T, flash_passes=flash_passes), grid=(B, T // tq),
        in_specs=[pl.BlockSpec((1, tq, D), lambda b, i: (b, i, 0)), pl.BlockSpec((1, tq, D), lambda b, i: (b, i, 0)),
                  full(kvb), full(sel_t), full(win_t), pl.BlockSpec((1, tq, LANES), lambda b, i: (b, i, 0))],
        out_specs=pl.BlockSpec((1, tq, D), lambda b, i: (b, i, 0)),
        out_shape=jax.ShapeDtypeStruct((B, T, D), F32),
        compiler_params=_cp("parallel", "arbitrary"), name="nsa_prompt",
    )(q3, qr3, kvb, sel_t, win_t, gl3)


def _dsa_prompt_kernel(q_ref, qi_ref, wi_ref, kv_ref, ki_ref, o_ref, ch_ref, *, tq, tk, T, topk, passes, idx_passes):
    i = pl.program_id(1)
    t0 = i * tq
    R = 4
    gw = kv_ref.shape[1] // 2
    c_hi = (t0 + tq + tk - 1) // tk
    assert tk >= topk

    def select_keys(width):
        wi = wi_ref[0] * (B_IDX_HEADS * B_IDX_DIM) ** -0.5
        ki = ki_ref[0, :, 0:width]
        score = jnp.zeros((tq, width), F32)
        for h in range(B_IDX_HEADS):
            logits = _dotp(qi_ref[0, :, h * B_IDX_DIM:(h + 1) * B_IDX_DIM], ki, NN, idx_passes)
            score = score + wi[:, h:h + 1] * jnp.maximum(logits, 0.0)
        causal = _iota((tq, width), 1) <= _iota((tq, width), 0) + t0
        score = jnp.where(causal, score, NEG_INF)
        ch_ref[:, 0:width] = jnp.where(_topk_mask(_sortable_key(score), topk) & causal, 0.0, NEG_INF)

    for nc in range(1, T // tk + 1):
        pl.when(c_hi == nc)(functools.partial(select_keys, nc * tk))

    def mask_fn(c, s):
        picked = ch_ref[:, pl.ds(pl.multiple_of(c * tk, tk), tk)]
        return s + jnp.concatenate([picked] * R, axis=0)

    for g0 in range(0, gw // HEAD_DIM, 2):
        streams = [(_stack_heads(q_ref, R * g, R) * QK_SCALE, kv_ref, kv_ref, g * HEAD_DIM, gw + g * HEAD_DIM, mask_fn, None)
                   for g in (g0, g0 + 1)]
        for g, (o, _, _) in zip((g0, g0 + 1), _flash_streams(streams, 0, c_hi, tk, passes)):
            o_ref[0, :, g * R * HEAD_DIM:(g + 1) * R * HEAD_DIM] = jnp.concatenate(
                [o[r * tq:(r + 1) * tq] for r in range(R)], axis=1)


def _dsa_prompt(q3, qi3, wi3, kv_t, ki_t, topk, passes, idx_passes, tq=128, tk=512):
    B, T, D = q3.shape
    full = lambda a: pl.BlockSpec((1,) + a.shape[1:], lambda b, i: (b, 0, 0))
    blk = lambda w: pl.BlockSpec((1, tq, w), lambda b, i: (b, i, 0))
    return pl.pallas_call(
        functools.partial(_dsa_prompt_kernel, tq=tq, tk=tk, T=T, topk=topk, passes=passes, idx_passes=idx_passes),
        grid=(B, T // tq),
        in_specs=[blk(D), blk(qi3.shape[2]), blk(LANES), full(kv_t), full(ki_t)],
        out_specs=blk(D), out_shape=jax.ShapeDtypeStruct((B, T, D), F32),
        scratch_shapes=[pltpu.VMEM((tq, T), F32)],
        compiler_params=_cp("parallel", "arbitrary"), name="dsa_prompt",
    )(q3, qi3, wi3, kv_t, ki_t)


def _dil_prompt_kernel(*refs, tq, tk, T, passes):
    ng = len(C_GROUPS)
    q_refs, k_refs, v_refs, o_ref = refs[0:ng], refs[ng:2 * ng], refs[2 * ng:3 * ng], refs[3 * ng]
    i = pl.program_id(2)
    t0 = i * tq
    dist = _iota((tq, tk), 0) + t0 - _iota((tq, tk), 1)
    c_hi = (t0 + tq + tk - 1) // tk
    nhl = LANES // HEAD_DIM
    per_group = []
    for (w, r), q_ref, k_ref, v_ref in zip(C_GROUPS, q_refs, k_refs, v_refs):
        def mask_fn(c, s, w=w, r=r):
            d = dist - c * tk
            ok = (d & (INT_MIN | (r - 1))) == 0
            if w < T - 1:
                ok = ok & (d <= w)
            return jnp.where(ok, s, NEG_INF)

        c_lo = jnp.maximum(t0 - w, 0) // tk
        streams = [(q_ref[0, :, hh * HEAD_DIM:(hh + 1) * HEAD_DIM] * QK_SCALE, k_ref, v_ref, hh * HEAD_DIM, hh * HEAD_DIM,
                    mask_fn, None) for hh in range(nhl)]
        per_group.append(_flash_streams(streams, c_lo, c_hi, tk, passes))
    outs = []
    for hh in range(nhl):
        res = [grp[hh] for grp in per_group]
        m_all = functools.reduce(jnp.maximum, [m for _, m, _ in res])
        wts = [den * jnp.exp(m - m_all) for _, m, den in res]
        tot = functools.reduce(lambda a, b: a + b, wts)
        outs.append(functools.reduce(lambda a, b: a + b, [(wt / tot) * o for wt, (o, _, _) in zip(wts, res)]))
    o_ref[0] = jnp.concatenate(outs, axis=1)


def _dil_prompt(qs, kvs_t, passes, tq=256, tk=512):
    B, T, W = qs[0].shape
    hp = W // LANES
    qspec = pl.BlockSpec((1, tq, LANES), lambda b, h, i: (b, i, h))
    kspec = pl.BlockSpec((1, LANES, T), lambda b, h, i: (b, h, 0))
    vspec = pl.BlockSpec((1, LANES, T), lambda b, h, i: (b, hp + h, 0))
    ng = len(C_GROUPS)
    return pl.pallas_call(
        functools.partial(_dil_prompt_kernel, tq=tq, tk=tk, T=T, passes=passes), grid=(B, hp, T // tq),
        in_specs=[qspec] * ng + [kspec] * ng + [vspec] * ng,
        out_specs=qspec, out_shape=jax.ShapeDtypeStruct((B, T, W), F32),
        compiler_params=_cp("parallel", "parallel", "arbitrary"), name="dilated_prompt",
    )(*qs, *kvs_t, *kvs_t)


def _logf_kernel(f_ref, b_ref, lf_ref, c_ref, *, T, tc):
    x = f_ref[0] + b_ref[...]
    lf = jnp.minimum(x, 0.0) - jnp.log(1.0 + jnp.exp(-jnp.abs(x)))
    lf_ref[0] = lf
    parts = _split(lf, 3)
    for c in range(T // tc):
        upto = _onehot(_iota((T, tc), 0) <= _iota((T, tc), 1) + c * tc)
        c_ref[0, :, c * tc:(c + 1) * tc] = functools.reduce(lambda a, b: a + b, [_dotf(p, upto) for p in parts])


def _fox_logf(f_t, bias_col):
    B, H, T = f_t.shape
    spec = pl.BlockSpec((1, H, T), lambda b: (b, 0, 0))
    return pl.pallas_call(
        functools.partial(_logf_kernel, T=T, tc=min(256, T)), grid=(B,),
        in_specs=[spec, pl.BlockSpec((H, 1), lambda b: (0, 0))],
        out_specs=[spec, spec], out_shape=[jax.ShapeDtypeStruct((B, H, T), F32)] * 2,
        compiler_params=_cp("parallel"), name="fox_logf_cumsum",
    )(f_t, bias_col)


def _fox_prompt_kernel(q_ref, k_ref, v_ref, cc_ref, cr_ref, o_ref, *, tq, tk, T, passes):
    hp = pl.program_id(1)
    i = pl.program_id(2)
    t0 = i * tq
    dist = _iota((tq, tk), 0) + t0 - _iota((tq, tk), 1)
    c_hi = (t0 + tq + tk - 1) // tk
    c_free = (t0 + 1) // tk
    nh = cc_ref.shape[2]

    def mask_fn(c, s):
        return jnp.where(dist >= c * tk, s, NEG_INF)

    streams = []
    for hh in range(LANES // HEAD_DIM):
        h = hp * (LANES // HEAD_DIM) + hh
        lane = hh * HEAD_DIM
        c_col = jnp.sum(jnp.where(_iota((tq, nh), 1) == h, cc_ref[0], 0.0), axis=-1, keepdims=True)

        def bias_fn(c, h=h, c_col=c_col):
            off = pl.multiple_of(c * tk, tk)
            return c_col - cr_ref[0, pl.ds(h, 1), pl.ds(off, tk)]

        streams.append((q_ref[0, :, lane:lane + HEAD_DIM] * QK_SCALE, k_ref, v_ref, lane, lane, mask_fn, bias_fn))
    res = _flash_streams(streams, 0, c_hi, tk, passes, c_free)
    o_ref[0] = jnp.concatenate([o for o, _, _ in res], axis=1)


def _fox_prompt(q3, kv_t, c_col, c_row, passes, tq=256, tk=512):
    B, T, W = q3.shape
    hp = W // LANES
    nh = c_row.shape[1]
    return pl.pallas_call(
        functools.partial(_fox_prompt_kernel, tq=tq, tk=tk, T=T, passes=passes), grid=(B, hp, T // tq),
        in_specs=[pl.BlockSpec((1, tq, LANES), lambda b, h, i: (b, i, h)),
                  pl.BlockSpec((1, LANES, T), lambda b, h, i: (b, h, 0)),
                  pl.BlockSpec((1, LANES, T), lambda b, h, i: (b, hp + h, 0)),
                  pl.BlockSpec((1, tq, nh), lambda b, h, i: (b, i, 0)),
                  pl.BlockSpec((1, nh, T), lambda b, h, i: (b, 0, 0))],
        out_specs=pl.BlockSpec((1, tq, LANES), lambda b, h, i: (b, i, h)),
        out_shape=jax.ShapeDtypeStruct((B, T, W), F32),
        compiler_params=_cp("parallel", "parallel", "arbitrary"), name="fox_prompt",
    )(q3, kv_t, kv_t, c_col, c_row)


def _page_specs(n, block, layer, index_fn, lead=()):
    specs = []
    for pi in range(n):
        def imap(b, s, *pf, pi=pi):
            return (layer, index_fn(b, s, pi, *pf)) + lead + (0,) * (len(block) - 2 - len(lead))
        specs.append(pl.BlockSpec(block, imap))
    return specs


def _tdec_init(m_ref, l_ref, acc_ref):
    m_ref[...] = jnp.full(m_ref.shape, NEG_INF, F32)
    l_ref[...] = jnp.zeros(l_ref.shape, F32)
    acc_ref[...] = jnp.zeros(acc_ref.shape, F32)


def _tdec_update(kt, vt, qexp, bias, valid, m_ref, l_ref, acc_ref, r):
    G = kt.shape[0] // HEAD_DIM
    s = jnp.sum((kt * qexp).reshape(G, HEAD_DIM, LANES), axis=1)
    if bias is not None:
        s = s + bias
    if valid is not None:
        s = jnp.where(valid, s, NEG_INF)
    m_old = m_ref[r]
    m_new = jnp.maximum(m_old, jnp.max(s, axis=-1, keepdims=True))
    alpha = jnp.exp(m_old - m_new)
    e = jnp.exp(s - m_new)
    if valid is not None:
        e = jnp.where(valid, e, 0.0)
    l_ref[r] = alpha * l_ref[r] + e
    acc_ref[r] = acc_ref[r] * alpha[:, :, None] + vt.reshape(G, HEAD_DIM, LANES) * e[:, None, :]
    m_ref[r] = m_new


def _tdec_finish(l_ref, acc_ref, r):
    den = jnp.maximum(jnp.sum(l_ref[r], axis=-1, keepdims=True), TINY)
    o = jnp.sum(acc_ref[r], axis=-1, keepdims=True) / den[:, :, None]
    return jnp.broadcast_to(o, acc_ref.shape[1:])


def _tdec_scratch(R, G):
    return [pltpu.VMEM((R, G, 1), F32), pltpu.VMEM((R, G, LANES), F32), pltpu.VMEM((R, G, HEAD_DIM, LANES), F32)]


def _scmp_kernel(pt_ref, *refs, pp):
    pages, wlo_ref, whi_ref, o_ref = refs[:pp], refs[pp], refs[pp + 1], refs[pp + 2]
    rows = wlo_ref.shape[0]
    per = PAGE // A_CMP_STRIDE
    lo = jnp.zeros((rows, LANES), F32)
    hi = jnp.zeros((rows, LANES), F32)
    for pi in range(pp):
        x = pages[pi][...]
        m = _chunk_sum_matrix(pi * per)
        lo = lo + _dotx(x * wlo_ref[...], m, 2)
        hi = hi + _dotx(x * whi_ref[...], m, 2)
    o_ref[0, 0:rows, :] = lo
    o_ref[0, rows:2 * rows, :] = hi


def _nsa_compress_sample(cache_t, layer, page_table, wlo, whi):
    Bd, NP = page_table.shape
    rows = cache_t.shape[2]
    pp = LANES * A_CMP_STRIDE // PAGE
    grid_spec = pltpu.PrefetchScalarGridSpec(
        num_scalar_prefetch=1, grid=(Bd, NP // pp),
        in_specs=_page_specs(pp, (None, None, rows, PAGE), layer, lambda b, s, pi, pt: pt[b, s * pp + pi])
        + [pl.BlockSpec((rows, LANES), lambda b, s, pt: (0, 0))] * 2,
        out_specs=pl.BlockSpec((1, 2 * rows, LANES), lambda b, s, pt: (b, 0, s)))
    return pl.pallas_call(
        functools.partial(_scmp_kernel, pp=pp), grid_spec=grid_spec,
        out_shape=jax.ShapeDtypeStruct((Bd, 2 * rows, NP * PAGE // A_CMP_STRIDE), F32),
        compiler_params=_cp("parallel", "arbitrary"), name="nsa_compress_paged",
    )(page_table, *([cache_t] * pp), wlo, whi)


def _nsa_sample_cmp_kernel(lohi_ref, qt_ref, o_ref, idx_ref, *, P, ncols):
    rows = lohi_ref.shape[1] // 2
    gw = rows // 2
    nch = lohi_ref.shape[2]
    H = qt_ref.shape[1]
    R = H // A_KV
    lpad = -(-(P + 1) // A_SEL_LEN) * A_SEL_LEN
    ncb = lpad // A_CMP_STRIDE - 1
    nsb = lpad // A_SEL_LEN
    qb = P // A_SEL_LEN
    lohi = lohi_ref[0]
    kvb = lohi[0:rows] + pltpu.roll(lohi[rows:2 * rows], nch - 1, 1)
    qbd, keep = _group_block_diag(qt_ref[0] * QK_SCALE, H, A_KV)
    n_io = _iota((H, nch), 1)
    valid = (n_io * A_CMP_STRIDE + (A_CMP_LEN - 1) <= P) & (n_io < ncb)
    s = jnp.where(valid, _dot3(qbd, kvb[0:gw], NN), NEG_INF)
    m = jnp.max(s, axis=-1, keepdims=True)
    e = jnp.where(valid, jnp.exp(s - m), 0.0)
    p = e / jnp.maximum(jnp.sum(e, axis=-1, keepdims=True), TINY)
    o_ref[0] = _fold_groups(_dot3(p, kvb[gw:rows], NT), keep, A_KV)
    ratio = A_SEL_LEN // A_CMP_STRIDE
    back = A_CMP_LEN // A_CMP_STRIDE - 1
    mn, mj = _iota((nch, ncols), 0), _iota((nch, ncols), 1)
    imp_mat = _onehot((mn >= mj * ratio - back) & (mn <= mj * ratio + ratio - 1) & (mn < ncb))
    j_io = _iota((8, ncols), 1)
    jf = j_io.astype(F32)
    lane = _iota((8, LANES), 1)
    row = _iota((8, LANES), 0)
    out = jnp.zeros((8, LANES), F32)
    for g in range(A_KV):
        pg = jnp.sum(p[R * g:R * (g + 1)], axis=0, keepdims=True)
        imp = _dotx(jnp.broadcast_to(pg, (8, nch)), imp_mat, 3)
        score = jnp.where(j_io <= qb, jnp.where((j_io == 0) | (j_io == qb), A_FORCED, imp), -1.0)
        score = jnp.where(j_io < nsb, score, -2.0)
        for k in range(min(A_SEL_TOPK, nsb)):
            best = jnp.max(score, axis=-1, keepdims=True)
            pick = jnp.min(jnp.where(score == best, jf, 1e9), axis=-1, keepdims=True)
            out = jnp.where((lane == k) & (row == g), pick, out)
            score = jnp.where(jf == pick, -3.0, score)
    idx_ref[0] = out.astype(I32)


def _nsa_sample_cmp(lohi, qt3, P):
    Bd, rows2, nch = lohi.shape
    H, gw = qt3.shape[1], qt3.shape[2]
    nsb = -(-(P + 1) // A_SEL_LEN)
    ncols = -(-nsb // LANES) * LANES
    return pl.pallas_call(
        functools.partial(_nsa_sample_cmp_kernel, P=P, ncols=ncols), grid=(Bd,),
        in_specs=[pl.BlockSpec((1, rows2, nch), lambda b: (b, 0, 0)), pl.BlockSpec((1, H, gw), lambda b: (b, 0, 0))],
        out_specs=[pl.BlockSpec((1, H, HEAD_DIM), lambda b: (b, 0, 0)), pl.BlockSpec((1, 8, LANES), lambda b: (b, 0, 0))],
        out_shape=[jax.ShapeDtypeStruct((Bd, H, HEAD_DIM), F32), jax.ShapeDtypeStruct((Bd, 8, LANES), I32)],
        compiler_params=_cp("parallel"), name="nsa_sample_cmp_select",
    )(lohi, qt3)


def _nsa_sample_sel_kernel(pt_ref, ix_ref, *refs, P, nk):
    kblks, vblks = refs[0:A_KV], refs[A_KV:2 * A_KV]
    q_ref, new_ref, o_ref, m_ref, l_ref, acc_ref = refs[2 * A_KV:]
    b = pl.program_id(0)
    k = pl.program_id(1)
    R = q_ref.shape[2] // HEAD_DIM
    qb = P // A_SEL_LEN
    half_shift = A_SEL_LEN.bit_length() - 1
    lane = _iota((1, PAGE), 1)

    @pl.when(k == 0)
    def _():
        _tdec_init(m_ref, l_ref, acc_ref)

    for g in range(A_KV):
        j = ix_ref[b, g * nk + k]
        valid = ((lane >> half_shift) == (j & (PAGE // A_SEL_LEN - 1))) & (j < qb)
        kt = jnp.concatenate([kblks[g][...]] * R, axis=0)
        vt = jnp.concatenate([vblks[g][...]] * R, axis=0)
        _tdec_update(kt, vt, q_ref[0, g] * QK_SCALE, None, valid, m_ref, l_ref, acc_ref, g)

    @pl.when(k == nk - 1)
    def _():
        for g in range(A_KV):
            hit = ix_ref[b, g * nk] == qb
            for kk in range(1, nk):
                hit = hit | (ix_ref[b, g * nk + kk] == qb)
            kt = jnp.concatenate([new_ref[0, 0, g]] * R, axis=0)
            vt = jnp.concatenate([new_ref[0, 1, g]] * R, axis=0)
            _tdec_update(kt, vt, q_ref[0, g] * QK_SCALE, None, (lane == 0) & hit, m_ref, l_ref, acc_ref, g)
            o_ref[0, g] = _tdec_finish(l_ref, acc_ref, g)


def _nsa_sample_sel(cache_t6, layer, page_table, idx_flat, qexp, new_cols, P):
    Bd, G, rw = qexp.shape[0], qexp.shape[1], qexp.shape[2]
    R = rw // HEAD_DIM
    nk = idx_flat.shape[1] // A_KV
    per = PAGE // A_SEL_LEN
    last = P // A_SEL_LEN - 1

    def blk_index(c, g):
        def imap(b, k, pt, ix):
            j = jnp.minimum(ix[b, g * nk + k], last)
            return (layer, pt[b, j // per], c, g, 0, 0)
        return imap

    per_b = lambda shape: pl.BlockSpec((1,) + shape, lambda b, k, pt, ix: (b,) + (0,) * len(shape))
    blk = (None, None, None, None, HEAD_DIM, PAGE)
    grid_spec = pltpu.PrefetchScalarGridSpec(
        num_scalar_prefetch=2, grid=(Bd, nk),
        in_specs=[pl.BlockSpec(blk, blk_index(0, g)) for g in range(A_KV)]
        + [pl.BlockSpec(blk, blk_index(1, g)) for g in range(A_KV)]
        + [per_b((G, rw, LANES)), per_b((2, G, HEAD_DIM, LANES))],
        out_specs=per_b((G, R, HEAD_DIM, LANES)),
        scratch_shapes=_tdec_scratch(G, R))
    return pl.pallas_call(
        functools.partial(_nsa_sample_sel_kernel, P=P, nk=nk), grid_spec=grid_spec,
        out_shape=jax.ShapeDtypeStruct((Bd, G, R, HEAD_DIM, LANES), F32),
        compiler_params=_cp("parallel", "arbitrary"), name="nsa_sample_selected",
    )(page_table, idx_flat, *([cache_t6] * (2 * A_KV)), qexp, new_cols)


def _nsa_merge_kernel(gl_ref, ocmp_ref, osel_ref, owin_ref, o_ref):
    gates = _sigmoid(gl_ref[...])
    o_ref[...] = gates[:, :, 0:1] * ocmp_ref[...] + gates[:, :, 1:2] * osel_ref[...] + gates[:, :, 2:3] * owin_ref[...]


def _nsa_sample_merge(gl3, ocmp3, osel3, owin3):
    whole = lambda a: pl.BlockSpec(a.shape, lambda i: (0,) * a.ndim)
    return pl.pallas_call(
        _nsa_merge_kernel, grid=(1,),
        in_specs=[whole(gl3), whole(ocmp3), whole(osel3), whole(owin3)], out_specs=whole(ocmp3),
        out_shape=jax.ShapeDtypeStruct(ocmp3.shape, F32),
        compiler_params=_cp("arbitrary"), name="nsa_sample_merge",
    )(gl3, ocmp3, osel3, owin3)


def _slide_lanes(buf, new_cols):
    W = buf.shape[-1]
    axis = buf.ndim - 1
    new = jnp.concatenate([new_cols] * (W // LANES), axis=axis)
    return jnp.where(_iota(buf.shape, axis) == W - 1, new, pltpu.roll(buf, W - 1, axis))


def _nsa_sample_win_kernel(buf_ref, new_ref, qt_ref, nbuf_ref, o_ref):
    rows = buf_ref.shape[1]
    gw = rows // 2
    H = qt_ref.shape[1]
    nb = _slide_lanes(buf_ref[0], new_ref[0])
    nbuf_ref[0] = nb
    qbd, keep = _group_block_diag(qt_ref[0] * QK_SCALE, H, A_KV)
    s = _dot3(qbd, nb[0:gw], NN)
    e = jnp.exp(s - jnp.max(s, axis=-1, keepdims=True))
    p = e / jnp.maximum(jnp.sum(e, axis=-1, keepdims=True), TINY)
    o_ref[0] = _fold_groups(_dot3(p, nb[gw:rows], NT), keep, A_KV)


def _nsa_sample_win(buf_t, new_cols, qt3):
    Bd, rows, W = buf_t.shape
    H, gw = qt3.shape[1], qt3.shape[2]
    return pl.pallas_call(
        _nsa_sample_win_kernel, grid=(Bd,),
        in_specs=[pl.BlockSpec((1, rows, W), lambda b: (b, 0, 0)), pl.BlockSpec((1, rows, LANES), lambda b: (b, 0, 0)),
                  pl.BlockSpec((1, H, gw), lambda b: (b, 0, 0))],
        out_specs=[pl.BlockSpec((1, rows, W), lambda b: (b, 0, 0)), pl.BlockSpec((1, H, HEAD_DIM), lambda b: (b, 0, 0))],
        out_shape=[jax.ShapeDtypeStruct((Bd, rows, W), F32), jax.ShapeDtypeStruct((Bd, H, HEAD_DIM), F32)],
        compiler_params=_cp("parallel"), name="nsa_sample_window",
    )(buf_t, new_cols, qt3)


def _dsa_idx_kernel(pt_ref, *refs, pp):
    pages, qi_ref, wi_ref, o_ref = refs[:pp], refs[pp], refs[pp + 1], refs[pp + 2]
    wi = wi_ref[0] * (B_IDX_HEADS * B_IDX_DIM) ** -0.5
    qi = qi_ref[0]
    for pi in range(pp):
        logits = _dot3(qi, pages[pi][...], NN)
        o_ref[0, pi] = jnp.sum(wi * jnp.maximum(logits, 0.0), axis=0, keepdims=True)


def _dsa_sample_scores(kidx_t, layer, page_table, qi3, wi3, pp=8):
    Bd, NP = page_table.shape
    grid_spec = pltpu.PrefetchScalarGridSpec(
        num_scalar_prefetch=1, grid=(Bd, NP // pp),
        in_specs=_page_specs(pp, (None, None, B_IDX_DIM, PAGE), layer, lambda b, s, pi, pt: pt[b, s * pp + pi])
        + [pl.BlockSpec((1, B_IDX_HEADS, B_IDX_DIM), lambda b, s, pt: (b, 0, 0)),
           pl.BlockSpec((1, B_IDX_HEADS, 1), lambda b, s, pt: (b, 0, 0))],
        out_specs=pl.BlockSpec((1, pp, 1, PAGE), lambda b, s, pt: (b, s, 0, 0)))
    return pl.pallas_call(
        functools.partial(_dsa_idx_kernel, pp=pp), grid_spec=grid_spec,
        out_shape=jax.ShapeDtypeStruct((Bd, NP, 1, PAGE), F32),
        compiler_params=_cp("parallel", "arbitrary"), name="dsa_sample_indexer",
    )(page_table, *([kidx_t] * pp), qi3, wi3)


def _dsa_select_kernel(sc_ref, qi_ref, wi_ref, kin_ref, o_ref, *, topk):
    Bd, P = sc_ref.shape
    wi = wi_ref[...] * (B_IDX_HEADS * B_IDX_DIM) ** -0.5
    logit = jnp.sum(qi_ref[...] * kin_ref[...], axis=-1, keepdims=True)
    s_new = jnp.sum(wi * jnp.maximum(logit, 0.0), axis=1)
    tail = jnp.where(_iota((Bd, LANES), 1) == 0, s_new, -jnp.inf)
    full = jnp.concatenate([sc_ref[...], tail], axis=1)
    o_ref[...] = jnp.where(_topk_mask(_sortable_key(full), topk), 1.0, 0.0)


def _dsa_sample_select(scores2, qi3, wi3, kinew3, topk):
    Bd, P = scores2.shape
    whole = lambda shape: pl.BlockSpec(shape, lambda i: (0,) * len(shape))
    return pl.pallas_call(
        functools.partial(_dsa_select_kernel, topk=topk), grid=(1,),
        in_specs=[whole(scores2.shape), whole(qi3.shape), whole(wi3.shape), whole(kinew3.shape)],
        out_specs=whole((Bd, P + LANES)), out_shape=jax.ShapeDtypeStruct((Bd, P + LANES), F32),
        compiler_params=_cp("arbitrary"), name="dsa_sample_topk",
    )(scores2, qi3, wi3, kinew3)


def _dsa_sample_attn_kernel(pt_ref, *refs, pp, nsteps):
    pages = refs[:pp]
    mask_ref, q_ref, new_ref, mnew_ref, o_ref, m_ref, l_ref, acc_ref = refs[pp:]
    s_id = pl.program_id(1)
    R = q_ref.shape[1]
    gw = new_ref.shape[1] // 2

    @pl.when(s_id == 0)
    def _():
        _tdec_init(m_ref, l_ref, acc_ref)

    for pi in range(pp):
        kv = pages[pi][...]
        valid = mask_ref[0, pi] > 0.5
        for r in range(R):
            _tdec_update(kv[0:gw], kv[gw:2 * gw], q_ref[0, r] * QK_SCALE, None, valid, m_ref, l_ref, acc_ref, r)

    @pl.when(s_id == nsteps - 1)
    def _():
        new = new_ref[0]
        valid = (_iota((1, LANES), 1) == 0) & (mnew_ref[0] > 0.5)
        for r in range(R):
            _tdec_update(new[0:gw], new[gw:2 * gw], q_ref[0, r] * QK_SCALE, None, valid, m_ref, l_ref, acc_ref, r)
            o_ref[0, r] = _tdec_finish(l_ref, acc_ref, r)


def _dsa_sample_attn(kv_t, layer, page_table, mask4, qexp, new_cols, mnew3, pp=16):
    Bd, NP = page_table.shape
    rows = kv_t.shape[2]
    R, gw = qexp.shape[1], qexp.shape[2]
    G = gw // HEAD_DIM
    nsteps = NP // pp
    per_b = lambda shape: pl.BlockSpec((1,) + shape, lambda b, s, pt: (b,) + (0,) * len(shape))
    grid_spec = pltpu.PrefetchScalarGridSpec(
        num_scalar_prefetch=1, grid=(Bd, nsteps),
        in_specs=_page_specs(pp, (None, None, rows, PAGE), layer, lambda b, s, pi, pt: pt[b, s * pp + pi])
        + [pl.BlockSpec((1, pp, 1, PAGE), lambda b, s, pt: (b, s, 0, 0)), per_b((R, gw, LANES)), per_b((rows, LANES)),
           per_b((1, LANES))],
        out_specs=per_b((R, G, HEAD_DIM, LANES)),
        scratch_shapes=_tdec_scratch(R, G))
    return pl.pallas_call(
        functools.partial(_dsa_sample_attn_kernel, pp=pp, nsteps=nsteps), grid_spec=grid_spec,
        out_shape=jax.ShapeDtypeStruct((Bd, R, G, HEAD_DIM, LANES), F32),
        compiler_params=_cp("parallel", "arbitrary"), name="dsa_sample_attention",
    )(page_table, *([kv_t] * pp), mask4, qexp, new_cols, mnew3)


def _dil_sample_kernel(*refs, P):
    ng = len(C_GROUPS)
    buf_refs, new_refs, q_refs = refs[0:ng], refs[ng:2 * ng], refs[2 * ng:3 * ng]
    nbuf_refs, o_ref = refs[3 * ng:4 * ng], refs[4 * ng]
    res = []
    for (w, r), buf_ref, new_ref, q_ref, nbuf_ref in zip(C_GROUPS, buf_refs, new_refs, q_refs, nbuf_refs):
        buf = buf_ref[...]
        new = new_ref[...]
        W = buf.shape[2]
        nbuf_ref[...] = _slide_lanes(buf, new)
        q = q_ref[...] * QK_SCALE
        qw = jnp.concatenate([q] * (W // LANES), axis=1)
        s_old = jnp.sum(buf[0] * qw, axis=0, keepdims=True)
        dist = W - _iota((1, W), 1)
        valid = ((dist & (r - 1)) == 0) & (dist <= P)
        s_old = jnp.where(valid, s_old, NEG_INF)
        s_new = jnp.sum(new[0] * q, axis=0, keepdims=True)
        m = jnp.maximum(jnp.max(s_old, axis=-1, keepdims=True), s_new)
        e_old = jnp.where(valid, jnp.exp(s_old - m[:, 0:1]), 0.0)
        e_new = jnp.exp(s_new - m)
        den = jnp.sum(e_old, axis=-1, keepdims=True) + e_new
        o = (jnp.sum(buf[1] * e_old, axis=-1, keepdims=True) + new[1] * e_new) / jnp.maximum(den, TINY)
        res.append((o, m, den))
    m_all = functools.reduce(jnp.maximum, [m for _, m, _ in res])
    wts = [den * jnp.exp(m - m_all) for _, m, den in res]
    tot = functools.reduce(lambda a, b: a + b, wts)
    o_ref[...] = functools.reduce(lambda a, b: a + b, [(wt / tot) * o for wt, (o, _, _) in zip(wts, res)])


def _dil_sample(bufs_t, layer, news_cols, qs_cols, P):
    Bd, H = qs_cols[0].shape[0], qs_cols[0].shape[1]
    in_specs, out_specs, out_shape = [], [], []
    for buf in bufs_t:
        W = buf.shape[5]
        in_specs.append(pl.BlockSpec((None, None, 2, None, HEAD_DIM, W), lambda b, h: (layer, b, 0, h, 0, 0)))
        out_specs.append(pl.BlockSpec((None, 2, None, HEAD_DIM, W), lambda b, h: (b, 0, h, 0, 0)))
        out_shape.append(jax.ShapeDtypeStruct((Bd, 2, H, HEAD_DIM, W), F32))
    in_specs += [pl.BlockSpec((None, 2, None, HEAD_DIM, LANES), lambda b, h: (b, 0, h, 0, 0))] * len(bufs_t)
    in_specs += [pl.BlockSpec((None, None, HEAD_DIM, LANES), lambda b, h: (b, h, 0, 0))] * len(bufs_t)
    out_specs.append(pl.BlockSpec((None, None, HEAD_DIM, LANES), lambda b, h: (b, h, 0, 0)))
    out_shape.append(jax.ShapeDtypeStruct((Bd, H, HEAD_DIM, LANES), F32))
    return pl.pallas_call(
        functools.partial(_dil_sample_kernel, P=P), grid=(Bd, H),
        in_specs=in_specs, out_specs=out_specs, out_shape=out_shape,
        compiler_params=_cp("parallel", "parallel"), name="dilated_sample",
    )(*bufs_t, *news_cols, *qs_cols)


def _fox_sample_kernel(pt_ref, *refs, pp, nsteps):
    pages, lfs = refs[:pp], refs[pp:2 * pp]
    q_ref, new_ref, lfnew_ref, o_ref, m_ref, l_ref, acc_ref, carry_ref = refs[2 * pp:]
    s_id = pl.program_id(1)
    L = new_ref.shape[1] // 2
    q = q_ref[0, 0] * QK_SCALE

    @pl.when(s_id == 0)
    def _():
        _tdec_init(m_ref, l_ref, acc_ref)
        new = new_ref[0]
        _tdec_update(new[0:L], new[L:2 * L], q, None, _iota((1, LANES), 1) == 0, m_ref, l_ref, acc_ref, 0)
        carry_ref[...] = lfnew_ref[0]

    later = _onehot(_iota((PAGE, PAGE), 0) > _iota((PAGE, PAGE), 1))
    for pi in range(pp):
        kv = pages[pi][...]
        lf = lfs[pi][...]
        carry = carry_ref[...]
        bias = _dotx(lf, later, 3) + carry
        carry_ref[...] = carry + jnp.sum(lf, axis=-1, keepdims=True)
        _tdec_update(kv[0:L], kv[L:2 * L], q, bias, None, m_ref, l_ref, acc_ref, 0)

    @pl.when(s_id == nsteps - 1)
    def _():
        o_ref[0, 0] = _tdec_finish(l_ref, acc_ref, 0)


def _fox_sample(kv_t, lf_t, layer, page_table, qexp, new_cols, lfnew3, pp=8):
    Bd, NP = page_table.shape
    rows = kv_t.shape[2]
    H = lf_t.shape[2]
    nsteps = NP // pp
    rev = lambda b, s, pi, pt: pt[b, NP - 1 - (s * pp + pi)]
    per_b = lambda shape: pl.BlockSpec((1,) + shape, lambda b, s, pt: (b,) + (0,) * len(shape))
    grid_spec = pltpu.PrefetchScalarGridSpec(
        num_scalar_prefetch=1, grid=(Bd, nsteps),
        in_specs=_page_specs(pp, (None, None, rows, PAGE), layer, rev) + _page_specs(pp, (None, None, H, PAGE), layer, rev)
        + [per_b((1, rows // 2, LANES)), per_b((rows, LANES)), per_b((H, LANES))],
        out_specs=per_b((1, H, HEAD_DIM, LANES)),
        scratch_shapes=_tdec_scratch(1, H) + [pltpu.VMEM((H, LANES), F32)])
    return pl.pallas_call(
        functools.partial(_fox_sample_kernel, pp=pp, nsteps=nsteps), grid_spec=grid_spec,
        out_shape=jax.ShapeDtypeStruct((Bd, 1, H, HEAD_DIM, LANES), F32),
        compiler_params=_cp("parallel", "arbitrary"), name="fox_sample",
    )(page_table, *([kv_t] * pp), *([lf_t] * pp), qexp, new_cols, lfnew3)


def _cols(x, lead):
    Bd = x.shape[0]
    return jnp.broadcast_to(x.reshape((Bd,) + lead + (1,)), (Bd,) + lead + (LANES,))


def _rows_minor(cache):
    n = cache.ndim
    return jnp.moveaxis(cache, n - 4, n - 1)


def _rows_major(x):
    n = x.ndim
    return jnp.moveaxis(x, n - 1, n - 4)


def kernel(x_prompt, x_sample, cache_a_cmp, cache_a_sel, cache_a_win, cache_b_kv, cache_b_kidx, cache_c_win0, cache_c_win1, cache_c_win2, cache_d_kv, cache_d_logf, page_table, c_prompt, c_sample, ada_w, ada_b, norm_g, final_g, a_w_in, a_w_out, a_cmp_wk, a_cmp_wv, b_w_in, b_w_out, c_w_in, c_w_out, d_w_in, d_w_out, d_f_bias):
    B, T, D = x_prompt.shape
    Bd = x_sample.shape[0]
    depth = ada_w.shape[0]
    NP = page_table.shape[1]
    P = NP * PAGE
    pool = cache_b_kidx.shape[1]
    assert x_sample.shape[1] == 1 and D == 1024 and P % A_SEL_LEN == 0
    assert cache_a_win.shape[2] == A_WINDOW
    assert all(b.shape[2] == w for b, (w, _) in zip((cache_c_win0, cache_c_win1, cache_c_win2), C_GROUPS))
    H = D // HEAD_DIM
    R = H // A_KV
    tm_p = 256
    tpb = T // tm_p

    mod = _mod_all(jnp.concatenate([c_prompt, c_sample], axis=0), ada_w, ada_b)
    pos_p = jnp.arange(T, dtype=I32)
    pos_s = jnp.full((Bd,), P, I32)
    tab_p, ttab_p = _rope_tables(pos_p), _rope_tables_t(pos_p)
    tab_s = _rope_tables(pos_s)
    xp = x_prompt.reshape(B * T, D)
    xs = x_sample.reshape(Bd, D)
    c_bufs = (cache_c_win0, cache_c_win1, cache_c_win2)
    st = {}
    put = lambda name, val: st.setdefault(name, []).append(val)
    rows_of = lambda t, c, g: _rows_major(t.reshape(t.shape[0], c, g, HEAD_DIM, t.shape[2]))
    yp = ys = None
    for i in range(depth):
        kind, li = i % 4, i // 4
        shift_p, scale_p, gate_p = [mod[i, :B, j * D:(j + 1) * D].reshape(B, 1, D) for j in range(3)]
        shift_s, scale_s, gate_s = [mod[i, B:, j * D:(j + 1) * D].reshape(1, Bd, D) for j in range(3)]
        proj_p = lambda w, segs: _proj(xp, scale_p, shift_p, norm_g[i], w, segs, tab_p, ttab_p, tpb, tm_p, PASSES_PROMPT_PROJ)
        proj_s = lambda w, segs: _proj(xs, scale_s, shift_s, norm_g[i], w, segs, tab_s, None, 1, Bd, PASSES_SELECTIVE)
        if kind == 0:
            w = a_w_in[li]
            gw = A_KV * HEAD_DIM
            q, qr, gl, z, cmp_t, sel_t, win_t = proj_p(w, [
                ("rm", 0, D, ("n", "r")), ("rm", D + 6 * gw, 3 * H, ("n",)), ("rm", D + 6 * gw + 3 * H, D, ("n",)),
                ("t", D, 2 * gw, "t"), ("t", D + 2 * gw, 2 * gw, "trk"), ("t", D + 4 * gw, 2 * gw, "trk")])
            wlo, whi = _cmp_taps(a_cmp_wk[li], a_cmp_wv[li], A_KV)
            kvb = _nsa_compress_prompt(cmp_t, wlo, whi)
            o = _nsa_prompt(q.reshape(B, T, D), qr.reshape(B, T, D), kvb, sel_t, win_t, gl.reshape(B, T, LANES),
                            PASSES_NSA_FLASH)
            op, zp, wo = o.reshape(B * T, D), z, a_w_out[li]
            put("a_cmp_p", rows_of(cmp_t, 2, A_KV))
            put("a_sel_p", rows_of(sel_t, 2, A_KV))
            put("a_win_p", rows_of(win_t[:, :, T - min(A_WINDOW, T):], 2, A_KV))

            q, qr, cmp_, sel, win, gl, z = proj_s(w, [
                ("rm", 0, D, ("n", "r")), ("rm", D, 2 * gw, ("n",)), ("rm", D + 2 * gw, 2 * gw, ("rk",)),
                ("rm", D + 4 * gw, 2 * gw, ("rk",)), ("rm", D + 6 * gw, 3 * H, ("n",)), ("rm", D + 6 * gw + 3 * H, D, ("n",))])
            cmp_t6 = _rows_minor(cache_a_cmp)
            lohi = _nsa_compress_sample(cmp_t6.reshape(cmp_t6.shape[0], pool, 2 * gw, PAGE), li, page_table, wlo, whi)
            tile_g = lambda t: jnp.tile(t.reshape(Bd, H, HEAD_DIM), (1, 1, A_KV))
            ocmp, idx = _nsa_sample_cmp(lohi, tile_g(q), P)
            win_t6 = _rows_minor(cache_a_win)
            nwin, owin = _nsa_sample_win(win_t6[li].reshape(Bd, 2 * gw, A_WINDOW), _cols(win, (2 * gw,)), tile_g(qr))
            nk = min(A_SEL_TOPK, -(-(P + 1) // A_SEL_LEN))
            osel = _nsa_sample_sel(_rows_minor(cache_a_sel), li, page_table, idx[:, :A_KV, :nk].reshape(Bd, A_KV * nk),
                                   _cols(qr, (A_KV, R * HEAD_DIM)), _cols(sel, (2, A_KV, HEAD_DIM)), P)
            o = _nsa_sample_merge(gl[:, :3 * H].reshape(Bd, H, 3), ocmp, osel[..., 0].reshape(Bd, H, HEAD_DIM), owin)
            os_, zs = o.reshape(Bd, D), z
            put("a_cmp_s", cmp_.reshape(Bd, 1, 2, A_KV, HEAD_DIM))
            put("a_sel_s", sel.reshape(Bd, 1, 2, A_KV, HEAD_DIM))
            put("a_win_s", _rows_major(nwin.reshape(Bd, 2, A_KV, HEAD_DIM, A_WINDOW)))
        elif kind == 1:
            w = b_w_in[li]
            gw = A_KV * HEAD_DIM
            c_qi = D + 2 * gw
            c_ki = c_qi + B_IDX_HEADS * B_IDX_DIM
            c_wi = c_ki + B_IDX_DIM
            c_z = c_wi + B_IDX_HEADS
            q, qi, wi, z, kv_t, ki_t = proj_p(w, [
                ("rm", 0, D, ("r",)), ("rm", c_qi, c_ki - c_qi, ("r",)), ("rm", c_wi, B_IDX_HEADS, ("n",)), ("rm", c_z, D, ("n",)),
                ("t", D, 2 * gw, "trk"), ("t", c_ki, B_IDX_DIM, "tr")])
            o = _dsa_prompt(q.reshape(B, T, D), qi.reshape(B, T, c_ki - c_qi), wi.reshape(B, T, LANES), kv_t, ki_t,
                            min(B_TOPK_MAX, T // 4), PASSES_SMOOTH, PASSES_PROMPT_INDEXER)
            op, zp, wo = o.reshape(B * T, D), z, b_w_out[li]
            put("b_kv_p", rows_of(kv_t, 2, A_KV))
            put("b_kidx_p", jnp.swapaxes(ki_t, 1, 2))

            q, kv, qi, ki, wi, z = proj_s(w, [
                ("rm", 0, D, ("r",)), ("rm", D, 2 * gw, ("rk",)), ("rm", c_qi, c_ki - c_qi, ("r",)), ("rm", c_ki, B_IDX_DIM, ("r",)),
                ("rm", c_wi, B_IDX_HEADS, ("n",)), ("rm", c_z, D, ("n",))])
            qi3 = qi.reshape(Bd, B_IDX_HEADS, B_IDX_DIM)
            wi3 = wi[:, :B_IDX_HEADS].reshape(Bd, B_IDX_HEADS, 1)
            ki_new = ki[:, :B_IDX_DIM]
            sc = _dsa_sample_scores(jnp.swapaxes(cache_b_kidx, 2, 3), li, page_table, qi3, wi3)
            mask = _dsa_sample_select(sc.reshape(Bd, P), qi3, wi3, ki_new.reshape(Bd, 1, B_IDX_DIM),
                                      min(B_TOPK_MAX, (P + 1) // 4))
            kv_t6 = _rows_minor(cache_b_kv)
            qexp = _cols(q.reshape(Bd, A_KV, R, HEAD_DIM).transpose(0, 2, 1, 3).reshape(Bd, R, gw), (R, gw))
            o = _dsa_sample_attn(kv_t6.reshape(kv_t6.shape[0], pool, 2 * gw, PAGE), li, page_table,
                                 mask[:, :P].reshape(Bd, NP, 1, PAGE), qexp, _cols(kv, (2 * gw,)),
                                 jnp.broadcast_to(mask[:, P:P + 1], (Bd, LANES)).reshape(Bd, 1, LANES))
            os_, zs = o[..., 0].transpose(0, 2, 1, 3).reshape(Bd, D), z
            put("b_kv_s", kv.reshape(Bd, 1, 2, A_KV, HEAD_DIM))
            put("b_kidx_s", ki_new.reshape(Bd, 1, B_IDX_DIM))
        elif kind == 2:
            w = c_w_in[li]
            Wd = C_HEADS * HEAD_DIM
            ng = len(C_GROUPS)
            res = proj_p(w, [("rm", 3 * g * Wd, Wd, ("r",)) for g in range(ng)] + [("rm", 3 * ng * Wd, Wd, ("n",))]
                         + [("t", (3 * g + 1) * Wd, 2 * Wd, "trk") for g in range(ng)])
            qs, z, kvs_t = res[0:ng], res[ng], res[ng + 1:]
            o = _dil_prompt([t.reshape(B, T, Wd) for t in qs], kvs_t, PASSES_SMOOTH)
            op, zp, wo = o.reshape(B * T, Wd), z, c_w_out[li]
            for g, (wg, _) in enumerate(C_GROUPS):
                put("c_win%d_p" % g, rows_of(kvs_t[g][:, :, T - min(wg, T):], 2, C_HEADS))

            res = proj_s(w, [seg for g in range(ng) for seg in (("rm", 3 * g * Wd, Wd, ("r",)), ("rm", (3 * g + 1) * Wd, 2 * Wd, ("rk",)))]
                         + [("rm", 3 * ng * Wd, Wd, ("n",))])
            qs, kvs, z = res[0:2 * ng:2], res[1:2 * ng:2], res[2 * ng]
            outs = _dil_sample([_rows_minor(b) for b in c_bufs], li, [_cols(t, (2, C_HEADS, HEAD_DIM)) for t in kvs],
                               [_cols(t, (C_HEADS, HEAD_DIM)) for t in qs], P)
            os_, zs = outs[ng][..., 0].reshape(Bd, Wd), z
            for g in range(ng):
                put("c_win%d_s" % g, _rows_major(outs[g]))
        else:
            w = d_w_in[li]
            q, z, kv_t, f_t = proj_p(w, [("rm", 0, D, ("n",)), ("rm", 3 * D + H, D, ("n",)), ("t", D, 2 * D, "t"), ("t", 3 * D, H, "t")])
            fb = d_f_bias[li].reshape(H, 1)
            lf_t, c_t = _fox_logf(f_t, fb)
            o = _fox_prompt(q.reshape(B, T, D), kv_t, jnp.swapaxes(c_t, 1, 2), c_t, PASSES_SMOOTH)
            op, zp, wo = o.reshape(B * T, D), z, d_w_out[li]
            put("d_kv_p", rows_of(kv_t, 2, H))
            put("d_logf_p", jnp.swapaxes(lf_t, 1, 2))

            q, kv, f, z = proj_s(w, [("rm", 0, D, ("n",)), ("rm", D, 2 * D, ("n",)), ("rm", 3 * D, H, ("n",)), ("rm", 3 * D + H, D, ("n",))])
            lf = _fox_logf(f[:, :H].T.reshape(1, H, Bd), fb)[0][0].T
            kv_t6 = _rows_minor(cache_d_kv)
            o = _fox_sample(kv_t6.reshape(kv_t6.shape[0], pool, 2 * D, PAGE), jnp.swapaxes(cache_d_logf, 2, 3), li, page_table,
                            _cols(q, (1, D)), _cols(kv, (2 * D,)), _cols(lf, (H,)))
            os_, zs = o[..., 0].reshape(Bd, D), z
            put("d_kv_s", kv.reshape(Bd, 1, 2, H, HEAD_DIM))
            put("d_logf_s", lf.reshape(Bd, 1, H))
        fg = final_g if i == depth - 1 else None
        xp, yp = _gated_out(op, zp, wo, xp, gate_p, tpb, tm_p, PASSES_PROMPT_PROJ, fg)
        xs, ys = _gated_out(os_, zs, wo, xs, gate_s, 1, Bd, PASSES_SELECTIVE, fg)
    ns = {n: jnp.stack(v) for n, v in st.items()}
    names = ("a_cmp_p", "a_cmp_s", "a_sel_p", "a_sel_s", "a_win_p", "a_win_s", "b_kv_p", "b_kv_s", "b_kidx_p", "b_kidx_s",
             "c_win0_p", "c_win0_s", "c_win1_p", "c_win1_s", "c_win2_p", "c_win2_s", "d_kv_p", "d_kv_s", "d_logf_p", "d_logf_s")
    return (yp.reshape(B, T, D), ys.reshape(Bd, 1, D)) + tuple(ns[n] for n in names)
```

```python
import functools

import jax
import jax.numpy as jnp
from jax import lax
from jax.experimental import pallas as pl
from jax.experimental.pallas import tpu as pltpu

F32 = jnp.float32
BF16 = jnp.bfloat16
I32 = jnp.int32

HEAD_DIM = 64
ROPE_DIMS = HEAD_DIM // 4
ROPE_HALF = ROPE_DIMS // 2
ROPE_THETA = 500000.0
NORM_EPS = 1e-6
NEG_INF = -1e30
TINY = 1e-30
PAGE = 128
LANES = 128
INT_MIN = -2 ** 31

A_KV = 4
A_CMP_STRIDE = 16
A_CMP_LEN = 32
A_SEL_LEN = 64
A_SEL_TOPK = 16
A_WINDOW = 512
A_FORCED = 1e4
B_IDX_HEADS = 8
B_IDX_DIM = 64
B_TOPK_MAX = 256
C_GROUPS = ((128, 1), (512, 4), (2048, 16))
C_HEADS = 8
QK_SCALE = HEAD_DIM ** -0.5

NN = ((1,), (0,))
NT = ((1,), (1,))

VMEM_LIMIT = 56 * 1024 * 1024
PROJ_COLS_PER_CALL = 2048

PASSES_SELECTIVE = 3
PASSES_SMOOTH = 1
PASSES_NSA_FLASH = 1
PASSES_PROMPT_PROJ = 1
PASSES_PROMPT_INDEXER = 1


def _cp(*sem):
    return pltpu.CompilerParams(dimension_semantics=sem, vmem_limit_bytes=VMEM_LIMIT)


def _dotf(a, b, dims=NN):
    return lax.dot_general(a, b, (dims, ((), ())), preferred_element_type=F32)


def _split(a, terms):
    out = []
    for _ in range(terms - 1):
        h = a.astype(BF16)
        out.append(h)
        a = a - h.astype(F32)
    out.append(a.astype(BF16))
    return out


def _dot3(a, b, dims=NN):
    ah, al = _split(a, 2)
    bh, bl = _split(b, 2)
    return _dotf(ah, bh, dims) + (_dotf(ah, bl, dims) + _dotf(al, bh, dims))


def _dotp(a, b, dims, passes):
    if passes == 1:
        return _dotf(a.astype(BF16), b.astype(BF16), dims)
    return _dot3(a, b, dims)


def _dotx(a, m01, terms, dims=NN):
    acc = None
    for t in _split(a, terms):
        d = _dotf(t, m01, dims)
        acc = d if acc is None else acc + d
    return acc


def _iota(shape, axis):
    return lax.broadcasted_iota(I32, shape, axis)


def _onehot(cond):
    return jnp.where(cond, 1.0, 0.0).astype(BF16)


def _sigmoid(x):
    return 1.0 / (1.0 + jnp.exp(-x))


def _sortable_key(x):
    x = jnp.where(x == 0.0, 0.0, x)
    b = lax.bitcast_convert_type(x, I32)
    return jnp.where(b < 0, b ^ 0x7FFFFFFF, b)


def _count_ge(key, cand):
    return jnp.sum(jnp.where(key >= cand, 1.0, 0.0), axis=-1, keepdims=True)


def _kth_largest_key(key, k):
    base = jnp.where(_count_ge(key, 0) >= k, 0, INT_MIN).astype(I32)

    def body(it, base):
        cand = base | jnp.left_shift(jnp.int32(1), 30 - it)
        return jnp.where(_count_ge(key, cand) >= k, cand, base)

    return lax.fori_loop(0, 31, body, base)


def _topk_mask(key, k):
    n = key.shape[1]
    thr = _kth_largest_key(key, k)
    gt = key > thr
    eq = key == thr
    need = k - jnp.sum(jnp.where(gt, 1.0, 0.0), axis=-1, keepdims=True)
    before = _onehot(_iota((LANES, LANES), 0) < _iota((LANES, LANES), 1))
    run = jnp.zeros_like(need)
    out = []
    for c in range(n // LANES):
        sl = slice(c * LANES, (c + 1) * LANES)
        eqf = jnp.where(eq[:, sl], 1.0, 0.0)
        prior = _dotf(eqf.astype(BF16), before) + run
        out.append(gt[:, sl] | (eq[:, sl] & (prior < need)))
        run = run + jnp.sum(eqf, axis=-1, keepdims=True)
    return jnp.concatenate(out, axis=1)


def _flash_streams(streams, c_lo, c_hi, tk, passes, c_free=None):
    ones_rows = jnp.ones((8, tk), F32)

    def step(c, carry, masked):
        off = pl.multiple_of(c * tk, tk)
        out = []
        for (q, kt_ref, vt_ref, krow, vrow, mask_fn, bias_fn), (m, acc) in zip(streams, carry):
            kt = kt_ref[0, krow:krow + HEAD_DIM, pl.ds(off, tk)]
            vt = jnp.concatenate([vt_ref[0, vrow:vrow + HEAD_DIM, pl.ds(off, tk)], ones_rows], axis=0)
            s = _dotp(q, kt, NN, passes)
            if bias_fn is not None:
                s = s + bias_fn(c)
            if masked:
                s = mask_fn(c, s)
            m_new = jnp.maximum(m, jnp.max(s, axis=-1, keepdims=True))
            e = jnp.exp(s - m_new)
            out.append((m_new, jnp.exp(m - m_new) * acc + _dotp(e, vt, NT, passes)))
        return tuple(out)

    carry = tuple((jnp.full((s[0].shape[0], 1), 0.1 * NEG_INF, F32), jnp.zeros((s[0].shape[0], HEAD_DIM + 8), F32))
                  for s in streams)
    if c_free is not None:
        carry = lax.fori_loop(c_lo, c_free, functools.partial(step, masked=False), carry)
        c_lo = c_free
    carry = lax.fori_loop(c_lo, c_hi, functools.partial(step, masked=True), carry)
    res = []
    for m, acc in carry:
        den = acc[:, HEAD_DIM:HEAD_DIM + 1]
        res.append((acc[:, 0:HEAD_DIM] / jnp.maximum(den, TINY), m, den))
    return res


def _flash(q, kt_ref, vt_ref, krow, vrow, c_lo, c_hi, tk, mask_fn, passes, bias_fn=None, c_free=None):
    return _flash_streams([(q, kt_ref, vt_ref, krow, vrow, mask_fn, bias_fn)], c_lo, c_hi, tk, passes, c_free)[0]


def _stack_heads(ref, first_head, n):
    return jnp.concatenate(
        [ref[0, :, (first_head + r) * HEAD_DIM:(first_head + r + 1) * HEAD_DIM] for r in range(n)], axis=0)


def _group_block_diag(qt, n_heads, n_groups):
    per = n_heads // n_groups
    shift = per.bit_length() - 1
    keep = (_iota(qt.shape, 0) >> shift) == (_iota(qt.shape, 1) >> 6)
    return jnp.where(keep, qt, 0.0), keep


def _fold_groups(x, keep, n_groups):
    x = jnp.where(keep, x, 0.0)
    return functools.reduce(lambda a, b: a + b, [x[:, g * HEAD_DIM:(g + 1) * HEAD_DIM] for g in range(n_groups)])


def _mod_kernel(c_ref, w_ref, b_ref, o_ref):
    c = c_ref[...]
    o_ref[0] = _dot3(c * _sigmoid(c), w_ref[0]) + b_ref[0]


def _mod_all(c_all, ada_w, ada_b):
    L, D, D3 = ada_w.shape
    NC = c_all.shape[0]
    tn = 1024
    return pl.pallas_call(
        _mod_kernel, grid=(L, D3 // tn),
        in_specs=[pl.BlockSpec((NC, D), lambda l, j: (0, 0)),
                  pl.BlockSpec((1, D, tn), lambda l, j: (l, 0, j)),
                  pl.BlockSpec((1, 1, tn), lambda l, j: (l, 0, j))],
        out_specs=pl.BlockSpec((1, NC, tn), lambda l, j: (l, 0, j)),
        out_shape=jax.ShapeDtypeStruct((L, NC, D3), F32),
        compiler_params=_cp("parallel", "parallel"), name="adaln_mod",
    )(c_all, ada_w, ada_b.reshape(L, 1, D3))


def _proj_kernel(*refs, rsegs, tsegs, passes):
    x_ref, sc_ref, sh_ref, g_ref = refs[0:4]
    pos = 4
    if rsegs:
        wh_ref, wl_ref, cos_ref, sn_ref, sp_ref = refs[pos:pos + 5]
        pos += 5
    if tsegs:
        wth_ref, wtl_ref, cost_ref, sint_ref = refs[pos:pos + 4]
        pos += 4
    out_refs = refs[pos:]
    x = x_ref[...]
    y = x * lax.rsqrt(jnp.mean(x * x, axis=-1, keepdims=True) + NORM_EPS) * g_ref[...]
    h = y * (1.0 + sc_ref[0]) + sh_ref[0]
    hh, hl = _split(h, 2)
    oi = 0
    for start, width, modes in rsegs:
        wh = wh_ref[:, start:start + width]
        u = _dotf(hh, wh)
        if passes == 3:
            u = u + (_dotf(hh, wl_ref[:, start:start + width]) + _dotf(hl, wh))
        for mode in modes:
            o_ref = out_refs[oi]
            oi += 1
            n_rope = {"n": 0, "r": width, "rk": width // 2}[mode]
            for c in range(width // LANES):
                sl = slice(c * LANES, (c + 1) * LANES)
                uc = u[:, sl]
                if c * LANES < n_rope:
                    uc = (uc * cos_ref[...] + pltpu.roll(uc, LANES - ROPE_HALF, 1) * sn_ref[...]
                          + pltpu.roll(uc, ROPE_HALF, 1) * sp_ref[...])
                o_ref[:, sl] = uc
    for start, width, mode in tsegs:
        wth = wth_ref[start:start + width, :]
        ut = _dotf(wth, hh, NT)
        if passes == 3:
            ut = ut + (_dotf(wth, hl, NT) + _dotf(wtl_ref[start:start + width, :], hh, NT))
        o_ref = out_refs[oi]
        oi += 1
        o_ref[0] = ut
        n_rope = {"t": 0, "tr": width, "trk": width // 2}[mode]
        for hd in range(n_rope // HEAD_DIM):
            r0 = hd * HEAD_DIM
            x1 = ut[r0:r0 + ROPE_HALF]
            x2 = ut[r0 + ROPE_HALF:r0 + ROPE_DIMS]
            o_ref[0, r0:r0 + ROPE_HALF, :] = x1 * cost_ref[...] - x2 * sint_ref[...]
            o_ref[0, r0 + ROPE_HALF:r0 + ROPE_DIMS, :] = x2 * cost_ref[...] + x1 * sint_ref[...]


def _proj(x2, scale, shift, g, w, segs, tables, ttables, tiles_per_batch, tm, passes):
    R, D = x2.shape
    NB, RB = scale.shape[0], scale.shape[1]
    Tt = R // NB
    groups, cur, cur_w = [], [], 0
    for seg in segs:
        wd = -(-seg[2] // LANES) * LANES
        if cur and cur_w + wd > PROJ_COLS_PER_CALL:
            groups.append(cur)
            cur, cur_w = [], 0
        cur.append(seg)
        cur_w += wd
    groups.append(cur)
    results = {}
    for grp in groups:
        rsegs, tsegs, rcols, tcols = [], [], [], []
        out_shapes, out_specs, keys = [], [], []
        rpos = tpos = 0
        for si, (kind, c0, wd, spec) in enumerate(grp):
            if kind == "rm":
                pw = -(-wd // LANES) * LANES
                rcols.append(w[:, c0:c0 + wd])
                if pw > wd:
                    rcols.append(jnp.zeros((D, pw - wd), F32))
                rsegs.append((rpos, pw, spec))
                rpos += pw
                for mi in range(len(spec)):
                    out_shapes.append(jax.ShapeDtypeStruct((R, pw), F32))
                    out_specs.append(pl.BlockSpec((tm, pw), lambda i: (i, 0)))
                    keys.append((id(grp), si, mi))
        for si, (kind, c0, wd, spec) in enumerate(grp):
            if kind == "t":
                tcols.append(w[:, c0:c0 + wd])
                tsegs.append((tpos, wd, spec))
                tpos += wd
                out_shapes.append(jax.ShapeDtypeStruct((NB, wd, Tt), F32))
                out_specs.append(pl.BlockSpec((1, wd, tm), lambda i: (i // tiles_per_batch, 0, i % tiles_per_batch)))
                keys.append((id(grp), si, 0))
        mod_spec = pl.BlockSpec((1, RB, D), lambda i: (i // tiles_per_batch, 0, 0))
        in_specs = [pl.BlockSpec((tm, D), lambda i: (i, 0)), mod_spec, mod_spec, pl.BlockSpec((1, D), lambda i: (0, 0))]
        args = [x2, scale, shift, g.reshape(1, D)]
        if rsegs:
            wr = jnp.concatenate(rcols, axis=1)
            wh = wr.astype(BF16)
            wl = (wr - wh.astype(F32)).astype(BF16) if passes == 3 else wh
            tab_spec = pl.BlockSpec((tm, LANES), lambda i: (i % tiles_per_batch, 0))
            in_specs += [pl.BlockSpec((D, rpos), lambda i: (0, 0))] * 2 + [tab_spec] * 3
            args += [wh, wl, *tables]
        if tsegs:
            wt = jnp.concatenate(tcols, axis=1).T
            wth = wt.astype(BF16)
            wtl = (wt - wth.astype(F32)).astype(BF16) if passes == 3 else wth
            ttab_spec = pl.BlockSpec((ROPE_HALF, tm), lambda i: (0, i % tiles_per_batch))
            in_specs += [pl.BlockSpec((tpos, D), lambda i: (0, 0))] * 2 + [ttab_spec] * 2
            args += [wth, wtl, *ttables]
        res = pl.pallas_call(
            functools.partial(_proj_kernel, rsegs=tuple(rsegs), tsegs=tuple(tsegs), passes=passes), grid=(R // tm,),
            in_specs=in_specs, out_specs=out_specs, out_shape=out_shapes,
            compiler_params=_cp("parallel"), name="norm_mod_proj",
        )(*args)
        for key, r in zip(keys, res):
            results[key] = r
    outs = []
    for grp in groups:
        for si, (kind, c0, wd, spec) in enumerate(grp):
            for mi in range(len(spec) if kind == "rm" else 1):
                outs.append(results[(id(grp), si, mi)])
    return outs


def _rope_angles(pos):
    inv = ROPE_THETA ** (-(jnp.arange(ROPE_HALF, dtype=F32) / ROPE_HALF))
    ang = pos.astype(F32)[:, None] * inv[None, :]
    return jnp.cos(ang), jnp.sin(ang)


def _rope_tables(pos):
    cos, sin = _rope_angles(pos)
    R = pos.shape[0]
    one = jnp.ones((R, HEAD_DIM - ROPE_DIMS), F32)
    zero = jnp.zeros((R, HEAD_DIM - ROPE_DIMS), F32)
    zh = jnp.zeros((R, ROPE_HALF), F32)
    cos_h = jnp.concatenate([cos, cos, one], axis=1)
    sn_h = jnp.concatenate([-sin, zh, zero], axis=1)
    sp_h = jnp.concatenate([zh, sin, zero], axis=1)
    rep = LANES // HEAD_DIM
    return tuple(jnp.tile(t, (1, rep)) for t in (cos_h, sn_h, sp_h))


def _rope_tables_t(pos):
    cos, sin = _rope_angles(pos)
    return cos.T, sin.T


def _out_kernel(o_ref, z_ref, wh_ref, wl_ref, x_ref, gate_ref, *rest, final, passes):
    z = z_ref[...]
    y = o_ref[...] * (z * _sigmoid(z))
    yh, yl = _split(y, 2)
    wh = wh_ref[...]
    r = _dotf(yh, wh)
    if passes == 3:
        r = r + (_dotf(yh, wl_ref[...]) + _dotf(yl, wh))
    xn = x_ref[...] + gate_ref[0] * r
    if final:
        fg_ref, xo_ref, yo_ref = rest
        yo_ref[...] = xn * lax.rsqrt(jnp.mean(xn * xn, axis=-1, keepdims=True) + NORM_EPS) * fg_ref[...]
    else:
        (xo_ref,) = rest
    xo_ref[...] = xn


def _gated_out(o2, z2, w_out, x2, gate, tiles_per_batch, tm, passes, final_g=None):
    R, W = o2.shape
    D = x2.shape[1]
    RB = gate.shape[1]
    wh = w_out.astype(BF16)
    wl = (w_out - wh.astype(F32)).astype(BF16) if passes == 3 else wh
    final = final_g is not None
    in_specs = [pl.BlockSpec((tm, W), lambda i: (i, 0)), pl.BlockSpec((tm, W), lambda i: (i, 0)),
                pl.BlockSpec((W, D), lambda i: (0, 0)), pl.BlockSpec((W, D), lambda i: (0, 0)),
                pl.BlockSpec((tm, D), lambda i: (i, 0)),
                pl.BlockSpec((1, RB, D), lambda i: (i // tiles_per_batch, 0, 0))]
    args = [o2, z2, wh, wl, x2, gate]
    out_shape = [jax.ShapeDtypeStruct((R, D), F32)]
    out_specs = [pl.BlockSpec((tm, D), lambda i: (i, 0))]
    if final:
        in_specs.append(pl.BlockSpec((1, D), lambda i: (0, 0)))
        args.append(final_g.reshape(1, D))
        out_shape.append(jax.ShapeDtypeStruct((R, D), F32))
        out_specs.append(pl.BlockSpec((tm, D), lambda i: (i, 0)))
    res = pl.pallas_call(
        functools.partial(_out_kernel, final=final, passes=passes), grid=(R // tm,),
        in_specs=in_specs, out_specs=out_specs, out_shape=out_shape,
        compiler_params=_cp("parallel"), name="gated_out_proj",
    )(*args)
    return res if final else (res[0], None)


def _cmp_taps(wk, wv, groups):
    def table(w):
        t = jnp.tile(w.T, (1, LANES // A_CMP_STRIDE))
        return jnp.tile(t, (groups, 1))
    S = A_CMP_STRIDE
    lo = jnp.concatenate([table(wk[:S]), table(wv[:S])], axis=0)
    hi = jnp.concatenate([table(wk[S:]), table(wv[S:])], axis=0)
    return lo, hi


def _chunk_sum_matrix(first_col):
    shift = A_CMP_STRIDE.bit_length() - 1
    return _onehot((_iota((LANES, LANES), 0) >> shift) + first_col == _iota((LANES, LANES), 1))


def _cmp_kernel(x_ref, wlo_ref, whi_ref, o_ref, *, T):
    rows = x_ref.shape[1]
    per = LANES // A_CMP_STRIDE
    lo = jnp.zeros((rows, LANES), F32)
    hi = jnp.zeros((rows, LANES), F32)
    for c in range(T // LANES):
        xc = x_ref[0, :, c * LANES:(c + 1) * LANES]
        m = _chunk_sum_matrix(c * per)
        lo = lo + _dotx(xc * wlo_ref[...], m, 3)
        hi = hi + _dotx(xc * whi_ref[...], m, 3)
    o_ref[0] = lo + pltpu.roll(hi, LANES - 1, 1)


def _nsa_compress_prompt(cmp_t, wlo, whi):
    B, rows, T = cmp_t.shape
    assert T // A_CMP_STRIDE <= LANES
    return pl.pallas_call(
        functools.partial(_cmp_kernel, T=T), grid=(B,),
        in_specs=[pl.BlockSpec((1, rows, T), lambda b: (b, 0, 0)), pl.BlockSpec((rows, LANES), lambda b: (0, 0)),
                  pl.BlockSpec((rows, LANES), lambda b: (0, 0))],
        out_specs=pl.BlockSpec((1, rows, LANES), lambda b: (b, 0, 0)),
        out_shape=jax.ShapeDtypeStruct((B, rows, LANES), F32),
        compiler_params=_cp("parallel"), name="nsa_compress",
    )(cmp_t, wlo, whi)


def _nsa_prompt_kernel(q_ref, qr_ref, kvb_ref, sel_ref, win_ref, gl_ref, o_ref, *, tq, tk, T, flash_passes):
    i = pl.program_id(1)
    t0 = i * tq
    R = 4
    nb = kvb_ref.shape[2]
    ncb = T // A_CMP_STRIDE - 1
    nsb = T // A_SEL_LEN
    gw = A_KV * HEAD_DIM
    gates = _sigmoid(gl_ref[0])

    n_io = _iota((R * tq, nb), 1)
    t_c = (_iota((R * tq, nb), 0) & (tq - 1)) + t0
    cmask = (n_io * A_CMP_STRIDE + (A_CMP_LEN - 1) <= t_c) & (n_io < ncb)

    ratio = A_SEL_LEN // A_CMP_STRIDE
    back = A_CMP_LEN // A_CMP_STRIDE - 1
    mn, mj = _iota((nb, LANES), 0), _iota((nb, LANES), 1)
    imp_mat = _onehot((mn >= mj * ratio - back) & (mn <= mj * ratio + ratio - 1) & (mn < ncb) & (mj < nsb))

    j_io = _iota((tq, LANES), 1)
    qb = (_iota((tq, LANES), 0) + t0) >> 6
    allowed = (j_io <= qb) & (j_io < nsb)
    forced = (j_io == 0) | (j_io == qb)
    nsr = -(-nsb // 8) * 8
    jt_io = _iota((nsr, tq), 0)
    allowed_t = (jt_io <= ((_iota((nsr, tq), 1) + t0) >> 6)) & (jt_io < nsb)

    dist = (_iota((R * tq, tk), 0) & (tq - 1)) + t0 - _iota((R * tq, tk), 1)
    c_hi = (t0 + tq + tk - 1) // tk
    w_lo = jnp.maximum(t0 - (A_WINDOW - 1), 0) // tk

    def win_mask(c, s):
        return jnp.where(((dist - c * tk) & ~(A_WINDOW - 1)) == 0, s, NEG_INF)

    def group_front(g):
        q4 = _stack_heads(q_ref, R * g, R) * QK_SCALE
        kb = kvb_ref[0, g * HEAD_DIM:(g + 1) * HEAD_DIM, :]
        vb = kvb_ref[0, gw + g * HEAD_DIM:gw + (g + 1) * HEAD_DIM, :]
        s = jnp.where(cmask, _dot3(q4, kb, NN), NEG_INF)
        m = jnp.max(s, axis=-1, keepdims=True)
        e = jnp.where(cmask, jnp.exp(s - m), 0.0)
        p = e / jnp.maximum(jnp.sum(e, axis=-1, keepdims=True), TINY)
        o_cmp = _dot3(p, vb, NT)
        pg = p[0:tq] + p[tq:2 * tq] + p[2 * tq:3 * tq] + p[3 * tq:4 * tq]
        imp = _dotx(pg, imp_mat, 3)
        score = jnp.where(allowed, jnp.where(forced, A_FORCED, imp), -1.0)
        score = jnp.where(j_io < nsb, score, -2.0)
        score_t = score.T[0:nsr]
        rank = jnp.zeros((nsr, tq), F32)
        for j2 in range(nsb):
            other = score_t[j2:j2 + 1, :]
            rank = rank + jnp.where((other > score_t) | ((other == score_t) & (j2 < jt_io)), 1.0, 0.0)
        sel_t = jnp.where((rank < min(A_SEL_TOPK, nsb)) & allowed_t, 1.0, 0.0)
        sel = jnp.concatenate([sel_t, jnp.zeros((LANES - nsr, tq), F32)], axis=0).T.astype(BF16)
        sel4 = jnp.concatenate([sel] * R, axis=0)

        def sel_mask(c, s):
            blk = (_iota((LANES, tk), 1) + c * tk) >> 6
            hit = _dotf(sel4, _onehot(blk == _iota((LANES, tk), 0)))
            return jnp.where((hit > 0.5) & (dist >= c * tk), s, NEG_INF)

        return o_cmp, sel_mask

    for g0 in range(0, A_KV, 2):
        pair = (g0, g0 + 1)
        fronts = [group_front(g) for g in pair]
        q4rs = [_stack_heads(qr_ref, R * g, R) * QK_SCALE for g in pair]
        sel_res = _flash_streams(
            [(q4r, sel_ref, sel_ref, g * HEAD_DIM, gw + g * HEAD_DIM, front[1], None) for g, q4r, front in zip(pair, q4rs, fronts)],
            0, c_hi, tk, flash_passes)
        win_res = _flash_streams(
            [(q4r, win_ref, win_ref, g * HEAD_DIM, gw + g * HEAD_DIM, win_mask, None) for g, q4r in zip(pair, q4rs)],
            w_lo, c_hi, tk, flash_passes)
        for g, (o_cmp, _), (o_sel, _, _), (o_win, _, _) in zip(pair, fronts, sel_res, win_res):
            outs = []
            for r in range(R):
                h = R * g + r
                rows = slice(r * tq, (r + 1) * tq)
                outs.append(gates[:, 3 * h:3 * h + 1] * o_cmp[rows] + gates[:, 3 * h + 1:3 * h + 2] * o_sel[rows]
                            + gates[:, 3 * h + 2:3 * h + 3] * o_win[rows])
            o_ref[0, :, g * R * HEAD_DIM:(g + 1) * R * HEAD_DIM] = jnp.concatenate(outs, axis=1)


def _nsa_prompt(q3, qr3, kvb, sel_t, win_t, gl3, flash_passes, tq=256, tk=512):
    B, T, D = q3.shape
    full = lambda a: pl.BlockSpec((1,) + a.shape[1:], lambda b, i: (b, 0, 0))
    return pl.pallas_call(
        functools.partial(_nsa_prompt_kernel, tq=tq, tk=tk, T=T, flash_passes=flash_passes), grid=(B, T // tq),
        in_specs=[pl.BlockSpec((1, tq, D), lambda b, i: (b, i, 0)), pl.BlockSpec((1, tq, D), lambda b, i: (b, i, 0)),
                  full(kvb), full(sel_t), full(win_t), pl.BlockSpec((1, tq, LANES), lambda b, i: (b, i, 0))],
        out_specs=pl.BlockSpec((1, tq, D), lambda b, i: (b, i, 0)),
        out_shape=jax.ShapeDtypeStruct((B, T, D), F32),
        compiler_params=_cp("parallel", "arbitrary"), name="nsa_prompt",
    )(q3, qr3, kvb, sel_t, win_t, gl3)


def _dsa_prompt_kernel(q_ref, qi_ref, wi_ref, kv_ref, ki_ref, o_ref, ch_ref, *, tq, tk, T, topk, passes, idx_passes):
    i = pl.program_id(1)
    t0 = i * tq
    R = 4
    gw = kv_ref.shape[1] // 2
    c_hi = (t0 + tq + tk - 1) // tk
    assert tk >= topk

    def select_keys(width):
        wi = wi_ref[0] * (B_IDX_HEADS * B_IDX_DIM) ** -0.5
        ki = ki_ref[0, :, 0:width]
        score = jnp.zeros((tq, width), F32)
        for h in range(B_IDX_HEADS):
            logits = _dotp(qi_ref[0, :, h * B_IDX_DIM:(h + 1) * B_IDX_DIM], ki, NN, idx_passes)
            score = score + wi[:, h:h + 1] * jnp.maximum(logits, 0.0)
        causal = _iota((tq, width), 1) <= _iota((tq, width), 0) + t0
        score = jnp.where(causal, score, NEG_INF)
        ch_ref[:, 0:width] = jnp.where(_topk_mask(_sortable_key(score), topk) & causal, 0.0, NEG_INF)

    for nc in range(1, T // tk + 1):
        pl.when(c_hi == nc)(functools.partial(select_keys, nc * tk))

    def mask_fn(c, s):
        picked = ch_ref[:, pl.ds(pl.multiple_of(c * tk, tk), tk)]
        return s + jnp.concatenate([picked] * R, axis=0)

    for g0 in range(0, gw // HEAD_DIM, 2):
        streams = [(_stack_heads(q_ref, R * g, R) * QK_SCALE, kv_ref, kv_ref, g * HEAD_DIM, gw + g * HEAD_DIM, mask_fn, None)
                   for g in (g0, g0 + 1)]
        for g, (o, _, _) in zip((g0, g0 + 1), _flash_streams(streams, 0, c_hi, tk, passes)):
            o_ref[0, :, g * R * HEAD_DIM:(g + 1) * R * HEAD_DIM] = jnp.concatenate(
                [o[r * tq:(r + 1) * tq] for r in range(R)], axis=1)


def _dsa_prompt(q3, qi3, wi3, kv_t, ki_t, topk, passes, idx_passes, tq=256, tk=512):
    B, T, D = q3.shape
    full = lambda a: pl.BlockSpec((1,) + a.shape[1:], lambda b, i: (b, 0, 0))
    blk = lambda w: pl.BlockSpec((1, tq, w), lambda b, i: (b, i, 0))
    return pl.pallas_call(
        functools.partial(_dsa_prompt_kernel, tq=tq, tk=tk, T=T, topk=topk, passes=passes, idx_passes=idx_passes),
        grid=(B, T // tq),
        in_specs=[blk(D), blk(qi3.shape[2]), blk(LANES), full(kv_t), full(ki_t)],
        out_specs=blk(D), out_shape=jax.ShapeDtypeStruct((B, T, D), F32),
        scratch_shapes=[pltpu.VMEM((tq, T), F32)],
        compiler_params=_cp("parallel", "arbitrary"), name="dsa_prompt",
    )(q3, qi3, wi3, kv_t, ki_t)


def _dil_prompt_kernel(*refs, tq, tk, T, passes):
    ng = len(C_GROUPS)
    q_refs, k_refs, v_refs, o_ref = refs[0:ng], refs[ng:2 * ng], refs[2 * ng:3 * ng], refs[3 * ng]
    i = pl.program_id(2)
    t0 = i * tq
    dist = _iota((tq, tk), 0) + t0 - _iota((tq, tk), 1)
    c_hi = (t0 + tq + tk - 1) // tk
    nhl = LANES // HEAD_DIM
    per_group = []
    for (w, r), q_ref, k_ref, v_ref in zip(C_GROUPS, q_refs, k_refs, v_refs):
        def mask_fn(c, s, w=w, r=r):
            d = dist - c * tk
            ok = (d & (INT_MIN | (r - 1))) == 0
            if w < T - 1:
                ok = ok & (d <= w)
            return jnp.where(ok, s, NEG_INF)

        c_lo = jnp.maximum(t0 - w, 0) // tk
        streams = [(q_ref[0, :, hh * HEAD_DIM:(hh + 1) * HEAD_DIM] * QK_SCALE, k_ref, v_ref, hh * HEAD_DIM, hh * HEAD_DIM,
                    mask_fn, None) for hh in range(nhl)]
        per_group.append(_flash_streams(streams, c_lo, c_hi, tk, passes))
    outs = []
    for hh in range(nhl):
        res = [grp[hh] for grp in per_group]
        m_all = functools.reduce(jnp.maximum, [m for _, m, _ in res])
        wts = [den * jnp.exp(m - m_all) for _, m, den in res]
        tot = functools.reduce(lambda a, b: a + b, wts)
        outs.append(functools.reduce(lambda a, b: a + b, [(wt / tot) * o for wt, (o, _, _) in zip(wts, res)]))
    o_ref[0] = jnp.concatenate(outs, axis=1)


def _dil_prompt(qs, kvs_t, passes, tq=512, tk=512):
    B, T, W = qs[0].shape
    hp = W // LANES
    qspec = pl.BlockSpec((1, tq, LANES), lambda b, h, i: (b, i, h))
    kspec = pl.BlockSpec((1, LANES, T), lambda b, h, i: (b, h, 0))
    vspec = pl.BlockSpec((1, LANES, T), lambda b, h, i: (b, hp + h, 0))
    ng = len(C_GROUPS)
    return pl.pallas_call(
        functools.partial(_dil_prompt_kernel, tq=tq, tk=tk, T=T, passes=passes), grid=(B, hp, T // tq),
        in_specs=[qspec] * ng + [kspec] * ng + [vspec] * ng,
        out_specs=qspec, out_shape=jax.ShapeDtypeStruct((B, T, W), F32),
        compiler_params=_cp("parallel", "parallel", "arbitrary"), name="dilated_prompt",
    )(*qs, *kvs_t, *kvs_t)


def _logf_kernel(f_ref, b_ref, lf_ref, c_ref, *, T, tc):
    x = f_ref[0] + b_ref[...]
    lf = jnp.minimum(x, 0.0) - jnp.log(1.0 + jnp.exp(-jnp.abs(x)))
    lf_ref[0] = lf
    parts = _split(lf, 3)
    for c in range(T // tc):
        upto = _onehot(_iota((T, tc), 0) <= _iota((T, tc), 1) + c * tc)
        c_ref[0, :, c * tc:(c + 1) * tc] = functools.reduce(lambda a, b: a + b, [_dotf(p, upto) for p in parts])


def _fox_logf(f_t, bias_col):
    B, H, T = f_t.shape
    spec = pl.BlockSpec((1, H, T), lambda b: (b, 0, 0))
    return pl.pallas_call(
        functools.partial(_logf_kernel, T=T, tc=min(256, T)), grid=(B,),
        in_specs=[spec, pl.BlockSpec((H, 1), lambda b: (0, 0))],
        out_specs=[spec, spec], out_shape=[jax.ShapeDtypeStruct((B, H, T), F32)] * 2,
        compiler_params=_cp("parallel"), name="fox_logf_cumsum",
    )(f_t, bias_col)


def _fox_prompt_kernel(q_ref, k_ref, v_ref, cc_ref, cr_ref, o_ref, *, tq, tk, T, passes):
    hp = pl.program_id(1)
    i = pl.program_id(2)
    t0 = i * tq
    dist = _iota((tq, tk), 0) + t0 - _iota((tq, tk), 1)
    c_hi = (t0 + tq + tk - 1) // tk
    c_free = (t0 + 1) // tk
    nh = cc_ref.shape[2]

    def mask_fn(c, s):
        return jnp.where(dist >= c * tk, s, NEG_INF)

    streams = []
    for hh in range(LANES // HEAD_DIM):
        h = hp * (LANES // HEAD_DIM) + hh
        lane = hh * HEAD_DIM
        c_col = jnp.sum(jnp.where(_iota((tq, nh), 1) == h, cc_ref[0], 0.0), axis=-1, keepdims=True)

        def bias_fn(c, h=h, c_col=c_col):
            off = pl.multiple_of(c * tk, tk)
            return c_col - cr_ref[0, pl.ds(h, 1), pl.ds(off, tk)]

        streams.append((q_ref[0, :, lane:lane + HEAD_DIM] * QK_SCALE, k_ref, v_ref, lane, lane, mask_fn, bias_fn))
    res = _flash_streams(streams, 0, c_hi, tk, passes, c_free)
    o_ref[0] = jnp.concatenate([o for o, _, _ in res], axis=1)


def _fox_prompt(q3, kv_t, c_col, c_row, passes, tq=512, tk=512):
    B, T, W = q3.shape
    hp = W // LANES
    nh = c_row.shape[1]
    return pl.pallas_call(
        functools.partial(_fox_prompt_kernel, tq=tq, tk=tk, T=T, passes=passes), grid=(B, hp, T // tq),
        in_specs=[pl.BlockSpec((1, tq, LANES), lambda b, h, i: (b, i, h)),
                  pl.BlockSpec((1, LANES, T), lambda b, h, i: (b, h, 0)),
                  pl.BlockSpec((1, LANES, T), lambda b, h, i: (b, hp + h, 0)),
                  pl.BlockSpec((1, tq, nh), lambda b, h, i: (b, i, 0)),
                  pl.BlockSpec((1, nh, T), lambda b, h, i: (b, 0, 0))],
        out_specs=pl.BlockSpec((1, tq, LANES), lambda b, h, i: (b, i, h)),
        out_shape=jax.ShapeDtypeStruct((B, T, W), F32),
        compiler_params=_cp("parallel", "parallel", "arbitrary"), name="fox_prompt",
    )(q3, kv_t, kv_t, c_col, c_row)


def _page_specs(n, block, layer, index_fn, lead=()):
    specs = []
    for pi in range(n):
        def imap(b, s, *pf, pi=pi):
            return (layer, index_fn(b, s, pi, *pf)) + lead + (0,) * (len(block) - 2 - len(lead))
        specs.append(pl.BlockSpec(block, imap))
    return specs


def _tdec_init(m_ref, l_ref, acc_ref):
    m_ref[...] = jnp.full(m_ref.shape, NEG_INF, F32)
    l_ref[...] = jnp.zeros(l_ref.shape, F32)
    acc_ref[...] = jnp.zeros(acc_ref.shape, F32)


def _tdec_update(kt, vt, qexp, bias, valid, m_ref, l_ref, acc_ref, r):
    G = kt.shape[0] // HEAD_DIM
    s = jnp.sum((kt * qexp).reshape(G, HEAD_DIM, LANES), axis=1)
    if bias is not None:
        s = s + bias
    if valid is not None:
        s = jnp.where(valid, s, NEG_INF)
    m_old = m_ref[r]
    m_new = jnp.maximum(m_old, jnp.max(s, axis=-1, keepdims=True))
    alpha = jnp.exp(m_old - m_new)
    e = jnp.exp(s - m_new)
    if valid is not None:
        e = jnp.where(valid, e, 0.0)
    l_ref[r] = alpha * l_ref[r] + e
    acc_ref[r] = acc_ref[r] * alpha[:, :, None] + vt.reshape(G, HEAD_DIM, LANES) * e[:, None, :]
    m_ref[r] = m_new


def _tdec_finish(l_ref, acc_ref, r):
    den = jnp.maximum(jnp.sum(l_ref[r], axis=-1, keepdims=True), TINY)
    o = jnp.sum(acc_ref[r], axis=-1, keepdims=True) / den[:, :, None]
    return jnp.broadcast_to(o, acc_ref.shape[1:])


def _tdec_scratch(R, G):
    return [pltpu.VMEM((R, G, 1), F32), pltpu.VMEM((R, G, LANES), F32), pltpu.VMEM((R, G, HEAD_DIM, LANES), F32)]


def _scmp_kernel(pt_ref, *refs, pp):
    pages, wlo_ref, whi_ref, o_ref = refs[:pp], refs[pp], refs[pp + 1], refs[pp + 2]
    rows = wlo_ref.shape[0]
    per = PAGE // A_CMP_STRIDE
    lo = jnp.zeros((rows, LANES), F32)
    hi = jnp.zeros((rows, LANES), F32)
    for pi in range(pp):
        x = pages[pi][...]
        m = _chunk_sum_matrix(pi * per)
        lo = lo + _dotx(x * wlo_ref[...], m, 2)
        hi = hi + _dotx(x * whi_ref[...], m, 2)
    o_ref[0, 0:rows, :] = lo
    o_ref[0, rows:2 * rows, :] = hi


def _nsa_compress_sample(cache_t, layer, page_table, wlo, whi):
    Bd, NP = page_table.shape
    rows = cache_t.shape[2]
    pp = LANES * A_CMP_STRIDE // PAGE
    grid_spec = pltpu.PrefetchScalarGridSpec(
        num_scalar_prefetch=1, grid=(Bd, NP // pp),
        in_specs=_page_specs(pp, (None, None, rows, PAGE), layer, lambda b, s, pi, pt: pt[b, s * pp + pi])
        + [pl.BlockSpec((rows, LANES), lambda b, s, pt: (0, 0))] * 2,
        out_specs=pl.BlockSpec((1, 2 * rows, LANES), lambda b, s, pt: (b, 0, s)))
    return pl.pallas_call(
        functools.partial(_scmp_kernel, pp=pp), grid_spec=grid_spec,
        out_shape=jax.ShapeDtypeStruct((Bd, 2 * rows, NP * PAGE // A_CMP_STRIDE), F32),
        compiler_params=_cp("parallel", "arbitrary"), name="nsa_compress_paged",
    )(page_table, *([cache_t] * pp), wlo, whi)


def _nsa_sample_cmp_kernel(lohi_ref, qt_ref, o_ref, idx_ref, *, P, ncols):
    rows = lohi_ref.shape[1] // 2
    gw = rows // 2
    nch = lohi_ref.shape[2]
    H = qt_ref.shape[1]
    R = H // A_KV
    lpad = -(-(P + 1) // A_SEL_LEN) * A_SEL_LEN
    ncb = lpad // A_CMP_STRIDE - 1
    nsb = lpad // A_SEL_LEN
    qb = P // A_SEL_LEN
    lohi = lohi_ref[0]
    kvb = lohi[0:rows] + pltpu.roll(lohi[rows:2 * rows], nch - 1, 1)
    qbd, keep = _group_block_diag(qt_ref[0] * QK_SCALE, H, A_KV)
    n_io = _iota((H, nch), 1)
    valid = (n_io * A_CMP_STRIDE + (A_CMP_LEN - 1) <= P) & (n_io < ncb)
    s = jnp.where(valid, _dot3(qbd, kvb[0:gw], NN), NEG_INF)
    m = jnp.max(s, axis=-1, keepdims=True)
    e = jnp.where(valid, jnp.exp(s - m), 0.0)
    p = e / jnp.maximum(jnp.sum(e, axis=-1, keepdims=True), TINY)
    o_ref[0] = _fold_groups(_dot3(p, kvb[gw:rows], NT), keep, A_KV)
    ratio = A_SEL_LEN // A_CMP_STRIDE
    back = A_CMP_LEN // A_CMP_STRIDE - 1
    mn, mj = _iota((nch, ncols), 0), _iota((nch, ncols), 1)
    imp_mat = _onehot((mn >= mj * ratio - back) & (mn <= mj * ratio + ratio - 1) & (mn < ncb))
    j_io = _iota((8, ncols), 1)
    jf = j_io.astype(F32)
    lane = _iota((8, LANES), 1)
    row = _iota((8, LANES), 0)
    out = jnp.zeros((8, LANES), F32)
    for g in range(A_KV):
        pg = jnp.sum(p[R * g:R * (g + 1)], axis=0, keepdims=True)
        imp = _dotx(jnp.broadcast_to(pg, (8, nch)), imp_mat, 3)
        score = jnp.where(j_io <= qb, jnp.where((j_io == 0) | (j_io == qb), A_FORCED, imp), -1.0)
        score = jnp.where(j_io < nsb, score, -2.0)
        for k in range(min(A_SEL_TOPK, nsb)):
            best = jnp.max(score, axis=-1, keepdims=True)
            pick = jnp.min(jnp.where(score == best, jf, 1e9), axis=-1, keepdims=True)
            out = jnp.where((lane == k) & (row == g), pick, out)
            score = jnp.where(jf == pick, -3.0, score)
    idx_ref[0] = out.astype(I32)


def _nsa_sample_cmp(lohi, qt3, P):
    Bd, rows2, nch = lohi.shape
    H, gw = qt3.shape[1], qt3.shape[2]
    nsb = -(-(P + 1) // A_SEL_LEN)
    ncols = -(-nsb // LANES) * LANES
    return pl.pallas_call(
        functools.partial(_nsa_sample_cmp_kernel, P=P, ncols=ncols), grid=(Bd,),
        in_specs=[pl.BlockSpec((1, rows2, nch), lambda b: (b, 0, 0)), pl.BlockSpec((1, H, gw), lambda b: (b, 0, 0))],
        out_specs=[pl.BlockSpec((1, H, HEAD_DIM), lambda b: (b, 0, 0)), pl.BlockSpec((1, 8, LANES), lambda b: (b, 0, 0))],
        out_shape=[jax.ShapeDtypeStruct((Bd, H, HEAD_DIM), F32), jax.ShapeDtypeStruct((Bd, 8, LANES), I32)],
        compiler_params=_cp("parallel"), name="nsa_sample_cmp_select",
    )(lohi, qt3)


def _nsa_sample_sel_kernel(pt_ref, ix_ref, *refs, P, nk):
    kblks, vblks = refs[0:A_KV], refs[A_KV:2 * A_KV]
    q_ref, new_ref, o_ref, m_ref, l_ref, acc_ref = refs[2 * A_KV:]
    b = pl.program_id(0)
    k = pl.program_id(1)
    R = q_ref.shape[2] // HEAD_DIM
    qb = P // A_SEL_LEN
    half_shift = A_SEL_LEN.bit_length() - 1
    lane = _iota((1, PAGE), 1)

    @pl.when(k == 0)
    def _():
        _tdec_init(m_ref, l_ref, acc_ref)

    for g in range(A_KV):
        j = ix_ref[b, g * nk + k]
        valid = ((lane >> half_shift) == (j & (PAGE // A_SEL_LEN - 1))) & (j < qb)
        kt = jnp.concatenate([kblks[g][...]] * R, axis=0)
        vt = jnp.concatenate([vblks[g][...]] * R, axis=0)
        _tdec_update(kt, vt, q_ref[0, g] * QK_SCALE, None, valid, m_ref, l_ref, acc_ref, g)

    @pl.when(k == nk - 1)
    def _():
        for g in range(A_KV):
            hit = ix_ref[b, g * nk] == qb
            for kk in range(1, nk):
                hit = hit | (ix_ref[b, g * nk + kk] == qb)
            kt = jnp.concatenate([new_ref[0, 0, g]] * R, axis=0)
            vt = jnp.concatenate([new_ref[0, 1, g]] * R, axis=0)
            _tdec_update(kt, vt, q_ref[0, g] * QK_SCALE, None, (lane == 0) & hit, m_ref, l_ref, acc_ref, g)
            o_ref[0, g] = _tdec_finish(l_ref, acc_ref, g)


def _nsa_sample_sel(cache_t6, layer, page_table, idx_flat, qexp, new_cols, P):
    Bd, G, rw = qexp.shape[0], qexp.shape[1], qexp.shape[2]
    R = rw // HEAD_DIM
    nk = idx_flat.shape[1] // A_KV
    per = PAGE // A_SEL_LEN
    last = P // A_SEL_LEN - 1

    def blk_index(c, g):
        def imap(b, k, pt, ix):
            j = jnp.minimum(ix[b, g * nk + k], last)
            return (layer, pt[b, j // per], c, g, 0, 0)
        return imap

    per_b = lambda shape: pl.BlockSpec((1,) + shape, lambda b, k, pt, ix: (b,) + (0,) * len(shape))
    blk = (None, None, None, None, HEAD_DIM, PAGE)
    grid_spec = pltpu.PrefetchScalarGridSpec(
        num_scalar_prefetch=2, grid=(Bd, nk),
        in_specs=[pl.BlockSpec(blk, blk_index(0, g)) for g in range(A_KV)]
        + [pl.BlockSpec(blk, blk_index(1, g)) for g in range(A_KV)]
        + [per_b((G, rw, LANES)), per_b((2, G, HEAD_DIM, LANES))],
        out_specs=per_b((G, R, HEAD_DIM, LANES)),
        scratch_shapes=_tdec_scratch(G, R))
    return pl.pallas_call(
        functools.partial(_nsa_sample_sel_kernel, P=P, nk=nk), grid_spec=grid_spec,
        out_shape=jax.ShapeDtypeStruct((Bd, G, R, HEAD_DIM, LANES), F32),
        compiler_params=_cp("parallel", "arbitrary"), name="nsa_sample_selected",
    )(page_table, idx_flat, *([cache_t6] * (2 * A_KV)), qexp, new_cols)


def _nsa_merge_kernel(gl_ref, ocmp_ref, osel_ref, owin_ref, o_ref):
    gates = _sigmoid(gl_ref[...])
    o_ref[...] = gates[:, :, 0:1] * ocmp_ref[...] + gates[:, :, 1:2] * osel_ref[...] + gates[:, :, 2:3] * owin_ref[...]


def _nsa_sample_merge(gl3, ocmp3, osel3, owin3):
    whole = lambda a: pl.BlockSpec(a.shape, lambda i: (0,) * a.ndim)
    return pl.pallas_call(
        _nsa_merge_kernel, grid=(1,),
        in_specs=[whole(gl3), whole(ocmp3), whole(osel3), whole(owin3)], out_specs=whole(ocmp3),
        out_shape=jax.ShapeDtypeStruct(ocmp3.shape, F32),
        compiler_params=_cp("arbitrary"), name="nsa_sample_merge",
    )(gl3, ocmp3, osel3, owin3)


def _slide_lanes(buf, new_cols):
    W = buf.shape[-1]
    axis = buf.ndim - 1
    new = jnp.concatenate([new_cols] * (W // LANES), axis=axis)
    return jnp.where(_iota(buf.shape, axis) == W - 1, new, pltpu.roll(buf, W - 1, axis))


def _nsa_sample_win_kernel(buf_ref, new_ref, qt_ref, nbuf_ref, o_ref):
    rows = buf_ref.shape[1]
    gw = rows // 2
    H = qt_ref.shape[1]
    nb = _slide_lanes(buf_ref[0], new_ref[0])
    nbuf_ref[0] = nb
    qbd, keep = _group_block_diag(qt_ref[0] * QK_SCALE, H, A_KV)
    s = _dot3(qbd, nb[0:gw], NN)
    e = jnp.exp(s - jnp.max(s, axis=-1, keepdims=True))
    p = e / jnp.maximum(jnp.sum(e, axis=-1, keepdims=True), TINY)
    o_ref[0] = _fold_groups(_dot3(p, nb[gw:rows], NT), keep, A_KV)


def _nsa_sample_win(buf_t, new_cols, qt3):
    Bd, rows, W = buf_t.shape
    H, gw = qt3.shape[1], qt3.shape[2]
    return pl.pallas_call(
        _nsa_sample_win_kernel, grid=(Bd,),
        in_specs=[pl.BlockSpec((1, rows, W), lambda b: (b, 0, 0)), pl.BlockSpec((1, rows, LANES), lambda b: (b, 0, 0)),
                  pl.BlockSpec((1, H, gw), lambda b: (b, 0, 0))],
        out_specs=[pl.BlockSpec((1, rows, W), lambda b: (b, 0, 0)), pl.BlockSpec((1, H, HEAD_DIM), lambda b: (b, 0, 0))],
        out_shape=[jax.ShapeDtypeStruct((Bd, rows, W), F32), jax.ShapeDtypeStruct((Bd, H, HEAD_DIM), F32)],
        compiler_params=_cp("parallel"), name="nsa_sample_window",
    )(buf_t, new_cols, qt3)


def _dsa_idx_kernel(pt_ref, *refs, pp):
    pages, qi_ref, wi_ref, o_ref = refs[:pp], refs[pp], refs[pp + 1], refs[pp + 2]
    wi = wi_ref[0] * (B_IDX_HEADS * B_IDX_DIM) ** -0.5
    qi = qi_ref[0]
    for pi in range(pp):
        logits = _dot3(qi, pages[pi][...], NN)
        o_ref[0, pi] = jnp.sum(wi * jnp.maximum(logits, 0.0), axis=0, keepdims=True)


def _dsa_sample_scores(kidx_t, layer, page_table, qi3, wi3, pp=8):
    Bd, NP = page_table.shape
    grid_spec = pltpu.PrefetchScalarGridSpec(
        num_scalar_prefetch=1, grid=(Bd, NP // pp),
        in_specs=_page_specs(pp, (None, None, B_IDX_DIM, PAGE), layer, lambda b, s, pi, pt: pt[b, s * pp + pi])
        + [pl.BlockSpec((1, B_IDX_HEADS, B_IDX_DIM), lambda b, s, pt: (b, 0, 0)),
           pl.BlockSpec((1, B_IDX_HEADS, 1), lambda b, s, pt: (b, 0, 0))],
        out_specs=pl.BlockSpec((1, pp, 1, PAGE), lambda b, s, pt: (b, s, 0, 0)))
    return pl.pallas_call(
        functools.partial(_dsa_idx_kernel, pp=pp), grid_spec=grid_spec,
        out_shape=jax.ShapeDtypeStruct((Bd, NP, 1, PAGE), F32),
        compiler_params=_cp("parallel", "arbitrary"), name="dsa_sample_indexer",
    )(page_table, *([kidx_t] * pp), qi3, wi3)


def _dsa_select_kernel(sc_ref, qi_ref, wi_ref, kin_ref, o_ref, *, topk):
    Bd, P = sc_ref.shape
    wi = wi_ref[...] * (B_IDX_HEADS * B_IDX_DIM) ** -0.5
    logit = jnp.sum(qi_ref[...] * kin_ref[...], axis=-1, keepdims=True)
    s_new = jnp.sum(wi * jnp.maximum(logit, 0.0), axis=1)
    tail = jnp.where(_iota((Bd, LANES), 1) == 0, s_new, -jnp.inf)
    full = jnp.concatenate([sc_ref[...], tail], axis=1)
    o_ref[...] = jnp.where(_topk_mask(_sortable_key(full), topk), 1.0, 0.0)


def _dsa_sample_select(scores2, qi3, wi3, kinew3, topk):
    Bd, P = scores2.shape
    whole = lambda shape: pl.BlockSpec(shape, lambda i: (0,) * len(shape))
    return pl.pallas_call(
        functools.partial(_dsa_select_kernel, topk=topk), grid=(1,),
        in_specs=[whole(scores2.shape), whole(qi3.shape), whole(wi3.shape), whole(kinew3.shape)],
        out_specs=whole((Bd, P + LANES)), out_shape=jax.ShapeDtypeStruct((Bd, P + LANES), F32),
        compiler_params=_cp("arbitrary"), name="dsa_sample_topk",
    )(scores2, qi3, wi3, kinew3)


def _dsa_sample_attn_kernel(pt_ref, *refs, pp, nsteps):
    pages = refs[:pp]
    mask_ref, q_ref, new_ref, mnew_ref, o_ref, m_ref, l_ref, acc_ref = refs[pp:]
    s_id = pl.program_id(1)
    R = q_ref.shape[1]
    gw = new_ref.shape[1] // 2

    @pl.when(s_id == 0)
    def _():
        _tdec_init(m_ref, l_ref, acc_ref)

    for pi in range(pp):
        kv = pages[pi][...]
        valid = mask_ref[0, pi] > 0.5
        for r in range(R):
            _tdec_update(kv[0:gw], kv[gw:2 * gw], q_ref[0, r] * QK_SCALE, None, valid, m_ref, l_ref, acc_ref, r)

    @pl.when(s_id == nsteps - 1)
    def _():
        new = new_ref[0]
        valid = (_iota((1, LANES), 1) == 0) & (mnew_ref[0] > 0.5)
        for r in range(R):
            _tdec_update(new[0:gw], new[gw:2 * gw], q_ref[0, r] * QK_SCALE, None, valid, m_ref, l_ref, acc_ref, r)
            o_ref[0, r] = _tdec_finish(l_ref, acc_ref, r)


def _dsa_sample_attn(kv_t, layer, page_table, mask4, qexp, new_cols, mnew3, pp=16):
    Bd, NP = page_table.shape
    rows = kv_t.shape[2]
    R, gw = qexp.shape[1], qexp.shape[2]
    G = gw // HEAD_DIM
    nsteps = NP // pp
    per_b = lambda shape: pl.BlockSpec((1,) + shape, lambda b, s, pt: (b,) + (0,) * len(shape))
    grid_spec = pltpu.PrefetchScalarGridSpec(
        num_scalar_prefetch=1, grid=(Bd, nsteps),
        in_specs=_page_specs(pp, (None, None, rows, PAGE), layer, lambda b, s, pi, pt: pt[b, s * pp + pi])
        + [pl.BlockSpec((1, pp, 1, PAGE), lambda b, s, pt: (b, s, 0, 0)), per_b((R, gw, LANES)), per_b((rows, LANES)),
           per_b((1, LANES))],
        out_specs=per_b((R, G, HEAD_DIM, LANES)),
        scratch_shapes=_tdec_scratch(R, G))
    return pl.pallas_call(
        functools.partial(_dsa_sample_attn_kernel, pp=pp, nsteps=nsteps), grid_spec=grid_spec,
        out_shape=jax.ShapeDtypeStruct((Bd, R, G, HEAD_DIM, LANES), F32),
        compiler_params=_cp("parallel", "arbitrary"), name="dsa_sample_attention",
    )(page_table, *([kv_t] * pp), mask4, qexp, new_cols, mnew3)


def _dil_sample_kernel(*refs, P):
    ng = len(C_GROUPS)
    buf_refs, new_refs, q_refs = refs[0:ng], refs[ng:2 * ng], refs[2 * ng:3 * ng]
    nbuf_refs, o_ref = refs[3 * ng:4 * ng], refs[4 * ng]
    res = []
    for (w, r), buf_ref, new_ref, q_ref, nbuf_ref in zip(C_GROUPS, buf_refs, new_refs, q_refs, nbuf_refs):
        buf = buf_ref[...]
        new = new_ref[...]
        W = buf.shape[2]
        nbuf_ref[...] = _slide_lanes(buf, new)
        q = q_ref[...] * QK_SCALE
        qw = jnp.concatenate([q] * (W // LANES), axis=1)
        s_old = jnp.sum(buf[0] * qw, axis=0, keepdims=True)
        dist = W - _iota((1, W), 1)
        valid = ((dist & (r - 1)) == 0) & (dist <= P)
        s_old = jnp.where(valid, s_old, NEG_INF)
        s_new = jnp.sum(new[0] * q, axis=0, keepdims=True)
        m = jnp.maximum(jnp.max(s_old, axis=-1, keepdims=True), s_new)
        e_old = jnp.where(valid, jnp.exp(s_old - m[:, 0:1]), 0.0)
        e_new = jnp.exp(s_new - m)
        den = jnp.sum(e_old, axis=-1, keepdims=True) + e_new
        o = (jnp.sum(buf[1] * e_old, axis=-1, keepdims=True) + new[1] * e_new) / jnp.maximum(den, TINY)
        res.append((o, m, den))
    m_all = functools.reduce(jnp.maximum, [m for _, m, _ in res])
    wts = [den * jnp.exp(m - m_all) for _, m, den in res]
    tot = functools.reduce(lambda a, b: a + b, wts)
    o_ref[...] = functools.reduce(lambda a, b: a + b, [(wt / tot) * o for wt, (o, _, _) in zip(wts, res)])


def _dil_sample(bufs_t, layer, news_cols, qs_cols, P):
    Bd, H = qs_cols[0].shape[0], qs_cols[0].shape[1]
    in_specs, out_specs, out_shape = [], [], []
    for buf in bufs_t:
        W = buf.shape[5]
        in_specs.append(pl.BlockSpec((None, None, 2, None, HEAD_DIM, W), lambda b, h: (layer, b, 0, h, 0, 0)))
        out_specs.append(pl.BlockSpec((None, 2, None, HEAD_DIM, W), lambda b, h: (b, 0, h, 0, 0)))
        out_shape.append(jax.ShapeDtypeStruct((Bd, 2, H, HEAD_DIM, W), F32))
    in_specs += [pl.BlockSpec((None, 2, None, HEAD_DIM, LANES), lambda b, h: (b, 0, h, 0, 0))] * len(bufs_t)
    in_specs += [pl.BlockSpec((None, None, HEAD_DIM, LANES), lambda b, h: (b, h, 0, 0))] * len(bufs_t)
    out_specs.append(pl.BlockSpec((None, None, HEAD_DIM, LANES), lambda b, h: (b, h, 0, 0)))
    out_shape.append(jax.ShapeDtypeStruct((Bd, H, HEAD_DIM, LANES), F32))
    return pl.pallas_call(
        functools.partial(_dil_sample_kernel, P=P), grid=(Bd, H),
        in_specs=in_specs, out_specs=out_specs, out_shape=out_shape,
        compiler_params=_cp("parallel", "parallel"), name="dilated_sample",
    )(*bufs_t, *news_cols, *qs_cols)


def _fox_sample_kernel(pt_ref, *refs, pp, nsteps):
    pages, lfs = refs[:pp], refs[pp:2 * pp]
    q_ref, new_ref, lfnew_ref, o_ref, m_ref, l_ref, acc_ref, carry_ref = refs[2 * pp:]
    s_id = pl.program_id(1)
    L = new_ref.shape[1] // 2
    q = q_ref[0, 0] * QK_SCALE

    @pl.when(s_id == 0)
    def _():
        _tdec_init(m_ref, l_ref, acc_ref)
        new = new_ref[0]
        _tdec_update(new[0:L], new[L:2 * L], q, None, _iota((1, LANES), 1) == 0, m_ref, l_ref, acc_ref, 0)
        carry_ref[...] = lfnew_ref[0]

    later = _onehot(_iota((PAGE, PAGE), 0) > _iota((PAGE, PAGE), 1))
    for pi in range(pp):
        kv = pages[pi][...]
        lf = lfs[pi][...]
        carry = carry_ref[...]
        bias = _dotx(lf, later, 3) + carry
        carry_ref[...] = carry + jnp.sum(lf, axis=-1, keepdims=True)
        _tdec_update(kv[0:L], kv[L:2 * L], q, bias, None, m_ref, l_ref, acc_ref, 0)

    @pl.when(s_id == nsteps - 1)
    def _():
        o_ref[0, 0] = _tdec_finish(l_ref, acc_ref, 0)


def _fox_sample(kv_t, lf_t, layer, page_table, qexp, new_cols, lfnew3, pp=8):
    Bd, NP = page_table.shape
    rows = kv_t.shape[2]
    H = lf_t.shape[2]
    nsteps = NP // pp
    rev = lambda b, s, pi, pt: pt[b, NP - 1 - (s * pp + pi)]
    per_b = lambda shape: pl.BlockSpec((1,) + shape, lambda b, s, pt: (b,) + (0,) * len(shape))
    grid_spec = pltpu.PrefetchScalarGridSpec(
        num_scalar_prefetch=1, grid=(Bd, nsteps),
        in_specs=_page_specs(pp, (None, None, rows, PAGE), layer, rev) + _page_specs(pp, (None, None, H, PAGE), layer, rev)
        + [per_b((1, rows // 2, LANES)), per_b((rows, LANES)), per_b((H, LANES))],
        out_specs=per_b((1, H, HEAD_DIM, LANES)),
        scratch_shapes=_tdec_scratch(1, H) + [pltpu.VMEM((H, LANES), F32)])
    return pl.pallas_call(
        functools.partial(_fox_sample_kernel, pp=pp, nsteps=nsteps), grid_spec=grid_spec,
        out_shape=jax.ShapeDtypeStruct((Bd, 1, H, HEAD_DIM, LANES), F32),
        compiler_params=_cp("parallel", "arbitrary"), name="fox_sample",
    )(page_table, *([kv_t] * pp), *([lf_t] * pp), qexp, new_cols, lfnew3)


def _cols(x, lead):
    Bd = x.shape[0]
    return jnp.broadcast_to(x.reshape((Bd,) + lead + (1,)), (Bd,) + lead + (LANES,))


def _rows_minor(cache):
    n = cache.ndim
    return jnp.moveaxis(cache, n - 4, n - 1)


def _rows_major(x):
    n = x.ndim
    return jnp.moveaxis(x, n - 1, n - 4)


def kernel(x_prompt, x_sample, cache_a_cmp, cache_a_sel, cache_a_win, cache_b_kv, cache_b_kidx, cache_c_win0, cache_c_win1, cache_c_win2, cache_d_kv, cache_d_logf, page_table, c_prompt, c_sample, ada_w, ada_b, norm_g, final_g, a_w_in, a_w_out, a_cmp_wk, a_cmp_wv, b_w_in, b_w_out, c_w_in, c_w_out, d_w_in, d_w_out, d_f_bias):
    B, T, D = x_prompt.shape
    Bd = x_sample.shape[0]
    depth = ada_w.shape[0]
    NP = page_table.shape[1]
    P = NP * PAGE
    pool = cache_b_kidx.shape[1]
    assert x_sample.shape[1] == 1 and D == 1024 and P % A_SEL_LEN == 0
    assert cache_a_win.shape[2] == A_WINDOW
    assert all(b.shape[2] == w for b, (w, _) in zip((cache_c_win0, cache_c_win1, cache_c_win2), C_GROUPS))
    H = D // HEAD_DIM
    R = H // A_KV
    tm_p = 256
    tpb = T // tm_p

    mod = _mod_all(jnp.concatenate([c_prompt, c_sample], axis=0), ada_w, ada_b)
    pos_p = jnp.arange(T, dtype=I32)
    pos_s = jnp.full((Bd,), P, I32)
    tab_p, ttab_p = _rope_tables(pos_p), _rope_tables_t(pos_p)
    tab_s = _rope_tables(pos_s)
    xp = x_prompt.reshape(B * T, D)
    xs = x_sample.reshape(Bd, D)
    c_bufs = (cache_c_win0, cache_c_win1, cache_c_win2)
    st = {}
    put = lambda name, val: st.setdefault(name, []).append(val)
    rows_of = lambda t, c, g: _rows_major(t.reshape(t.shape[0], c, g, HEAD_DIM, t.shape[2]))
    yp = ys = None
    for i in range(depth):
        kind, li = i % 4, i // 4
        shift_p, scale_p, gate_p = [mod[i, :B, j * D:(j + 1) * D].reshape(B, 1, D) for j in range(3)]
        shift_s, scale_s, gate_s = [mod[i, B:, j * D:(j + 1) * D].reshape(1, Bd, D) for j in range(3)]
        proj_p = lambda w, segs: _proj(xp, scale_p, shift_p, norm_g[i], w, segs, tab_p, ttab_p, tpb, tm_p, PASSES_PROMPT_PROJ)
        proj_s = lambda w, segs: _proj(xs, scale_s, shift_s, norm_g[i], w, segs, tab_s, None, 1, Bd, PASSES_SELECTIVE)
        if kind == 0:
            w = a_w_in[li]
            gw = A_KV * HEAD_DIM
            q, qr, gl, z, cmp_t, sel_t, win_t = proj_p(w, [
                ("rm", 0, D, ("n", "r")), ("rm", D + 6 * gw, 3 * H, ("n",)), ("rm", D + 6 * gw + 3 * H, D, ("n",)),
                ("t", D, 2 * gw, "t"), ("t", D + 2 * gw, 2 * gw, "trk"), ("t", D + 4 * gw, 2 * gw, "trk")])
            wlo, whi = _cmp_taps(a_cmp_wk[li], a_cmp_wv[li], A_KV)
            kvb = _nsa_compress_prompt(cmp_t, wlo, whi)
            o = _nsa_prompt(q.reshape(B, T, D), qr.reshape(B, T, D), kvb, sel_t, win_t, gl.reshape(B, T, LANES),
                            PASSES_NSA_FLASH)
            op, zp, wo = o.reshape(B * T, D), z, a_w_out[li]
            put("a_cmp_p", rows_of(cmp_t, 2, A_KV))
            put("a_sel_p", rows_of(sel_t, 2, A_KV))
            put("a_win_p", rows_of(win_t[:, :, T - min(A_WINDOW, T):], 2, A_KV))

            q, qr, cmp_, sel, win, gl, z = proj_s(w, [
                ("rm", 0, D, ("n", "r")), ("rm", D, 2 * gw, ("n",)), ("rm", D + 2 * gw, 2 * gw, ("rk",)),
                ("rm", D + 4 * gw, 2 * gw, ("rk",)), ("rm", D + 6 * gw, 3 * H, ("n",)), ("rm", D + 6 * gw + 3 * H, D, ("n",))])
            cmp_t6 = _rows_minor(cache_a_cmp)
            lohi = _nsa_compress_sample(cmp_t6.reshape(cmp_t6.shape[0], pool, 2 * gw, PAGE), li, page_table, wlo, whi)
            tile_g = lambda t: jnp.tile(t.reshape(Bd, H, HEAD_DIM), (1, 1, A_KV))
            ocmp, idx = _nsa_sample_cmp(lohi, tile_g(q), P)
            win_t6 = _rows_minor(cache_a_win)
            nwin, owin = _nsa_sample_win(win_t6[li].reshape(Bd, 2 * gw, A_WINDOW), _cols(win, (2 * gw,)), tile_g(qr))
            nk = min(A_SEL_TOPK, -(-(P + 1) // A_SEL_LEN))
            osel = _nsa_sample_sel(_rows_minor(cache_a_sel), li, page_table, idx[:, :A_KV, :nk].reshape(Bd, A_KV * nk),
                                   _cols(qr, (A_KV, R * HEAD_DIM)), _cols(sel, (2, A_KV, HEAD_DIM)), P)
            o = _nsa_sample_merge(gl[:, :3 * H].reshape(Bd, H, 3), ocmp, osel[..., 0].reshape(Bd, H, HEAD_DIM), owin)
            os_, zs = o.reshape(Bd, D), z
            put("a_cmp_s", cmp_.reshape(Bd, 1, 2, A_KV, HEAD_DIM))
            put("a_sel_s", sel.reshape(Bd, 1, 2, A_KV, HEAD_DIM))
            put("a_win_s", _rows_major(nwin.reshape(Bd, 2, A_KV, HEAD_DIM, A_WINDOW)))
        elif kind == 1:
            w = b_w_in[li]
            gw = A_KV * HEAD_DIM
            c_qi = D + 2 * gw
            c_ki = c_qi + B_IDX_HEADS * B_IDX_DIM
            c_wi = c_ki + B_IDX_DIM
            c_z = c_wi + B_IDX_HEADS
            q, qi, wi, z, kv_t, ki_t = proj_p(w, [
                ("rm", 0, D, ("r",)), ("rm", c_qi, c_ki - c_qi, ("r",)), ("rm", c_wi, B_IDX_HEADS, ("n",)), ("rm", c_z, D, ("n",)),
                ("t", D, 2 * gw, "trk"), ("t", c_ki, B_IDX_DIM, "tr")])
            o = _dsa_prompt(q.reshape(B, T, D), qi.reshape(B, T, c_ki - c_qi), wi.reshape(B, T, LANES), kv_t, ki_t,
                            min(B_TOPK_MAX, T // 4), PASSES_SMOOTH, PASSES_PROMPT_INDEXER)
            op, zp, wo = o.reshape(B * T, D), z, b_w_out[li]
            put("b_kv_p", rows_of(kv_t, 2, A_KV))
            put("b_kidx_p", jnp.swapaxes(ki_t, 1, 2))

            q, kv, qi, ki, wi, z = proj_s(w, [
                ("rm", 0, D, ("r",)), ("rm", D, 2 * gw, ("rk",)), ("rm", c_qi, c_ki - c_qi, ("r",)), ("rm", c_ki, B_IDX_DIM, ("r",)),
                ("rm", c_wi, B_IDX_HEADS, ("n",)), ("rm", c_z, D, ("n",))])
            qi3 = qi.reshape(Bd, B_IDX_HEADS, B_IDX_DIM)
            wi3 = wi[:, :B_IDX_HEADS].reshape(Bd, B_IDX_HEADS, 1)
            ki_new = ki[:, :B_IDX_DIM]
            sc = _dsa_sample_scores(jnp.swapaxes(cache_b_kidx, 2, 3), li, page_table, qi3, wi3)
            mask = _dsa_sample_select(sc.reshape(Bd, P), qi3, wi3, ki_new.reshape(Bd, 1, B_IDX_DIM),
                                      min(B_TOPK_MAX, (P + 1) // 4))
            kv_t6 = _rows_minor(cache_b_kv)
            qexp = _cols(q.reshape(Bd, A_KV, R, HEAD_DIM).transpose(0, 2, 1, 3).reshape(Bd, R, gw), (R, gw))
            o = _dsa_sample_attn(kv_t6.reshape(kv_t6.shape[0], pool, 2 * gw, PAGE), li, page_table,
                                 mask[:, :P].reshape(Bd, NP, 1, PAGE), qexp, _cols(kv, (2 * gw,)),
                                 jnp.broadcast_to(mask[:, P:P + 1], (Bd, LANES)).reshape(Bd, 1, LANES))
            os_, zs = o[..., 0].transpose(0, 2, 1, 3).reshape(Bd, D), z
            put("b_kv_s", kv.reshape(Bd, 1, 2, A_KV, HEAD_DIM))
            put("b_kidx_s", ki_new.reshape(Bd, 1, B_IDX_DIM))
        elif kind == 2:
            w = c_w_in[li]
            Wd = C_HEADS * HEAD_DIM
            ng = len(C_GROUPS)
            res = proj_p(w, [("rm", 3 * g * Wd, Wd, ("r",)) for g in range(ng)] + [("rm", 3 * ng * Wd, Wd, ("n",))]
                         + [("t", (3 * g + 1) * Wd, 2 * Wd, "trk") for g in range(ng)])
            qs, z, kvs_t = res[0:ng], res[ng], res[ng + 1:]
            o = _dil_prompt([t.reshape(B, T, Wd) for t in qs], kvs_t, PASSES_SMOOTH)
            op, zp, wo = o.reshape(B * T, Wd), z, c_w_out[li]
            for g, (wg, _) in enumerate(C_GROUPS):
                put("c_win%d_p" % g, rows_of(kvs_t[g][:, :, T - min(wg, T):], 2, C_HEADS))

            res = proj_s(w, [seg for g in range(ng) for seg in (("rm", 3 * g * Wd, Wd, ("r",)), ("rm", (3 * g + 1) * Wd, 2 * Wd, ("rk",)))]
                         + [("rm", 3 * ng * Wd, Wd, ("n",))])
            qs, kvs, z = res[0:2 * ng:2], res[1:2 * ng:2], res[2 * ng]
            outs = _dil_sample([_rows_minor(b) for b in c_bufs], li, [_cols(t, (2, C_HEADS, HEAD_DIM)) for t in kvs],
                               [_cols(t, (C_HEADS, HEAD_DIM)) for t in qs], P)
            os_, zs = outs[ng][..., 0].reshape(Bd, Wd), z
            for g in range(ng):
                put("c_win%d_s" % g, _rows_major(outs[g]))
        else:
            w = d_w_in[li]
            q, z, kv_t, f_t = proj_p(w, [("rm", 0, D, ("n",)), ("rm", 3 * D + H, D, ("n",)), ("t", D, 2 * D, "t"), ("t", 3 * D, H, "t")])
            fb = d_f_bias[li].reshape(H, 1)
            lf_t, c_t = _fox_logf(f_t, fb)
            o = _fox_prompt(q.reshape(B, T, D), kv_t, jnp.swapaxes(c_t, 1, 2), c_t, PASSES_SMOOTH)
            op, zp, wo = o.reshape(B * T, D), z, d_w_out[li]
            put("d_kv_p", rows_of(kv_t, 2, H))
            put("d_logf_p", jnp.swapaxes(lf_t, 1, 2))

            q, kv, f, z = proj_s(w, [("rm", 0, D, ("n",)), ("rm", D, 2 * D, ("n",)), ("rm", 3 * D, H, ("n",)), ("rm", 3 * D + H, D, ("n",))])
            lf = _fox_logf(f[:, :H].T.reshape(1, H, Bd), fb)[0][0].T
            kv_t6 = _rows_minor(cache_d_kv)
            o = _fox_sample(kv_t6.reshape(kv_t6.shape[0], pool, 2 * D, PAGE), jnp.swapaxes(cache_d_logf, 2, 3), li, page_table,
                            _cols(q, (1, D)), _cols(kv, (2 * D,)), _cols(lf, (H,)))
            os_, zs = o[..., 0].reshape(Bd, D), z
            put("d_kv_s", kv.reshape(Bd, 1, 2, H, HEAD_DIM))
            put("d_logf_s", lf.reshape(Bd, 1, H))
        fg = final_g if i == depth - 1 else None
        xp, yp = _gated_out(op, zp, wo, xp, gate_p, tpb, tm_p, PASSES_PROMPT_PROJ, fg)
        xs, ys = _gated_out(os_, zs, wo, xs, gate_s, 1, Bd, PASSES_SELECTIVE, fg)
    ns = {n: jnp.stack(v) for n, v in st.items()}
    names = ("a_cmp_p", "a_cmp_s", "a_sel_p", "a_sel_s", "a_win_p", "a_win_s", "b_kv_p", "b_kv_s", "b_kidx_p", "b_kidx_s",
             "c_win0_p", "c_win0_s", "c_win1_p", "c_win1_s", "c_win2_p", "c_win2_s", "d_kv_p", "d_kv_s", "d_logf_p", "d_logf_s")
    return (yp.reshape(B, T, D), ys.reshape(Bd, 1, D)) + tuple(ns[n] for n in names)
```

```python
import functools

import jax
import jax.numpy as jnp
from jax import lax
from jax.experimental import pallas as pl
from jax.experimental.pallas import tpu as pltpu

F32 = jnp.float32
BF16 = jnp.bfloat16
I32 = jnp.int32

HEAD_DIM = 64
ROPE_DIMS = HEAD_DIM // 4
ROPE_HALF = ROPE_DIMS // 2
ROPE_THETA = 500000.0
NORM_EPS = 1e-6
NEG_INF = -1e30
TINY = 1e-30
PAGE = 128
LANES = 128
INT_MIN = -2 ** 31

A_KV = 4
A_CMP_STRIDE = 16
A_CMP_LEN = 32
A_SEL_LEN = 64
A_SEL_TOPK = 16
A_WINDOW = 512
A_FORCED = 1e4
B_IDX_HEADS = 8
B_IDX_DIM = 64
B_TOPK_MAX = 256
C_GROUPS = ((128, 1), (512, 4), (2048, 16))
C_HEADS = 8
QK_SCALE = HEAD_DIM ** -0.5

NN = ((1,), (0,))
NT = ((1,), (1,))

VMEM_LIMIT = 56 * 1024 * 1024
PROJ_COLS_PER_CALL = 2048

PASSES_SELECTIVE = 3
PASSES_SMOOTH = 1
PASSES_NSA_FLASH = 1
PASSES_PROMPT_PROJ = 1
PASSES_PROMPT_INDEXER = 1


def _cp(*sem):
    return pltpu.CompilerParams(dimension_semantics=sem, vmem_limit_bytes=VMEM_LIMIT)


def _dotf(a, b, dims=NN):
    return lax.dot_general(a, b, (dims, ((), ())), preferred_element_type=F32)


def _split(a, terms):
    out = []
    for _ in range(terms - 1):
        h = a.astype(BF16)
        out.append(h)
        a = a - h.astype(F32)
    out.append(a.astype(BF16))
    return out


def _dot3(a, b, dims=NN):
    ah, al = _split(a, 2)
    bh, bl = _split(b, 2)
    return _dotf(ah, bh, dims) + (_dotf(ah, bl, dims) + _dotf(al, bh, dims))


def _dotp(a, b, dims, passes):
    if passes == 1:
        return _dotf(a.astype(BF16), b.astype(BF16), dims)
    return _dot3(a, b, dims)


def _dotx(a, m01, terms, dims=NN):
    acc = None
    for t in _split(a, terms):
        d = _dotf(t, m01, dims)
        acc = d if acc is None else acc + d
    return acc


def _iota(shape, axis):
    return lax.broadcasted_iota(I32, shape, axis)


def _onehot(cond):
    return jnp.where(cond, 1.0, 0.0).astype(BF16)


def _sigmoid(x):
    return 1.0 / (1.0 + jnp.exp(-x))


def _sortable_key(x):
    x = jnp.where(x == 0.0, 0.0, x)
    b = lax.bitcast_convert_type(x, I32)
    return jnp.where(b < 0, b ^ 0x7FFFFFFF, b)


def _count_ge(key, cand):
    return jnp.sum(jnp.where(key >= cand, 1.0, 0.0), axis=-1, keepdims=True)


def _kth_largest_key(key, k):
    base = jnp.where(_count_ge(key, 0) >= k, 0, INT_MIN).astype(I32)

    def body(it, base):
        cand = base | jnp.left_shift(jnp.int32(1), 30 - it)
        return jnp.where(_count_ge(key, cand) >= k, cand, base)

    return lax.fori_loop(0, 31, body, base)


def _topk_mask(key, k):
    n = key.shape[1]
    thr = _kth_largest_key(key, k)
    gt = key > thr
    eq = key == thr
    need = k - jnp.sum(jnp.where(gt, 1.0, 0.0), axis=-1, keepdims=True)
    before = _onehot(_iota((LANES, LANES), 0) < _iota((LANES, LANES), 1))
    run = jnp.zeros_like(need)
    out = []
    for c in range(n // LANES):
        sl = slice(c * LANES, (c + 1) * LANES)
        eqf = jnp.where(eq[:, sl], 1.0, 0.0)
        prior = _dotf(eqf.astype(BF16), before) + run
        out.append(gt[:, sl] | (eq[:, sl] & (prior < need)))
        run = run + jnp.sum(eqf, axis=-1, keepdims=True)
    return jnp.concatenate(out, axis=1)


def _flash_streams(streams, c_lo, c_hi, tk, passes, c_free=None):
    ones_rows = jnp.ones((8, tk), F32)

    def step(c, carry, masked):
        off = pl.multiple_of(c * tk, tk)
        out = []
        for (q, kt_ref, vt_ref, krow, vrow, mask_fn, bias_fn), (m, acc) in zip(streams, carry):
            kt = kt_ref[0, krow:krow + HEAD_DIM, pl.ds(off, tk)]
            vt = jnp.concatenate([vt_ref[0, vrow:vrow + HEAD_DIM, pl.ds(off, tk)], ones_rows], axis=0)
            s = _dotp(q, kt, NN, passes)
            if bias_fn is not None:
                s = s + bias_fn(c)
            if masked:
                s = mask_fn(c, s)
            m_new = jnp.maximum(m, jnp.max(s, axis=-1, keepdims=True))
            e = jnp.exp(s - m_new)
            out.append((m_new, jnp.exp(m - m_new) * acc + _dotp(e, vt, NT, passes)))
        return tuple(out)

    carry = tuple((jnp.full((s[0].shape[0], 1), 0.1 * NEG_INF, F32), jnp.zeros((s[0].shape[0], HEAD_DIM + 8), F32))
                  for s in streams)
    if c_free is not None:
        carry = lax.fori_loop(c_lo, c_free, functools.partial(step, masked=False), carry)
        c_lo = c_free
    carry = lax.fori_loop(c_lo, c_hi, functools.partial(step, masked=True), carry)
    res = []
    for m, acc in carry:
        den = acc[:, HEAD_DIM:HEAD_DIM + 1]
        res.append((acc[:, 0:HEAD_DIM] / jnp.maximum(den, TINY), m, den))
    return res


def _flash(q, kt_ref, vt_ref, krow, vrow, c_lo, c_hi, tk, mask_fn, passes, bias_fn=None, c_free=None):
    return _flash_streams([(q, kt_ref, vt_ref, krow, vrow, mask_fn, bias_fn)], c_lo, c_hi, tk, passes, c_free)[0]


def _stack_heads(ref, first_head, n):
    return jnp.concatenate(
        [ref[0, :, (first_head + r) * HEAD_DIM:(first_head + r + 1) * HEAD_DIM] for r in range(n)], axis=0)


def _group_block_diag(qt, n_heads, n_groups):
    per = n_heads // n_groups
    shift = per.bit_length() - 1
    keep = (_iota(qt.shape, 0) >> shift) == (_iota(qt.shape, 1) >> 6)
    return jnp.where(keep, qt, 0.0), keep


def _fold_groups(x, keep, n_groups):
    x = jnp.where(keep, x, 0.0)
    return functools.reduce(lambda a, b: a + b, [x[:, g * HEAD_DIM:(g + 1) * HEAD_DIM] for g in range(n_groups)])


def _mod_kernel(c_ref, w_ref, b_ref, o_ref):
    c = c_ref[...]
    o_ref[0] = _dot3(c * _sigmoid(c), w_ref[0]) + b_ref[0]


def _mod_all(c_all, ada_w, ada_b):
    L, D, D3 = ada_w.shape
    NC = c_all.shape[0]
    tn = 1024
    return pl.pallas_call(
        _mod_kernel, grid=(L, D3 // tn),
        in_specs=[pl.BlockSpec((NC, D), lambda l, j: (0, 0)),
                  pl.BlockSpec((1, D, tn), lambda l, j: (l, 0, j)),
                  pl.BlockSpec((1, 1, tn), lambda l, j: (l, 0, j))],
        out_specs=pl.BlockSpec((1, NC, tn), lambda l, j: (l, 0, j)),
        out_shape=jax.ShapeDtypeStruct((L, NC, D3), F32),
        compiler_params=_cp("parallel", "parallel"), name="adaln_mod",
    )(c_all, ada_w, ada_b.reshape(L, 1, D3))


def _proj_kernel(*refs, rsegs, tsegs, passes):
    x_ref, sc_ref, sh_ref, g_ref = refs[0:4]
    pos = 4
    if rsegs:
        wh_ref, wl_ref, cos_ref, sn_ref, sp_ref = refs[pos:pos + 5]
        pos += 5
    if tsegs:
        wth_ref, wtl_ref, cost_ref, sint_ref = refs[pos:pos + 4]
        pos += 4
    out_refs = refs[pos:]
    x = x_ref[...]
    y = x * lax.rsqrt(jnp.mean(x * x, axis=-1, keepdims=True) + NORM_EPS) * g_ref[...]
    h = y * (1.0 + sc_ref[0]) + sh_ref[0]
    hh, hl = _split(h, 2)
    oi = 0
    for start, width, modes in rsegs:
        wh = wh_ref[:, start:start + width]
        u = _dotf(hh, wh)
        if passes == 3:
            u = u + (_dotf(hh, wl_ref[:, start:start + width]) + _dotf(hl, wh))
        for mode in modes:
            o_ref = out_refs[oi]
            oi += 1
            n_rope = {"n": 0, "r": width, "rk": width // 2}[mode]
            for c in range(width // LANES):
                sl = slice(c * LANES, (c + 1) * LANES)
                uc = u[:, sl]
                if c * LANES < n_rope:
                    uc = (uc * cos_ref[...] + pltpu.roll(uc, LANES - ROPE_HALF, 1) * sn_ref[...]
                          + pltpu.roll(uc, ROPE_HALF, 1) * sp_ref[...])
                o_ref[:, sl] = uc
    for start, width, mode in tsegs:
        wth = wth_ref[start:start + width, :]
        ut = _dotf(wth, hh, NT)
        if passes == 3:
            ut = ut + (_dotf(wth, hl, NT) + _dotf(wtl_ref[start:start + width, :], hh, NT))
        o_ref = out_refs[oi]
        oi += 1
        o_ref[0] = ut
        n_rope = {"t": 0, "tr": width, "trk": width // 2}[mode]
        for hd in range(n_rope // HEAD_DIM):
            r0 = hd * HEAD_DIM
            x1 = ut[r0:r0 + ROPE_HALF]
            x2 = ut[r0 + ROPE_HALF:r0 + ROPE_DIMS]
            o_ref[0, r0:r0 + ROPE_HALF, :] = x1 * cost_ref[...] - x2 * sint_ref[...]
            o_ref[0, r0 + ROPE_HALF:r0 + ROPE_DIMS, :] = x2 * cost_ref[...] + x1 * sint_ref[...]


def _proj(x2, scale, shift, g, w, segs, tables, ttables, tiles_per_batch, tm, passes):
    R, D = x2.shape
    NB, RB = scale.shape[0], scale.shape[1]
    Tt = R // NB
    groups, cur, cur_w = [], [], 0
    for seg in segs:
        wd = -(-seg[2] // LANES) * LANES
        if cur and cur_w + wd > PROJ_COLS_PER_CALL:
            groups.append(cur)
            cur, cur_w = [], 0
        cur.append(seg)
        cur_w += wd
    groups.append(cur)
    results = {}
    for grp in groups:
        rsegs, tsegs, rcols, tcols = [], [], [], []
        out_shapes, out_specs, keys = [], [], []
        rpos = tpos = 0
        for si, (kind, c0, wd, spec) in enumerate(grp):
            if kind == "rm":
                pw = -(-wd // LANES) * LANES
                rcols.append(w[:, c0:c0 + wd])
                if pw > wd:
                    rcols.append(jnp.zeros((D, pw - wd), F32))
                rsegs.append((rpos, pw, spec))
                rpos += pw
                for mi in range(len(spec)):
                    out_shapes.append(jax.ShapeDtypeStruct((R, pw), F32))
                    out_specs.append(pl.BlockSpec((tm, pw), lambda i: (i, 0)))
                    keys.append((id(grp), si, mi))
        for si, (kind, c0, wd, spec) in enumerate(grp):
            if kind == "t":
                tcols.append(w[:, c0:c0 + wd])
                tsegs.append((tpos, wd, spec))
                tpos += wd
                out_shapes.append(jax.ShapeDtypeStruct((NB, wd, Tt), F32))
                out_specs.append(pl.BlockSpec((1, wd, tm), lambda i: (i // tiles_per_batch, 0, i % tiles_per_batch)))
                keys.append((id(grp), si, 0))
        mod_spec = pl.BlockSpec((1, RB, D), lambda i: (i // tiles_per_batch, 0, 0))
        in_specs = [pl.BlockSpec((tm, D), lambda i: (i, 0)), mod_spec, mod_spec, pl.BlockSpec((1, D), lambda i: (0, 0))]
        args = [x2, scale, shift, g.reshape(1, D)]
        if rsegs:
            wr = jnp.concatenate(rcols, axis=1)
            wh = wr.astype(BF16)
            wl = (wr - wh.astype(F32)).astype(BF16) if passes == 3 else wh
            tab_spec = pl.BlockSpec((tm, LANES), lambda i: (i % tiles_per_batch, 0))
            in_specs += [pl.BlockSpec((D, rpos), lambda i: (0, 0))] * 2 + [tab_spec] * 3
            args += [wh, wl, *tables]
        if tsegs:
            wt = jnp.concatenate(tcols, axis=1).T
            wth = wt.astype(BF16)
            wtl = (wt - wth.astype(F32)).astype(BF16) if passes == 3 else wth
            ttab_spec = pl.BlockSpec((ROPE_HALF, tm), lambda i: (0, i % tiles_per_batch))
            in_specs += [pl.BlockSpec((tpos, D), lambda i: (0, 0))] * 2 + [ttab_spec] * 2
            args += [wth, wtl, *ttables]
        res = pl.pallas_call(
            functools.partial(_proj_kernel, rsegs=tuple(rsegs), tsegs=tuple(tsegs), passes=passes), grid=(R // tm,),
            in_specs=in_specs, out_specs=out_specs, out_shape=out_shapes,
            compiler_params=_cp("parallel"), name="norm_mod_proj",
        )(*args)
        for key, r in zip(keys, res):
            results[key] = r
    outs = []
    for grp in groups:
        for si, (kind, c0, wd, spec) in enumerate(grp):
            for mi in range(len(spec) if kind == "rm" else 1):
                outs.append(results[(id(grp), si, mi)])
    return outs


def _rope_angles(pos):
    inv = ROPE_THETA ** (-(jnp.arange(ROPE_HALF, dtype=F32) / ROPE_HALF))
    ang = pos.astype(F32)[:, None] * inv[None, :]
    return jnp.cos(ang), jnp.sin(ang)


def _rope_tables(pos):
    cos, sin = _rope_angles(pos)
    R = pos.shape[0]
    one = jnp.ones((R, HEAD_DIM - ROPE_DIMS), F32)
    zero = jnp.zeros((R, HEAD_DIM - ROPE_DIMS), F32)
    zh = jnp.zeros((R, ROPE_HALF), F32)
    cos_h = jnp.concatenate([cos, cos, one], axis=1)
    sn_h = jnp.concatenate([-sin, zh, zero], axis=1)
    sp_h = jnp.concatenate([zh, sin, zero], axis=1)
    rep = LANES // HEAD_DIM
    return tuple(jnp.tile(t, (1, rep)) for t in (cos_h, sn_h, sp_h))


def _rope_tables_t(pos):
    cos, sin = _rope_angles(pos)
    return cos.T, sin.T


def _out_kernel(o_ref, z_ref, wh_ref, wl_ref, x_ref, gate_ref, *rest, final, passes):
    z = z_ref[...]
    y = o_ref[...] * (z * _sigmoid(z))
    yh, yl = _split(y, 2)
    wh = wh_ref[...]
    r = _dotf(yh, wh)
    if passes == 3:
        r = r + (_dotf(yh, wl_ref[...]) + _dotf(yl, wh))
    xn = x_ref[...] + gate_ref[0] * r
    if final:
        fg_ref, xo_ref, yo_ref = rest
        yo_ref[...] = xn * lax.rsqrt(jnp.mean(xn * xn, axis=-1, keepdims=True) + NORM_EPS) * fg_ref[...]
    else:
        (xo_ref,) = rest
    xo_ref[...] = xn


def _gated_out(o2, z2, w_out, x2, gate, tiles_per_batch, tm, passes, final_g=None):
    R, W = o2.shape
    D = x2.shape[1]
    RB = gate.shape[1]
    wh = w_out.astype(BF16)
    wl = (w_out - wh.astype(F32)).astype(BF16) if passes == 3 else wh
    final = final_g is not None
    in_specs = [pl.BlockSpec((tm, W), lambda i: (i, 0)), pl.BlockSpec((tm, W), lambda i: (i, 0)),
                pl.BlockSpec((W, D), lambda i: (0, 0)), pl.BlockSpec((W, D), lambda i: (0, 0)),
                pl.BlockSpec((tm, D), lambda i: (i, 0)),
                pl.BlockSpec((1, RB, D), lambda i: (i // tiles_per_batch, 0, 0))]
    args = [o2, z2, wh, wl, x2, gate]
    out_shape = [jax.ShapeDtypeStruct((R, D), F32)]
    out_specs = [pl.BlockSpec((tm, D), lambda i: (i, 0))]
    if final:
        in_specs.append(pl.BlockSpec((1, D), lambda i: (0, 0)))
        args.append(final_g.reshape(1, D))
        out_shape.append(jax.ShapeDtypeStruct((R, D), F32))
        out_specs.append(pl.BlockSpec((tm, D), lambda i: (i, 0)))
    res = pl.pallas_call(
        functools.partial(_out_kernel, final=final, passes=passes), grid=(R // tm,),
        in_specs=in_specs, out_specs=out_specs, out_shape=out_shape,
        compiler_params=_cp("parallel"), name="gated_out_proj",
    )(*args)
    return res if final else (res[0], None)


def _cmp_taps(wk, wv, groups):
    def table(w):
        t = jnp.tile(w.T, (1, LANES // A_CMP_STRIDE))
        return jnp.tile(t, (groups, 1))
    S = A_CMP_STRIDE
    lo = jnp.concatenate([table(wk[:S]), table(wv[:S])], axis=0)
    hi = jnp.concatenate([table(wk[S:]), table(wv[S:])], axis=0)
    return lo, hi


def _chunk_sum_matrix(first_col):
    shift = A_CMP_STRIDE.bit_length() - 1
    return _onehot((_iota((LANES, LANES), 0) >> shift) + first_col == _iota((LANES, LANES), 1))


def _cmp_kernel(x_ref, wlo_ref, whi_ref, o_ref, *, T):
    rows = x_ref.shape[1]
    per = LANES // A_CMP_STRIDE
    lo = jnp.zeros((rows, LANES), F32)
    hi = jnp.zeros((rows, LANES), F32)
    for c in range(T // LANES):
        xc = x_ref[0, :, c * LANES:(c + 1) * LANES]
        m = _chunk_sum_matrix(c * per)
        lo = lo + _dotx(xc * wlo_ref[...], m, 3)
        hi = hi + _dotx(xc * whi_ref[...], m, 3)
    o_ref[0] = lo + pltpu.roll(hi, LANES - 1, 1)


def _nsa_compress_prompt(cmp_t, wlo, whi):
    B, rows, T = cmp_t.shape
    assert T // A_CMP_STRIDE <= LANES
    return pl.pallas_call(
        functools.partial(_cmp_kernel, T=T), grid=(B,),
        in_specs=[pl.BlockSpec((1, rows, T), lambda b: (b, 0, 0)), pl.BlockSpec((rows, LANES), lambda b: (0, 0)),
                  pl.BlockSpec((rows, LANES), lambda b: (0, 0))],
        out_specs=pl.BlockSpec((1, rows, LANES), lambda b: (b, 0, 0)),
        out_shape=jax.ShapeDtypeStruct((B, rows, LANES), F32),
        compiler_params=_cp("parallel"), name="nsa_compress",
    )(cmp_t, wlo, whi)


def _nsa_prompt_kernel(q_ref, qr_ref, kvb_ref, sel_ref, win_ref, gl_ref, wt_ref, o_ref, *, tq, tk, T, flash_passes):
    i = pl.program_id(1)
    t0 = i * tq
    R = 4
    nb = kvb_ref.shape[2]
    ncb = T // A_CMP_STRIDE - 1
    nsb = T // A_SEL_LEN
    gw = A_KV * HEAD_DIM
    gates = _sigmoid(gl_ref[0])

    n_io = _iota((R * tq, nb), 1)
    t_c = (_iota((R * tq, nb), 0) & (tq - 1)) + t0
    cmask = (n_io * A_CMP_STRIDE + (A_CMP_LEN - 1) <= t_c) & (n_io < ncb)

    ratio = A_SEL_LEN // A_CMP_STRIDE
    back = A_CMP_LEN // A_CMP_STRIDE - 1
    mn, mj = _iota((nb, LANES), 0), _iota((nb, LANES), 1)
    imp_mat = _onehot((mn >= mj * ratio - back) & (mn <= mj * ratio + ratio - 1) & (mn < ncb) & (mj < nsb))

    j_io = _iota((tq, LANES), 1)
    qb = (_iota((tq, LANES), 0) + t0) >> 6
    allowed = (j_io <= qb) & (j_io < nsb)
    forced = (j_io == 0) | (j_io == qb)
    nsr = -(-nsb // 8) * 8
    jt_io = _iota((nsr, tq), 0)
    allowed_t = (jt_io <= ((_iota((nsr, tq), 1) + t0) >> 6)) & (jt_io < nsb)

    dist = (_iota((R * tq, tk), 0) & (tq - 1)) + t0 - _iota((R * tq, tk), 1)
    c_hi = (t0 + tq + tk - 1) // tk
    w_lo = jnp.maximum(t0 - (A_WINDOW - 1), 0) // tk

    def win_mask(c, s):
        bias = wt_ref[:, pl.ds(pl.multiple_of(c * tk - t0 + T, LANES), tk)]
        return s + jnp.concatenate([bias] * R, axis=0)

    def group_front(g):
        q4 = _stack_heads(q_ref, R * g, R) * QK_SCALE
        kb = kvb_ref[0, g * HEAD_DIM:(g + 1) * HEAD_DIM, :]
        vb = kvb_ref[0, gw + g * HEAD_DIM:gw + (g + 1) * HEAD_DIM, :]
        s = jnp.where(cmask, _dot3(q4, kb, NN), NEG_INF)
        m = jnp.max(s, axis=-1, keepdims=True)
        e = jnp.where(cmask, jnp.exp(s - m), 0.0)
        p = e / jnp.maximum(jnp.sum(e, axis=-1, keepdims=True), TINY)
        o_cmp = _dot3(p, vb, NT)
        pg = p[0:tq] + p[tq:2 * tq] + p[2 * tq:3 * tq] + p[3 * tq:4 * tq]
        imp = _dotx(pg, imp_mat, 3)
        score = jnp.where(allowed, jnp.where(forced, A_FORCED, imp), -1.0)
        score = jnp.where(j_io < nsb, score, -2.0)
        score_t = score.T[0:nsr]
        rank = jnp.zeros((nsr, tq), F32)
        for j2 in range(nsb):
            other = score_t[j2:j2 + 1, :]
            rank = rank + jnp.where((other > score_t) | ((other == score_t) & (j2 < jt_io)), 1.0, 0.0)
        sel_t = jnp.where((rank < min(A_SEL_TOPK, nsb)) & allowed_t, 1.0, 0.0)
        sel = jnp.concatenate([sel_t, jnp.zeros((LANES - nsr, tq), F32)], axis=0).T.astype(BF16)
        sel4 = jnp.concatenate([sel] * R, axis=0)

        def sel_mask(c, s):
            blk = (_iota((LANES, tk), 1) + c * tk) >> 6
            hit = _dotf(sel4, _onehot(blk == _iota((LANES, tk), 0)))
            return jnp.where((hit > 0.5) & (dist >= c * tk), s, NEG_INF)

        return o_cmp, sel_mask

    for g0 in range(0, A_KV, 2):
        pair = (g0, g0 + 1)
        fronts = [group_front(g) for g in pair]
        q4rs = [_stack_heads(qr_ref, R * g, R) * QK_SCALE for g in pair]
        sel_res = _flash_streams(
            [(q4r, sel_ref, sel_ref, g * HEAD_DIM, gw + g * HEAD_DIM, front[1], None) for g, q4r, front in zip(pair, q4rs, fronts)],
            0, c_hi, tk, flash_passes)
        win_res = _flash_streams(
            [(q4r, win_ref, win_ref, g * HEAD_DIM, gw + g * HEAD_DIM, win_mask, None) for g, q4r in zip(pair, q4rs)],
            w_lo, c_hi, tk, flash_passes)
        for g, (o_cmp, _), (o_sel, _, _), (o_win, _, _) in zip(pair, fronts, sel_res, win_res):
            outs = []
            for r in range(R):
                h = R * g + r
                rows = slice(r * tq, (r + 1) * tq)
                outs.append(gates[:, 3 * h:3 * h + 1] * o_cmp[rows] + gates[:, 3 * h + 1:3 * h + 2] * o_sel[rows]
                            + gates[:, 3 * h + 2:3 * h + 3] * o_win[rows])
            o_ref[0, :, g * R * HEAD_DIM:(g + 1) * R * HEAD_DIM] = jnp.concatenate(outs, axis=1)


def _nsa_prompt(q3, qr3, kvb, sel_t, win_t, gl3, flash_passes, tq=256, tk=512):
    B, T, D = q3.shape
    full = lambda a: pl.BlockSpec((1,) + a.shape[1:], lambda b, i: (b, 0, 0))
    win_table = _mask_table(tq, tk, T, lambda d: (d >= 0) & (d < A_WINDOW))
    return pl.pallas_call(
        functools.partial(_nsa_prompt_kernel, tq=tq, tk=tk, T=T, flash_passes=flash_passes), grid=(B, T // tq),
        in_specs=[pl.BlockSpec((1, tq, D), lambda b, i: (b, i, 0)), pl.BlockSpec((1, tq, D), lambda b, i: (b, i, 0)),
                  full(kvb), full(sel_t), full(win_t), pl.BlockSpec((1, tq, LANES), lambda b, i: (b, i, 0)),
                  pl.BlockSpec(win_table.shape, lambda b, i: (0, 0), pipeline_mode=pl.Buffered(1))],
        out_specs=pl.BlockSpec((1, tq, D), lambda b, i: (b, i, 0)),
        out_shape=jax.ShapeDtypeStruct((B, T, D), F32),
        compiler_params=_cp("parallel", "arbitrary"), name="nsa_prompt",
    )(q3, qr3, kvb, sel_t, win_t, gl3, win_table)


def _dsa_prompt_kernel(q_ref, qi_ref, wi_ref, kv_ref, ki_ref, o_ref, ch_ref, *, tq, tk, T, topk, passes, idx_passes):
    i = pl.program_id(1)
    t0 = i * tq
    R = 4
    gw = kv_ref.shape[1] // 2
    c_hi = (t0 + tq + tk - 1) // tk
    assert tk >= topk

    def select_keys(width):
        wi = wi_ref[0] * (B_IDX_HEADS * B_IDX_DIM) ** -0.5
        ki = ki_ref[0, :, 0:width]
        score = jnp.zeros((tq, width), F32)
        for h in range(B_IDX_HEADS):
            logits = _dotp(qi_ref[0, :, h * B_IDX_DIM:(h + 1) * B_IDX_DIM], ki, NN, idx_passes)
            score = score + wi[:, h:h + 1] * jnp.maximum(logits, 0.0)
        causal = _iota((tq, width), 1) <= _iota((tq, width), 0) + t0
        score = jnp.where(causal, score, NEG_INF)
        ch_ref[:, 0:width] = jnp.where(_topk_mask(_sortable_key(score), topk) & causal, 0.0, NEG_INF)

    for nc in range(1, T // tk + 1):
        pl.when(c_hi == nc)(functools.partial(select_keys, nc * tk))

    def mask_fn(c, s):
        picked = ch_ref[:, pl.ds(pl.multiple_of(c * tk, tk), tk)]
        return s + jnp.concatenate([picked] * R, axis=0)

    for g0 in range(0, gw // HEAD_DIM, 2):
        streams = [(_stack_heads(q_ref, R * g, R) * QK_SCALE, kv_ref, kv_ref, g * HEAD_DIM, gw + g * HEAD_DIM, mask_fn, None)
                   for g in (g0, g0 + 1)]
        for g, (o, _, _) in zip((g0, g0 + 1), _flash_streams(streams, 0, c_hi, tk, passes)):
            o_ref[0, :, g * R * HEAD_DIM:(g + 1) * R * HEAD_DIM] = jnp.concatenate(
                [o[r * tq:(r + 1) * tq] for r in range(R)], axis=1)


def _dsa_prompt(q3, qi3, wi3, kv_t, ki_t, topk, passes, idx_passes, tq=256, tk=512):
    B, T, D = q3.shape
    full = lambda a: pl.BlockSpec((1,) + a.shape[1:], lambda b, i: (b, 0, 0))
    blk = lambda w: pl.BlockSpec((1, tq, w), lambda b, i: (b, i, 0))
    return pl.pallas_call(
        functools.partial(_dsa_prompt_kernel, tq=tq, tk=tk, T=T, topk=topk, passes=passes, idx_passes=idx_passes),
        grid=(B, T // tq),
        in_specs=[blk(D), blk(qi3.shape[2]), blk(LANES), full(kv_t), full(ki_t)],
        out_specs=blk(D), out_shape=jax.ShapeDtypeStruct((B, T, D), F32),
        scratch_shapes=[pltpu.VMEM((tq, T), F32)],
        compiler_params=_cp("parallel", "arbitrary"), name="dsa_prompt",
    )(q3, qi3, wi3, kv_t, ki_t)


def _dil_prompt_kernel(*refs, tq, tk, T, passes):
    ng = len(C_GROUPS)
    q_refs, k_refs, v_refs, mt_ref, o_ref = refs[0:ng], refs[ng:2 * ng], refs[2 * ng:3 * ng], refs[3 * ng], refs[3 * ng + 1]
    i = pl.program_id(2)
    t0 = i * tq
    c_hi = (t0 + tq + tk - 1) // tk
    nhl = LANES // HEAD_DIM
    per_group = []
    for g, ((w, r), q_ref, k_ref, v_ref) in enumerate(zip(C_GROUPS, q_refs, k_refs, v_refs)):
        def mask_fn(c, s, g=g):
            return s + mt_ref[g, :, pl.ds(pl.multiple_of(c * tk - t0 + T, LANES), tk)]

        c_lo = jnp.maximum(t0 - w, 0) // tk
        streams = [(q_ref[0, :, hh * HEAD_DIM:(hh + 1) * HEAD_DIM] * QK_SCALE, k_ref, v_ref, hh * HEAD_DIM, hh * HEAD_DIM,
                    mask_fn, None) for hh in range(nhl)]
        per_group.append(_flash_streams(streams, c_lo, c_hi, tk, passes))
    outs = []
    for hh in range(nhl):
        res = [grp[hh] for grp in per_group]
        m_all = functools.reduce(jnp.maximum, [m for _, m, _ in res])
        wts = [den * jnp.exp(m - m_all) for _, m, den in res]
        tot = functools.reduce(lambda a, b: a + b, wts)
        outs.append(functools.reduce(lambda a, b: a + b, [(wt / tot) * o for wt, (o, _, _) in zip(wts, res)]))
    o_ref[0] = jnp.concatenate(outs, axis=1)


def _mask_table(tq, tk, T, valid_fn):
    d = jnp.arange(tq, dtype=I32)[:, None] - (jnp.arange(T + tq + tk, dtype=I32)[None, :] - T)
    return jnp.where(valid_fn(d), 0.0, NEG_INF).astype(F32)


def _dil_prompt(qs, kvs_t, passes, tq=512, tk=512):
    B, T, W = qs[0].shape
    hp = W // LANES
    qspec = pl.BlockSpec((1, tq, LANES), lambda b, h, i: (b, i, h))
    kspec = pl.BlockSpec((1, LANES, T), lambda b, h, i: (b, h, 0))
    vspec = pl.BlockSpec((1, LANES, T), lambda b, h, i: (b, hp + h, 0))
    ng = len(C_GROUPS)
    tables = jnp.stack([_mask_table(tq, tk, T, lambda d, w=w, r=r: (d >= 0) & (d <= w) & (d % r == 0)) for w, r in C_GROUPS])
    tspec = pl.BlockSpec(tables.shape, lambda b, h, i: (0, 0, 0), pipeline_mode=pl.Buffered(1))
    return pl.pallas_call(
        functools.partial(_dil_prompt_kernel, tq=tq, tk=tk, T=T, passes=passes), grid=(B, hp, T // tq),
        in_specs=[qspec] * ng + [kspec] * ng + [vspec] * ng + [tspec],
        out_specs=qspec, out_shape=jax.ShapeDtypeStruct((B, T, W), F32),
        compiler_params=_cp("parallel", "parallel", "arbitrary"), name="dilated_prompt",
    )(*qs, *kvs_t, *kvs_t, tables)


def _logf_kernel(f_ref, b_ref, lf_ref, c_ref, *, T, tc):
    x = f_ref[0] + b_ref[...]
    lf = jnp.minimum(x, 0.0) - jnp.log(1.0 + jnp.exp(-jnp.abs(x)))
    lf_ref[0] = lf
    parts = _split(lf, 3)
    for c in range(T // tc):
        upto = _onehot(_iota((T, tc), 0) <= _iota((T, tc), 1) + c * tc)
        c_ref[0, :, c * tc:(c + 1) * tc] = functools.reduce(lambda a, b: a + b, [_dotf(p, upto) for p in parts])


def _fox_logf(f_t, bias_col):
    B, H, T = f_t.shape
    spec = pl.BlockSpec((1, H, T), lambda b: (b, 0, 0))
    return pl.pallas_call(
        functools.partial(_logf_kernel, T=T, tc=min(256, T)), grid=(B,),
        in_specs=[spec, pl.BlockSpec((H, 1), lambda b: (0, 0))],
        out_specs=[spec, spec], out_shape=[jax.ShapeDtypeStruct((B, H, T), F32)] * 2,
        compiler_params=_cp("parallel"), name="fox_logf_cumsum",
    )(f_t, bias_col)


def _fox_prompt_kernel(q_ref, k_ref, v_ref, cc_ref, cr_ref, o_ref, *, tq, tk, T, passes):
    hp = pl.program_id(1)
    i = pl.program_id(2)
    t0 = i * tq
    dist = _iota((tq, tk), 0) + t0 - _iota((tq, tk), 1)
    c_hi = (t0 + tq + tk - 1) // tk
    c_free = (t0 + 1) // tk
    nh = cc_ref.shape[2]

    def mask_fn(c, s):
        return jnp.where(dist >= c * tk, s, NEG_INF)

    streams = []
    for hh in range(LANES // HEAD_DIM):
        h = hp * (LANES // HEAD_DIM) + hh
        lane = hh * HEAD_DIM
        c_col = jnp.sum(jnp.where(_iota((tq, nh), 1) == h, cc_ref[0], 0.0), axis=-1, keepdims=True)

        def bias_fn(c, h=h, c_col=c_col):
            off = pl.multiple_of(c * tk, tk)
            return c_col - cr_ref[0, pl.ds(h, 1), pl.ds(off, tk)]

        streams.append((q_ref[0, :, lane:lane + HEAD_DIM] * QK_SCALE, k_ref, v_ref, lane, lane, mask_fn, bias_fn))
    res = _flash_streams(streams, 0, c_hi, tk, passes, c_free)
    o_ref[0] = jnp.concatenate([o for o, _, _ in res], axis=1)


def _fox_prompt(q3, kv_t, c_col, c_row, passes, tq=512, tk=512):
    B, T, W = q3.shape
    hp = W // LANES
    nh = c_row.shape[1]
    return pl.pallas_call(
        functools.partial(_fox_prompt_kernel, tq=tq, tk=tk, T=T, passes=passes), grid=(B, hp, T // tq),
        in_specs=[pl.BlockSpec((1, tq, LANES), lambda b, h, i: (b, i, h)),
                  pl.BlockSpec((1, LANES, T), lambda b, h, i: (b, h, 0)),
                  pl.BlockSpec((1, LANES, T), lambda b, h, i: (b, hp + h, 0)),
                  pl.BlockSpec((1, tq, nh), lambda b, h, i: (b, i, 0)),
                  pl.BlockSpec((1, nh, T), lambda b, h, i: (b, 0, 0))],
        out_specs=pl.BlockSpec((1, tq, LANES), lambda b, h, i: (b, i, h)),
        out_shape=jax.ShapeDtypeStruct((B, T, W), F32),
        compiler_params=_cp("parallel", "parallel", "arbitrary"), name="fox_prompt",
    )(q3, kv_t, kv_t, c_col, c_row)


def _page_specs(n, block, layer, index_fn, lead=()):
    specs = []
    for pi in range(n):
        def imap(b, s, *pf, pi=pi):
            return (layer, index_fn(b, s, pi, *pf)) + lead + (0,) * (len(block) - 2 - len(lead))
        specs.append(pl.BlockSpec(block, imap))
    return specs


def _tdec_init(m_ref, l_ref, acc_ref):
    m_ref[...] = jnp.full(m_ref.shape, NEG_INF, F32)
    l_ref[...] = jnp.zeros(l_ref.shape, F32)
    acc_ref[...] = jnp.zeros(acc_ref.shape, F32)


def _tdec_update(kt, vt, qexp, bias, valid, m_ref, l_ref, acc_ref, r):
    G = kt.shape[0] // HEAD_DIM
    s = jnp.sum((kt * qexp).reshape(G, HEAD_DIM, LANES), axis=1)
    if bias is not None:
        s = s + bias
    if valid is not None:
        s = jnp.where(valid, s, NEG_INF)
    m_old = m_ref[r]
    m_new = jnp.maximum(m_old, jnp.max(s, axis=-1, keepdims=True))
    alpha = jnp.exp(m_old - m_new)
    e = jnp.exp(s - m_new)
    if valid is not None:
        e = jnp.where(valid, e, 0.0)
    l_ref[r] = alpha * l_ref[r] + e
    acc_ref[r] = acc_ref[r] * alpha[:, :, None] + vt.reshape(G, HEAD_DIM, LANES) * e[:, None, :]
    m_ref[r] = m_new


def _tdec_finish(l_ref, acc_ref, r):
    den = jnp.maximum(jnp.sum(l_ref[r], axis=-1, keepdims=True), TINY)
    o = jnp.sum(acc_ref[r], axis=-1, keepdims=True) / den[:, :, None]
    return jnp.broadcast_to(o, acc_ref.shape[1:])


def _tdec_scratch(R, G):
    return [pltpu.VMEM((R, G, 1), F32), pltpu.VMEM((R, G, LANES), F32), pltpu.VMEM((R, G, HEAD_DIM, LANES), F32)]


def _scmp_kernel(pt_ref, *refs, pp):
    pages, wlo_ref, whi_ref, o_ref = refs[:pp], refs[pp], refs[pp + 1], refs[pp + 2]
    rows = wlo_ref.shape[0]
    per = PAGE // A_CMP_STRIDE
    lo = jnp.zeros((rows, LANES), F32)
    hi = jnp.zeros((rows, LANES), F32)
    for pi in range(pp):
        x = pages[pi][...]
        m = _chunk_sum_matrix(pi * per)
        lo = lo + _dotx(x * wlo_ref[...], m, 2)
        hi = hi + _dotx(x * whi_ref[...], m, 2)
    o_ref[0, 0:rows, :] = lo
    o_ref[0, rows:2 * rows, :] = hi


def _nsa_compress_sample(cache_t, layer, page_table, wlo, whi):
    Bd, NP = page_table.shape
    rows = cache_t.shape[2]
    pp = LANES * A_CMP_STRIDE // PAGE
    grid_spec = pltpu.PrefetchScalarGridSpec(
        num_scalar_prefetch=1, grid=(Bd, NP // pp),
        in_specs=_page_specs(pp, (None, None, rows, PAGE), layer, lambda b, s, pi, pt: pt[b, s * pp + pi])
        + [pl.BlockSpec((rows, LANES), lambda b, s, pt: (0, 0))] * 2,
        out_specs=pl.BlockSpec((1, 2 * rows, LANES), lambda b, s, pt: (b, 0, s)))
    return pl.pallas_call(
        functools.partial(_scmp_kernel, pp=pp), grid_spec=grid_spec,
        out_shape=jax.ShapeDtypeStruct((Bd, 2 * rows, NP * PAGE // A_CMP_STRIDE), F32),
        compiler_params=_cp("parallel", "arbitrary"), name="nsa_compress_paged",
    )(page_table, *([cache_t] * pp), wlo, whi)


def _nsa_sample_cmp_kernel(lohi_ref, qt_ref, o_ref, idx_ref, *, P, ncols):
    rows = lohi_ref.shape[1] // 2
    gw = rows // 2
    nch = lohi_ref.shape[2]
    H = qt_ref.shape[1]
    R = H // A_KV
    lpad = -(-(P + 1) // A_SEL_LEN) * A_SEL_LEN
    ncb = lpad // A_CMP_STRIDE - 1
    nsb = lpad // A_SEL_LEN
    qb = P // A_SEL_LEN
    lohi = lohi_ref[0]
    kvb = lohi[0:rows] + pltpu.roll(lohi[rows:2 * rows], nch - 1, 1)
    qbd, keep = _group_block_diag(qt_ref[0] * QK_SCALE, H, A_KV)
    n_io = _iota((H, nch), 1)
    valid = (n_io * A_CMP_STRIDE + (A_CMP_LEN - 1) <= P) & (n_io < ncb)
    s = jnp.where(valid, _dot3(qbd, kvb[0:gw], NN), NEG_INF)
    m = jnp.max(s, axis=-1, keepdims=True)
    e = jnp.where(valid, jnp.exp(s - m), 0.0)
    p = e / jnp.maximum(jnp.sum(e, axis=-1, keepdims=True), TINY)
    o_ref[0] = _fold_groups(_dot3(p, kvb[gw:rows], NT), keep, A_KV)
    ratio = A_SEL_LEN // A_CMP_STRIDE
    back = A_CMP_LEN // A_CMP_STRIDE - 1
    mn, mj = _iota((nch, ncols), 0), _iota((nch, ncols), 1)
    imp_mat = _onehot((mn >= mj * ratio - back) & (mn <= mj * ratio + ratio - 1) & (mn < ncb))
    j_io = _iota((8, ncols), 1)
    jf = j_io.astype(F32)
    lane = _iota((8, LANES), 1)
    row = _iota((8, LANES), 0)
    out = jnp.zeros((8, LANES), F32)
    for g in range(A_KV):
        pg = jnp.sum(p[R * g:R * (g + 1)], axis=0, keepdims=True)
        imp = _dotx(jnp.broadcast_to(pg, (8, nch)), imp_mat, 3)
        score = jnp.where(j_io <= qb, jnp.where((j_io == 0) | (j_io == qb), A_FORCED, imp), -1.0)
        score = jnp.where(j_io < nsb, score, -2.0)
        for k in range(min(A_SEL_TOPK, nsb)):
            best = jnp.max(score, axis=-1, keepdims=True)
            pick = jnp.min(jnp.where(score == best, jf, 1e9), axis=-1, keepdims=True)
            out = jnp.where((lane == k) & (row == g), pick, out)
            score = jnp.where(jf == pick, -3.0, score)
    idx_ref[0] = out.astype(I32)


def _nsa_sample_cmp(lohi, qt3, P):
    Bd, rows2, nch = lohi.shape
    H, gw = qt3.shape[1], qt3.shape[2]
    nsb = -(-(P + 1) // A_SEL_LEN)
    ncols = -(-nsb // LANES) * LANES
    return pl.pallas_call(
        functools.partial(_nsa_sample_cmp_kernel, P=P, ncols=ncols), grid=(Bd,),
        in_specs=[pl.BlockSpec((1, rows2, nch), lambda b: (b, 0, 0)), pl.BlockSpec((1, H, gw), lambda b: (b, 0, 0))],
        out_specs=[pl.BlockSpec((1, H, HEAD_DIM), lambda b: (b, 0, 0)), pl.BlockSpec((1, 8, LANES), lambda b: (b, 0, 0))],
        out_shape=[jax.ShapeDtypeStruct((Bd, H, HEAD_DIM), F32), jax.ShapeDtypeStruct((Bd, 8, LANES), I32)],
        compiler_params=_cp("parallel"), name="nsa_sample_cmp_select",
    )(lohi, qt3)


def _nsa_sample_sel_kernel(pt_ref, ix_ref, *refs, P, nk, bps):
    nblk = A_KV * bps
    kblks, vblks = refs[0:nblk], refs[nblk:2 * nblk]
    q_ref, new_ref, o_ref, m_ref, l_ref, acc_ref = refs[2 * nblk:]
    b = pl.program_id(0)
    k = pl.program_id(1)
    R = q_ref.shape[2] // HEAD_DIM
    qb = P // A_SEL_LEN
    half_shift = A_SEL_LEN.bit_length() - 1
    lane = _iota((1, PAGE), 1)

    @pl.when(k == 0)
    def _():
        _tdec_init(m_ref, l_ref, acc_ref)

    for g in range(A_KV):
        for u in range(bps):
            j = ix_ref[b, g * nk + k * bps + u]
            valid = ((lane >> half_shift) == (j & (PAGE // A_SEL_LEN - 1))) & (j < qb)
            kt = jnp.concatenate([kblks[g * bps + u][...]] * R, axis=0)
            vt = jnp.concatenate([vblks[g * bps + u][...]] * R, axis=0)
            _tdec_update(kt, vt, q_ref[0, g] * QK_SCALE, None, valid, m_ref, l_ref, acc_ref, g)

    @pl.when(k == nk // bps - 1)
    def _():
        for g in range(A_KV):
            hit = ix_ref[b, g * nk] == qb
            for kk in range(1, nk):
                hit = hit | (ix_ref[b, g * nk + kk] == qb)
            kt = jnp.concatenate([new_ref[0, 0, g]] * R, axis=0)
            vt = jnp.concatenate([new_ref[0, 1, g]] * R, axis=0)
            _tdec_update(kt, vt, q_ref[0, g] * QK_SCALE, None, (lane == 0) & hit, m_ref, l_ref, acc_ref, g)
            o_ref[0, g] = _tdec_finish(l_ref, acc_ref, g)


def _nsa_sample_sel(cache_t6, layer, page_table, idx_flat, qexp, new_cols, P):
    Bd, G, rw = qexp.shape[0], qexp.shape[1], qexp.shape[2]
    R = rw // HEAD_DIM
    nk = idx_flat.shape[1] // A_KV
    per = PAGE // A_SEL_LEN
    last = P // A_SEL_LEN - 1

    bps = 2 if nk % 2 == 0 else 1

    def blk_index(c, g, u):
        def imap(b, k, pt, ix):
            j = jnp.minimum(ix[b, g * nk + k * bps + u], last)
            return (layer, pt[b, j // per], c, g, 0, 0)
        return imap

    per_b = lambda shape: pl.BlockSpec((1,) + shape, lambda b, k, pt, ix: (b,) + (0,) * len(shape))
    blk = (None, None, None, None, HEAD_DIM, PAGE)
    grid_spec = pltpu.PrefetchScalarGridSpec(
        num_scalar_prefetch=2, grid=(Bd, nk // bps),
        in_specs=[pl.BlockSpec(blk, blk_index(c, g, u)) for c in range(2) for g in range(A_KV) for u in range(bps)]
        + [per_b((G, rw, LANES)), per_b((2, G, HEAD_DIM, LANES))],
        out_specs=per_b((G, R, HEAD_DIM, LANES)),
        scratch_shapes=_tdec_scratch(G, R))
    return pl.pallas_call(
        functools.partial(_nsa_sample_sel_kernel, P=P, nk=nk, bps=bps), grid_spec=grid_spec,
        out_shape=jax.ShapeDtypeStruct((Bd, G, R, HEAD_DIM, LANES), F32),
        compiler_params=_cp("parallel", "arbitrary"), name="nsa_sample_selected",
    )(page_table, idx_flat, *([cache_t6] * (2 * A_KV * bps)), qexp, new_cols)


def _nsa_merge_kernel(gl_ref, ocmp_ref, osel_ref, owin_ref, o_ref):
    gates = _sigmoid(gl_ref[...])
    o_ref[...] = gates[:, :, 0:1] * ocmp_ref[...] + gates[:, :, 1:2] * osel_ref[...] + gates[:, :, 2:3] * owin_ref[...]


def _nsa_sample_merge(gl3, ocmp3, osel3, owin3):
    whole = lambda a: pl.BlockSpec(a.shape, lambda i: (0,) * a.ndim)
    return pl.pallas_call(
        _nsa_merge_kernel, grid=(1,),
        in_specs=[whole(gl3), whole(ocmp3), whole(osel3), whole(owin3)], out_specs=whole(ocmp3),
        out_shape=jax.ShapeDtypeStruct(ocmp3.shape, F32),
        compiler_params=_cp("arbitrary"), name="nsa_sample_merge",
    )(gl3, ocmp3, osel3, owin3)


def _slide_lanes(buf, new_cols):
    W = buf.shape[-1]
    axis = buf.ndim - 1
    new = jnp.concatenate([new_cols] * (W // LANES), axis=axis)
    return jnp.where(_iota(buf.shape, axis) == W - 1, new, pltpu.roll(buf, W - 1, axis))


def _nsa_sample_win_kernel(buf_ref, new_ref, qt_ref, nbuf_ref, o_ref):
    rows = buf_ref.shape[1]
    gw = rows // 2
    H = qt_ref.shape[1]
    nb = _slide_lanes(buf_ref[0], new_ref[0])
    nbuf_ref[0] = nb
    qbd, keep = _group_block_diag(qt_ref[0] * QK_SCALE, H, A_KV)
    s = _dot3(qbd, nb[0:gw], NN)
    e = jnp.exp(s - jnp.max(s, axis=-1, keepdims=True))
    p = e / jnp.maximum(jnp.sum(e, axis=-1, keepdims=True), TINY)
    o_ref[0] = _fold_groups(_dot3(p, nb[gw:rows], NT), keep, A_KV)


def _nsa_sample_win(buf_t, new_cols, qt3):
    Bd, rows, W = buf_t.shape
    H, gw = qt3.shape[1], qt3.shape[2]
    return pl.pallas_call(
        _nsa_sample_win_kernel, grid=(Bd,),
        in_specs=[pl.BlockSpec((1, rows, W), lambda b: (b, 0, 0)), pl.BlockSpec((1, rows, LANES), lambda b: (b, 0, 0)),
                  pl.BlockSpec((1, H, gw), lambda b: (b, 0, 0))],
        out_specs=[pl.BlockSpec((1, rows, W), lambda b: (b, 0, 0)), pl.BlockSpec((1, H, HEAD_DIM), lambda b: (b, 0, 0))],
        out_shape=[jax.ShapeDtypeStruct((Bd, rows, W), F32), jax.ShapeDtypeStruct((Bd, H, HEAD_DIM), F32)],
        compiler_params=_cp("parallel"), name="nsa_sample_window",
    )(buf_t, new_cols, qt3)


def _dsa_idx_kernel(pt_ref, *refs, pp):
    pages, qi_ref, wi_ref, o_ref = refs[:pp], refs[pp], refs[pp + 1], refs[pp + 2]
    wi = wi_ref[0] * (B_IDX_HEADS * B_IDX_DIM) ** -0.5
    qi = qi_ref[0]
    for pi in range(pp):
        logits = _dot3(qi, pages[pi][...], NN)
        o_ref[0, pi] = jnp.sum(wi * jnp.maximum(logits, 0.0), axis=0, keepdims=True)


def _dsa_sample_scores(kidx_t, layer, page_table, qi3, wi3, pp=8):
    Bd, NP = page_table.shape
    grid_spec = pltpu.PrefetchScalarGridSpec(
        num_scalar_prefetch=1, grid=(Bd, NP // pp),
        in_specs=_page_specs(pp, (None, None, B_IDX_DIM, PAGE), layer, lambda b, s, pi, pt: pt[b, s * pp + pi])
        + [pl.BlockSpec((1, B_IDX_HEADS, B_IDX_DIM), lambda b, s, pt: (b, 0, 0)),
           pl.BlockSpec((1, B_IDX_HEADS, 1), lambda b, s, pt: (b, 0, 0))],
        out_specs=pl.BlockSpec((1, pp, 1, PAGE), lambda b, s, pt: (b, s, 0, 0)))
    return pl.pallas_call(
        functools.partial(_dsa_idx_kernel, pp=pp), grid_spec=grid_spec,
        out_shape=jax.ShapeDtypeStruct((Bd, NP, 1, PAGE), F32),
        compiler_params=_cp("parallel", "arbitrary"), name="dsa_sample_indexer",
    )(page_table, *([kidx_t] * pp), qi3, wi3)


def _dsa_select_kernel(sc_ref, qi_ref, wi_ref, kin_ref, o_ref, *, topk):
    Bd, P = sc_ref.shape
    wi = wi_ref[...] * (B_IDX_HEADS * B_IDX_DIM) ** -0.5
    logit = jnp.sum(qi_ref[...] * kin_ref[...], axis=-1, keepdims=True)
    s_new = jnp.sum(wi * jnp.maximum(logit, 0.0), axis=1)
    tail = jnp.where(_iota((Bd, LANES), 1) == 0, s_new, -jnp.inf)
    full = jnp.concatenate([sc_ref[...], tail], axis=1)
    o_ref[...] = jnp.where(_topk_mask(_sortable_key(full), topk), 1.0, 0.0)


def _dsa_sample_select(scores2, qi3, wi3, kinew3, topk):
    Bd, P = scores2.shape
    whole = lambda shape: pl.BlockSpec(shape, lambda i: (0,) * len(shape))
    return pl.pallas_call(
        functools.partial(_dsa_select_kernel, topk=topk), grid=(1,),
        in_specs=[whole(scores2.shape), whole(qi3.shape), whole(wi3.shape), whole(kinew3.shape)],
        out_specs=whole((Bd, P + LANES)), out_shape=jax.ShapeDtypeStruct((Bd, P + LANES), F32),
        compiler_params=_cp("arbitrary"), name="dsa_sample_topk",
    )(scores2, qi3, wi3, kinew3)


def _dsa_sample_attn_kernel(pt_ref, *refs, pp, nsteps):
    pages = refs[:pp]
    mask_ref, q_ref, new_ref, mnew_ref, o_ref, m_ref, l_ref, acc_ref = refs[pp:]
    s_id = pl.program_id(1)
    R = q_ref.shape[1]
    gw = new_ref.shape[1] // 2

    @pl.when(s_id == 0)
    def _():
        _tdec_init(m_ref, l_ref, acc_ref)

    for pi in range(pp):
        kv = pages[pi][...]
        valid = mask_ref[0, pi] > 0.5
        for r in range(R):
            _tdec_update(kv[0:gw], kv[gw:2 * gw], q_ref[0, r] * QK_SCALE, None, valid, m_ref, l_ref, acc_ref, r)

    @pl.when(s_id == nsteps - 1)
    def _():
        new = new_ref[0]
        valid = (_iota((1, LANES), 1) == 0) & (mnew_ref[0] > 0.5)
        for r in range(R):
            _tdec_update(new[0:gw], new[gw:2 * gw], q_ref[0, r] * QK_SCALE, None, valid, m_ref, l_ref, acc_ref, r)
            o_ref[0, r] = _tdec_finish(l_ref, acc_ref, r)


def _dsa_sample_attn(kv_t, layer, page_table, mask4, qexp, new_cols, mnew3, pp=16):
    Bd, NP = page_table.shape
    rows = kv_t.shape[2]
    R, gw = qexp.shape[1], qexp.shape[2]
    G = gw // HEAD_DIM
    nsteps = NP // pp
    per_b = lambda shape: pl.BlockSpec((1,) + shape, lambda b, s, pt: (b,) + (0,) * len(shape))
    grid_spec = pltpu.PrefetchScalarGridSpec(
        num_scalar_prefetch=1, grid=(Bd, nsteps),
        in_specs=_page_specs(pp, (None, None, rows, PAGE), layer, lambda b, s, pi, pt: pt[b, s * pp + pi])
        + [pl.BlockSpec((1, pp, 1, PAGE), lambda b, s, pt: (b, s, 0, 0)), per_b((R, gw, LANES)), per_b((rows, LANES)),
           per_b((1, LANES))],
        out_specs=per_b((R, G, HEAD_DIM, LANES)),
        scratch_shapes=_tdec_scratch(R, G))
    return pl.pallas_call(
        functools.partial(_dsa_sample_attn_kernel, pp=pp, nsteps=nsteps), grid_spec=grid_spec,
        out_shape=jax.ShapeDtypeStruct((Bd, R, G, HEAD_DIM, LANES), F32),
        compiler_params=_cp("parallel", "arbitrary"), name="dsa_sample_attention",
    )(page_table, *([kv_t] * pp), mask4, qexp, new_cols, mnew3)


def _dil_sample_kernel(*refs, P):
    ng = len(C_GROUPS)
    buf_refs, new_refs, q_refs = refs[0:ng], refs[ng:2 * ng], refs[2 * ng:3 * ng]
    nbuf_refs, o_ref = refs[3 * ng:4 * ng], refs[4 * ng]
    res = []
    for (w, r), buf_ref, new_ref, q_ref, nbuf_ref in zip(C_GROUPS, buf_refs, new_refs, q_refs, nbuf_refs):
        buf = buf_ref[...]
        new = new_ref[...]
        W = buf.shape[2]
        nbuf_ref[...] = _slide_lanes(buf, new)
        q = q_ref[...] * QK_SCALE
        qw = jnp.concatenate([q] * (W // LANES), axis=1)
        s_old = jnp.sum(buf[0] * qw, axis=0, keepdims=True)
        dist = W - _iota((1, W), 1)
        valid = ((dist & (r - 1)) == 0) & (dist <= P)
        s_old = jnp.where(valid, s_old, NEG_INF)
        s_new = jnp.sum(new[0] * q, axis=0, keepdims=True)
        m = jnp.maximum(jnp.max(s_old, axis=-1, keepdims=True), s_new)
        e_old = jnp.where(valid, jnp.exp(s_old - m[:, 0:1]), 0.0)
        e_new = jnp.exp(s_new - m)
        den = jnp.sum(e_old, axis=-1, keepdims=True) + e_new
        o = (jnp.sum(buf[1] * e_old, axis=-1, keepdims=True) + new[1] * e_new) / jnp.maximum(den, TINY)
        res.append((o, m, den))
    m_all = functools.reduce(jnp.maximum, [m for _, m, _ in res])
    wts = [den * jnp.exp(m - m_all) for _, m, den in res]
    tot = functools.reduce(lambda a, b: a + b, wts)
    o_ref[...] = functools.reduce(lambda a, b: a + b, [(wt / tot) * o for wt, (o, _, _) in zip(wts, res)])


def _dil_sample(bufs_t, layer, news_cols, qs_cols, P):
    Bd, H = qs_cols[0].shape[0], qs_cols[0].shape[1]
    in_specs, out_specs, out_shape = [], [], []
    for buf in bufs_t:
        W = buf.shape[5]
        in_specs.append(pl.BlockSpec((None, None, 2, None, HEAD_DIM, W), lambda b, h: (layer, b, 0, h, 0, 0)))
        out_specs.append(pl.BlockSpec((None, 2, None, HEAD_DIM, W), lambda b, h: (b, 0, h, 0, 0)))
        out_shape.append(jax.ShapeDtypeStruct((Bd, 2, H, HEAD_DIM, W), F32))
    in_specs += [pl.BlockSpec((None, 2, None, HEAD_DIM, LANES), lambda b, h: (b, 0, h, 0, 0))] * len(bufs_t)
    in_specs += [pl.BlockSpec((None, None, HEAD_DIM, LANES), lambda b, h: (b, h, 0, 0))] * len(bufs_t)
    out_specs.append(pl.BlockSpec((None, None, HEAD_DIM, LANES), lambda b, h: (b, h, 0, 0)))
    out_shape.append(jax.ShapeDtypeStruct((Bd, H, HEAD_DIM, LANES), F32))
    return pl.pallas_call(
        functools.partial(_dil_sample_kernel, P=P), grid=(Bd, H),
        in_specs=in_specs, out_specs=out_specs, out_shape=out_shape,
        compiler_params=_cp("parallel", "parallel"), name="dilated_sample",
    )(*bufs_t, *news_cols, *qs_cols)


def _fox_sample_kernel(pt_ref, *refs, pp, nsteps):
    pages, lfs = refs[:pp], refs[pp:2 * pp]
    q_ref, new_ref, lfnew_ref, o_ref, m_ref, l_ref, acc_ref, carry_ref = refs[2 * pp:]
    s_id = pl.program_id(1)
    L = new_ref.shape[1] // 2
    q = q_ref[0, 0] * QK_SCALE

    @pl.when(s_id == 0)
    def _():
        _tdec_init(m_ref, l_ref, acc_ref)
        new = new_ref[0]
        _tdec_update(new[0:L], new[L:2 * L], q, None, _iota((1, LANES), 1) == 0, m_ref, l_ref, acc_ref, 0)
        carry_ref[...] = lfnew_ref[0]

    later = _onehot(_iota((PAGE, PAGE), 0) > _iota((PAGE, PAGE), 1))
    for pi in range(pp):
        kv = pages[pi][...]
        lf = lfs[pi][...]
        carry = carry_ref[...]
        bias = _dotx(lf, later, 3) + carry
        carry_ref[...] = carry + jnp.sum(lf, axis=-1, keepdims=True)
        _tdec_update(kv[0:L], kv[L:2 * L], q, bias, None, m_ref, l_ref, acc_ref, 0)

    @pl.when(s_id == nsteps - 1)
    def _():
        o_ref[0, 0] = _tdec_finish(l_ref, acc_ref, 0)


def _fox_sample(kv_t, lf_t, layer, page_table, qexp, new_cols, lfnew3, pp=8):
    Bd, NP = page_table.shape
    rows = kv_t.shape[2]
    H = lf_t.shape[2]
    nsteps = NP // pp
    rev = lambda b, s, pi, pt: pt[b, NP - 1 - (s * pp + pi)]
    per_b = lambda shape: pl.BlockSpec((1,) + shape, lambda b, s, pt: (b,) + (0,) * len(shape))
    grid_spec = pltpu.PrefetchScalarGridSpec(
        num_scalar_prefetch=1, grid=(Bd, nsteps),
        in_specs=_page_specs(pp, (None, None, rows, PAGE), layer, rev) + _page_specs(pp, (None, None, H, PAGE), layer, rev)
        + [per_b((1, rows // 2, LANES)), per_b((rows, LANES)), per_b((H, LANES))],
        out_specs=per_b((1, H, HEAD_DIM, LANES)),
        scratch_shapes=_tdec_scratch(1, H) + [pltpu.VMEM((H, LANES), F32)])
    return pl.pallas_call(
        functools.partial(_fox_sample_kernel, pp=pp, nsteps=nsteps), grid_spec=grid_spec,
        out_shape=jax.ShapeDtypeStruct((Bd, 1, H, HEAD_DIM, LANES), F32),
        compiler_params=_cp("parallel", "arbitrary"), name="fox_sample",
    )(page_table, *([kv_t] * pp), *([lf_t] * pp), qexp, new_cols, lfnew3)


def _cols(x, lead):
    Bd = x.shape[0]
    return jnp.broadcast_to(x.reshape((Bd,) + lead + (1,)), (Bd,) + lead + (LANES,))


def _rows_minor(cache):
    n = cache.ndim
    return jnp.moveaxis(cache, n - 4, n - 1)


def _rows_major(x):
    n = x.ndim
    return jnp.moveaxis(x, n - 1, n - 4)


def kernel(x_prompt, x_sample, cache_a_cmp, cache_a_sel, cache_a_win, cache_b_kv, cache_b_kidx, cache_c_win0, cache_c_win1, cache_c_win2, cache_d_kv, cache_d_logf, page_table, c_prompt, c_sample, ada_w, ada_b, norm_g, final_g, a_w_in, a_w_out, a_cmp_wk, a_cmp_wv, b_w_in, b_w_out, c_w_in, c_w_out, d_w_in, d_w_out, d_f_bias):
    B, T, D = x_prompt.shape
    Bd = x_sample.shape[0]
    depth = ada_w.shape[0]
    NP = page_table.shape[1]
    P = NP * PAGE
    pool = cache_b_kidx.shape[1]
    assert x_sample.shape[1] == 1 and D == 1024 and P % A_SEL_LEN == 0
    assert cache_a_win.shape[2] == A_WINDOW
    assert all(b.shape[2] == w for b, (w, _) in zip((cache_c_win0, cache_c_win1, cache_c_win2), C_GROUPS))
    H = D // HEAD_DIM
    R = H // A_KV
    tm_p = 256
    tpb = T // tm_p

    mod = _mod_all(jnp.concatenate([c_prompt, c_sample], axis=0), ada_w, ada_b)
    pos_p = jnp.arange(T, dtype=I32)
    pos_s = jnp.full((Bd,), P, I32)
    tab_p, ttab_p = _rope_tables(pos_p), _rope_tables_t(pos_p)
    tab_s = _rope_tables(pos_s)
    xp = x_prompt.reshape(B * T, D)
    xs = x_sample.reshape(Bd, D)
    c_bufs = (cache_c_win0, cache_c_win1, cache_c_win2)
    st = {}
    put = lambda name, val: st.setdefault(name, []).append(val)
    rows_of = lambda t, c, g: _rows_major(t.reshape(t.shape[0], c, g, HEAD_DIM, t.shape[2]))
    yp = ys = None
    for i in range(depth):
        kind, li = i % 4, i // 4
        shift_p, scale_p, gate_p = [mod[i, :B, j * D:(j + 1) * D].reshape(B, 1, D) for j in range(3)]
        shift_s, scale_s, gate_s = [mod[i, B:, j * D:(j + 1) * D].reshape(1, Bd, D) for j in range(3)]
        proj_p = lambda w, segs: _proj(xp, scale_p, shift_p, norm_g[i], w, segs, tab_p, ttab_p, tpb, tm_p, PASSES_PROMPT_PROJ)
        proj_s = lambda w, segs: _proj(xs, scale_s, shift_s, norm_g[i], w, segs, tab_s, None, 1, Bd, PASSES_SELECTIVE)
        if kind == 0:
            w = a_w_in[li]
            gw = A_KV * HEAD_DIM
            q, qr, gl, z, cmp_t, sel_t, win_t = proj_p(w, [
                ("rm", 0, D, ("n", "r")), ("rm", D + 6 * gw, 3 * H, ("n",)), ("rm", D + 6 * gw + 3 * H, D, ("n",)),
                ("t", D, 2 * gw, "t"), ("t", D + 2 * gw, 2 * gw, "trk"), ("t", D + 4 * gw, 2 * gw, "trk")])
            wlo, whi = _cmp_taps(a_cmp_wk[li], a_cmp_wv[li], A_KV)
            kvb = _nsa_compress_prompt(cmp_t, wlo, whi)
            o = _nsa_prompt(q.reshape(B, T, D), qr.reshape(B, T, D), kvb, sel_t, win_t, gl.reshape(B, T, LANES),
                            PASSES_NSA_FLASH)
            op, zp, wo = o.reshape(B * T, D), z, a_w_out[li]
            put("a_cmp_p", rows_of(cmp_t, 2, A_KV))
            put("a_sel_p", rows_of(sel_t, 2, A_KV))
            put("a_win_p", rows_of(win_t[:, :, T - min(A_WINDOW, T):], 2, A_KV))

            q, qr, cmp_, sel, win, gl, z = proj_s(w, [
                ("rm", 0, D, ("n", "r")), ("rm", D, 2 * gw, ("n",)), ("rm", D + 2 * gw, 2 * gw, ("rk",)),
                ("rm", D + 4 * gw, 2 * gw, ("rk",)), ("rm", D + 6 * gw, 3 * H, ("n",)), ("rm", D + 6 * gw + 3 * H, D, ("n",))])
            cmp_t6 = _rows_minor(cache_a_cmp)
            lohi = _nsa_compress_sample(cmp_t6.reshape(cmp_t6.shape[0], pool, 2 * gw, PAGE), li, page_table, wlo, whi)
            tile_g = lambda t: jnp.tile(t.reshape(Bd, H, HEAD_DIM), (1, 1, A_KV))
            ocmp, idx = _nsa_sample_cmp(lohi, tile_g(q), P)
            win_t6 = _rows_minor(cache_a_win)
            nwin, owin = _nsa_sample_win(win_t6[li].reshape(Bd, 2 * gw, A_WINDOW), _cols(win, (2 * gw,)), tile_g(qr))
            nk = min(A_SEL_TOPK, -(-(P + 1) // A_SEL_LEN))
            osel = _nsa_sample_sel(_rows_minor(cache_a_sel), li, page_table, idx[:, :A_KV, :nk].reshape(Bd, A_KV * nk),
                                   _cols(qr, (A_KV, R * HEAD_DIM)), _cols(sel, (2, A_KV, HEAD_DIM)), P)
            o = _nsa_sample_merge(gl[:, :3 * H].reshape(Bd, H, 3), ocmp, osel[..., 0].reshape(Bd, H, HEAD_DIM), owin)
            os_, zs = o.reshape(Bd, D), z
            put("a_cmp_s", cmp_.reshape(Bd, 1, 2, A_KV, HEAD_DIM))
            put("a_sel_s", sel.reshape(Bd, 1, 2, A_KV, HEAD_DIM))
            put("a_win_s", _rows_major(nwin.reshape(Bd, 2, A_KV, HEAD_DIM, A_WINDOW)))
        elif kind == 1:
            w = b_w_in[li]
            gw = A_KV * HEAD_DIM
            c_qi = D + 2 * gw
            c_ki = c_qi + B_IDX_HEADS * B_IDX_DIM
            c_wi = c_ki + B_IDX_DIM
            c_z = c_wi + B_IDX_HEADS
            q, qi, wi, z, kv_t, ki_t = proj_p(w, [
                ("rm", 0, D, ("r",)), ("rm", c_qi, c_ki - c_qi, ("r",)), ("rm", c_wi, B_IDX_HEADS, ("n",)), ("rm", c_z, D, ("n",)),
                ("t", D, 2 * gw, "trk"), ("t", c_ki, B_IDX_DIM, "tr")])
            o = _dsa_prompt(q.reshape(B, T, D), qi.reshape(B, T, c_ki - c_qi), wi.reshape(B, T, LANES), kv_t, ki_t,
                            min(B_TOPK_MAX, T // 4), PASSES_SMOOTH, PASSES_PROMPT_INDEXER)
            op, zp, wo = o.reshape(B * T, D), z, b_w_out[li]
            put("b_kv_p", rows_of(kv_t, 2, A_KV))
            put("b_kidx_p", jnp.swapaxes(ki_t, 1, 2))

            q, kv, qi, ki, wi, z = proj_s(w, [
                ("rm", 0, D, ("r",)), ("rm", D, 2 * gw, ("rk",)), ("rm", c_qi, c_ki - c_qi, ("r",)), ("rm", c_ki, B_IDX_DIM, ("r",)),
                ("rm", c_wi, B_IDX_HEADS, ("n",)), ("rm", c_z, D, ("n",))])
            qi3 = qi.reshape(Bd, B_IDX_HEADS, B_IDX_DIM)
            wi3 = wi[:, :B_IDX_HEADS].reshape(Bd, B_IDX_HEADS, 1)
            ki_new = ki[:, :B_IDX_DIM]
            sc = _dsa_sample_scores(jnp.swapaxes(cache_b_kidx, 2, 3), li, page_table, qi3, wi3)
            mask = _dsa_sample_select(sc.reshape(Bd, P), qi3, wi3, ki_new.reshape(Bd, 1, B_IDX_DIM),
                                      min(B_TOPK_MAX, (P + 1) // 4))
            kv_t6 = _rows_minor(cache_b_kv)
            qexp = _cols(q.reshape(Bd, A_KV, R, HEAD_DIM).transpose(0, 2, 1, 3).reshape(Bd, R, gw), (R, gw))
            o = _dsa_sample_attn(kv_t6.reshape(kv_t6.shape[0], pool, 2 * gw, PAGE), li, page_table,
                                 mask[:, :P].reshape(Bd, NP, 1, PAGE), qexp, _cols(kv, (2 * gw,)),
                                 jnp.broadcast_to(mask[:, P:P + 1], (Bd, LANES)).reshape(Bd, 1, LANES))
            os_, zs = o[..., 0].transpose(0, 2, 1, 3).reshape(Bd, D), z
            put("b_kv_s", kv.reshape(Bd, 1, 2, A_KV, HEAD_DIM))
            put("b_kidx_s", ki_new.reshape(Bd, 1, B_IDX_DIM))
        elif kind == 2:
            w = c_w_in[li]
            Wd = C_HEADS * HEAD_DIM
            ng = len(C_GROUPS)
            res = proj_p(w, [("rm", 3 * g * Wd, Wd, ("r",)) for g in range(ng)] + [("rm", 3 * ng * Wd, Wd, ("n",))]
                         + [("t", (3 * g + 1) * Wd, 2 * Wd, "trk") for g in range(ng)])
            qs, z, kvs_t = res[0:ng], res[ng], res[ng + 1:]
            o = _dil_prompt([t.reshape(B, T, Wd) for t in qs], kvs_t, PASSES_SMOOTH)
            op, zp, wo = o.reshape(B * T, Wd), z, c_w_out[li]
            for g, (wg, _) in enumerate(C_GROUPS):
                put("c_win%d_p" % g, rows_of(kvs_t[g][:, :, T - min(wg, T):], 2, C_HEADS))

            res = proj_s(w, [seg for g in range(ng) for seg in (("rm", 3 * g * Wd, Wd, ("r",)), ("rm", (3 * g + 1) * Wd, 2 * Wd, ("rk",)))]
                         + [("rm", 3 * ng * Wd, Wd, ("n",))])
            qs, kvs, z = res[0:2 * ng:2], res[1:2 * ng:2], res[2 * ng]
            outs = _dil_sample([_rows_minor(b) for b in c_bufs], li, [_cols(t, (2, C_HEADS, HEAD_DIM)) for t in kvs],
                               [_cols(t, (C_HEADS, HEAD_DIM)) for t in qs], P)
            os_, zs = outs[ng][..., 0].reshape(Bd, Wd), z
            for g in range(ng):
                put("c_win%d_s" % g, _rows_major(outs[g]))
        else:
            w = d_w_in[li]
            q, z, kv_t, f_t = proj_p(w, [("rm", 0, D, ("n",)), ("rm", 3 * D + H, D, ("n",)), ("t", D, 2 * D, "t"), ("t", 3 * D, H, "t")])
            fb = d_f_bias[li].reshape(H, 1)
            lf_t, c_t = _fox_logf(f_t, fb)
            o = _fox_prompt(q.reshape(B, T, D), kv_t, jnp.swapaxes(c_t, 1, 2), c_t, PASSES_SMOOTH)
            op, zp, wo = o.reshape(B * T, D), z, d_w_out[li]
            put("d_kv_p", rows_of(kv_t, 2, H))
            put("d_logf_p", jnp.swapaxes(lf_t, 1, 2))

            q, kv, f, z = proj_s(w, [("rm", 0, D, ("n",)), ("rm", D, 2 * D, ("n",)), ("rm", 3 * D, H, ("n",)), ("rm", 3 * D + H, D, ("n",))])
            lf = _fox_logf(f[:, :H].T.reshape(1, H, Bd), fb)[0][0].T
            kv_t6 = _rows_minor(cache_d_kv)
            o = _fox_sample(kv_t6.reshape(kv_t6.shape[0], pool, 2 * D, PAGE), jnp.swapaxes(cache_d_logf, 2, 3), li, page_table,
                            _cols(q, (1, D)), _cols(kv, (2 * D,)), _cols(lf, (H,)))
            os_, zs = o[..., 0].reshape(Bd, D), z
            put("d_kv_s", kv.reshape(Bd, 1, 2, H, HEAD_DIM))
            put("d_logf_s", lf.reshape(Bd, 1, H))
        fg = final_g if i == depth - 1 else None
        xp, yp = _gated_out(op, zp, wo, xp, gate_p, tpb, tm_p, PASSES_PROMPT_PROJ, fg)
        xs, ys = _gated_out(os_, zs, wo, xs, gate_s, 1, Bd, PASSES_SELECTIVE, fg)
    ns = {n: jnp.stack(v) for n, v in st.items()}
    names = ("a_cmp_p", "a_cmp_s", "a_sel_p", "a_sel_s", "a_win_p", "a_win_s", "b_kv_p", "b_kv_s", "b_kidx_p", "b_kidx_s",
             "c_win0_p", "c_win0_s", "c_win1_p", "c_win1_s", "c_win2_p", "c_win2_s", "d_kv_p", "d_kv_s", "d_logf_p", "d_logf_s")
    return (yp.reshape(B, T, D), ys.reshape(Bd, 1, D)) + tuple(ns[n] for n in names)
```

```python
import functools
import math

import jax
import jax.numpy as jnp
from jax import lax
from jax.experimental import pallas as pl
from jax.experimental.pallas import tpu as pltpu

F32 = jnp.float32
BF16 = jnp.bfloat16
I32 = jnp.int32

HEAD_DIM = 64
ROPE_DIMS = HEAD_DIM // 4
ROPE_HALF = ROPE_DIMS // 2
ROPE_THETA = 500000.0
NORM_EPS = 1e-6
NEG_INF = -1e30
TINY = 1e-30
PAGE = 128
LANES = 128
INT_MIN = -2 ** 31

A_KV = 4
A_CMP_STRIDE = 16
A_CMP_LEN = 32
A_SEL_LEN = 64
A_SEL_TOPK = 16
A_WINDOW = 512
A_FORCED = 1e4
B_IDX_HEADS = 8
B_IDX_DIM = 64
B_TOPK_MAX = 256
C_GROUPS = ((128, 1), (512, 4), (2048, 16))
C_HEADS = 8
QK_SCALE = HEAD_DIM ** -0.5

NN = ((1,), (0,))
NT = ((1,), (1,))

VMEM_LIMIT = 56 * 1024 * 1024
PROJ_COLS_PER_CALL = 2048

PASSES_SELECTIVE = 3
PASSES_SMOOTH = 1
PASSES_NSA_FLASH = 1
PASSES_PROMPT_PROJ = 1
PASSES_PROMPT_INDEXER = 1


def _cp(*sem):
    return pltpu.CompilerParams(dimension_semantics=sem, vmem_limit_bytes=VMEM_LIMIT)


def _dotf(a, b, dims=NN):
    return lax.dot_general(a, b, (dims, ((), ())), preferred_element_type=F32)


def _split(a, terms):
    out = []
    for _ in range(terms - 1):
        h = a.astype(BF16)
        out.append(h)
        a = a - h.astype(F32)
    out.append(a.astype(BF16))
    return out


def _dot3(a, b, dims=NN):
    ah, al = _split(a, 2)
    bh, bl = _split(b, 2)
    return _dotf(ah, bh, dims) + (_dotf(ah, bl, dims) + _dotf(al, bh, dims))


def _dotp(a, b, dims, passes):
    if passes == 1:
        return _dotf(a.astype(BF16), b.astype(BF16), dims)
    return _dot3(a, b, dims)


def _dotx(a, m01, terms, dims=NN):
    acc = None
    for t in _split(a, terms):
        d = _dotf(t, m01, dims)
        acc = d if acc is None else acc + d
    return acc


def _iota(shape, axis):
    return lax.broadcasted_iota(I32, shape, axis)


def _onehot(cond):
    return jnp.where(cond, 1.0, 0.0).astype(BF16)


def _sigmoid(x):
    return 1.0 / (1.0 + jnp.exp(-x))


def _sortable_key(x):
    x = jnp.where(x == 0.0, 0.0, x)
    b = lax.bitcast_convert_type(x, I32)
    return jnp.where(b < 0, b ^ 0x7FFFFFFF, b)


def _count_ge(key, cand):
    return jnp.sum(jnp.where(key >= cand, 1.0, 0.0), axis=-1, keepdims=True)


def _kth_largest_key(key, k):
    base = jnp.where(_count_ge(key, 0) >= k, 0, INT_MIN).astype(I32)

    def body(it, base):
        cand = base | jnp.left_shift(jnp.int32(1), 30 - it)
        return jnp.where(_count_ge(key, cand) >= k, cand, base)

    return lax.fori_loop(0, 31, body, base)


def _topk_mask(key, k):
    n = key.shape[1]
    thr = _kth_largest_key(key, k)
    gt = key > thr
    eq = key == thr
    need = k - jnp.sum(jnp.where(gt, 1.0, 0.0), axis=-1, keepdims=True)
    before = _onehot(_iota((LANES, LANES), 0) < _iota((LANES, LANES), 1))
    run = jnp.zeros_like(need)
    out = []
    for c in range(n // LANES):
        sl = slice(c * LANES, (c + 1) * LANES)
        eqf = jnp.where(eq[:, sl], 1.0, 0.0)
        prior = _dotf(eqf.astype(BF16), before) + run
        out.append(gt[:, sl] | (eq[:, sl] & (prior < need)))
        run = run + jnp.sum(eqf, axis=-1, keepdims=True)
    return jnp.concatenate(out, axis=1)


def _flash_streams(streams, c_lo, c_hi, tk, passes, c_free=None):
    ones_rows = jnp.ones((8, tk), F32)

    def step(c, carry, masked):
        off = pl.multiple_of(c * tk, tk)
        out = []
        for (q, kt_ref, vt_ref, krow, vrow, mask_fn, bias_fn), (m, acc) in zip(streams, carry):
            kt = kt_ref[0, krow:krow + HEAD_DIM, pl.ds(off, tk)]
            vt = jnp.concatenate([vt_ref[0, vrow:vrow + HEAD_DIM, pl.ds(off, tk)], ones_rows], axis=0)
            s = _dotp(q, kt, NN, passes)
            if bias_fn is not None:
                s = s + bias_fn(c)
            if masked:
                s = mask_fn(c, s)
            m_new = jnp.maximum(m, jnp.max(s, axis=-1, keepdims=True))
            if passes == 1:
                pv = _dotf(jnp.exp((s - m_new).astype(BF16)), vt.astype(BF16), NT)
            else:
                pv = _dot3(jnp.exp(s - m_new), vt, NT)
            out.append((m_new, jnp.exp(m - m_new) * acc + pv))
        return tuple(out)

    carry = tuple((jnp.full((s[0].shape[0], 1), 0.1 * NEG_INF, F32), jnp.zeros((s[0].shape[0], HEAD_DIM + 8), F32))
                  for s in streams)
    if c_free is not None:
        carry = lax.fori_loop(c_lo, c_free, functools.partial(step, masked=False), carry)
        c_lo = c_free
    carry = lax.fori_loop(c_lo, c_hi, functools.partial(step, masked=True), carry)
    res = []
    for m, acc in carry:
        den = acc[:, HEAD_DIM:HEAD_DIM + 1]
        res.append((acc[:, 0:HEAD_DIM] / jnp.maximum(den, TINY), m, den))
    return res


def _flash(q, kt_ref, vt_ref, krow, vrow, c_lo, c_hi, tk, mask_fn, passes, bias_fn=None, c_free=None):
    return _flash_streams([(q, kt_ref, vt_ref, krow, vrow, mask_fn, bias_fn)], c_lo, c_hi, tk, passes, c_free)[0]


def _stack_heads(ref, first_head, n):
    return jnp.concatenate(
        [ref[0, :, (first_head + r) * HEAD_DIM:(first_head + r + 1) * HEAD_DIM] for r in range(n)], axis=0)


def _group_block_diag(qt, n_heads, n_groups):
    per = n_heads // n_groups
    shift = per.bit_length() - 1
    keep = (_iota(qt.shape, 0) >> shift) == (_iota(qt.shape, 1) >> 6)
    return jnp.where(keep, qt, 0.0), keep


def _fold_groups(x, keep, n_groups):
    x = jnp.where(keep, x, 0.0)
    return functools.reduce(lambda a, b: a + b, [x[:, g * HEAD_DIM:(g + 1) * HEAD_DIM] for g in range(n_groups)])


def _mod_kernel(c_ref, w_ref, b_ref, o_ref):
    c = c_ref[...]
    o_ref[0] = _dot3(c * _sigmoid(c), w_ref[0]) + b_ref[0]


def _mod_all(c_all, ada_w, ada_b):
    L, D, D3 = ada_w.shape
    NC = c_all.shape[0]
    tn = 1024
    return pl.pallas_call(
        _mod_kernel, grid=(L, D3 // tn),
        in_specs=[pl.BlockSpec((NC, D), lambda l, j: (0, 0)),
                  pl.BlockSpec((1, D, tn), lambda l, j: (l, 0, j)),
                  pl.BlockSpec((1, 1, tn), lambda l, j: (l, 0, j))],
        out_specs=pl.BlockSpec((1, NC, tn), lambda l, j: (l, 0, j)),
        out_shape=jax.ShapeDtypeStruct((L, NC, D3), F32),
        compiler_params=_cp("parallel", "parallel"), name="adaln_mod",
    )(c_all, ada_w, ada_b.reshape(L, 1, D3))


def _proj_kernel(*refs, rsegs, tsegs, passes):
    x_ref, sc_ref, sh_ref, g_ref = refs[0:4]
    pos = 4
    if rsegs:
        wh_ref, wl_ref, cos_ref, sn_ref, sp_ref = refs[pos:pos + 5]
        pos += 5
    if tsegs:
        wth_ref, wtl_ref, cost_ref, sint_ref = refs[pos:pos + 4]
        pos += 4
    out_refs = refs[pos:]
    x = x_ref[...]
    y = x * lax.rsqrt(jnp.mean(x * x, axis=-1, keepdims=True) + NORM_EPS) * g_ref[...]
    h = y * (1.0 + sc_ref[0]) + sh_ref[0]
    hh, hl = _split(h, 2)
    oi = 0
    for start, width, modes in rsegs:
        wh = wh_ref[:, start:start + width]
        u = _dotf(hh, wh)
        if passes == 3:
            u = u + (_dotf(hh, wl_ref[:, start:start + width]) + _dotf(hl, wh))
        for mode in modes:
            o_ref = out_refs[oi]
            oi += 1
            n_rope = {"n": 0, "r": width, "rk": width // 2}[mode]
            for c in range(width // LANES):
                sl = slice(c * LANES, (c + 1) * LANES)
                uc = u[:, sl]
                if c * LANES < n_rope:
                    uc = (uc * cos_ref[...] + pltpu.roll(uc, LANES - ROPE_HALF, 1) * sn_ref[...]
                          + pltpu.roll(uc, ROPE_HALF, 1) * sp_ref[...])
                o_ref[:, sl] = uc
    for start, width, mode in tsegs:
        wth = wth_ref[start:start + width, :]
        ut = _dotf(wth, hh, NT)
        if passes == 3:
            ut = ut + (_dotf(wth, hl, NT) + _dotf(wtl_ref[start:start + width, :], hh, NT))
        o_ref = out_refs[oi]
        oi += 1
        o_ref[0] = ut
        n_rope = {"t": 0, "tr": width, "trk": width // 2}[mode]
        for hd in range(n_rope // HEAD_DIM):
            r0 = hd * HEAD_DIM
            x1 = ut[r0:r0 + ROPE_HALF]
            x2 = ut[r0 + ROPE_HALF:r0 + ROPE_DIMS]
            o_ref[0, r0:r0 + ROPE_HALF, :] = x1 * cost_ref[...] - x2 * sint_ref[...]
            o_ref[0, r0 + ROPE_HALF:r0 + ROPE_DIMS, :] = x2 * cost_ref[...] + x1 * sint_ref[...]


def _proj(x2, scale, shift, g, w, segs, tables, ttables, tiles_per_batch, tm, passes):
    R, D = x2.shape
    NB, RB = scale.shape[0], scale.shape[1]
    Tt = R // NB
    groups, cur, cur_w = [], [], 0
    for seg in segs:
        wd = -(-seg[2] // LANES) * LANES
        if cur and cur_w + wd > PROJ_COLS_PER_CALL:
            groups.append(cur)
            cur, cur_w = [], 0
        cur.append(seg)
        cur_w += wd
    groups.append(cur)
    results = {}
    for grp in groups:
        rsegs, tsegs, rcols, tcols = [], [], [], []
        out_shapes, out_specs, keys = [], [], []
        rpos = tpos = 0
        for si, (kind, c0, wd, spec) in enumerate(grp):
            if kind == "rm":
                pw = -(-wd // LANES) * LANES
                rcols.append(w[:, c0:c0 + wd])
                if pw > wd:
                    rcols.append(jnp.zeros((D, pw - wd), F32))
                rsegs.append((rpos, pw, spec))
                rpos += pw
                for mi in range(len(spec)):
                    out_shapes.append(jax.ShapeDtypeStruct((R, pw), F32))
                    out_specs.append(pl.BlockSpec((tm, pw), lambda i: (i, 0)))
                    keys.append((id(grp), si, mi))
        for si, (kind, c0, wd, spec) in enumerate(grp):
            if kind == "t":
                tcols.append(w[:, c0:c0 + wd])
                tsegs.append((tpos, wd, spec))
                tpos += wd
                out_shapes.append(jax.ShapeDtypeStruct((NB, wd, Tt), F32))
                out_specs.append(pl.BlockSpec((1, wd, tm), lambda i: (i // tiles_per_batch, 0, i % tiles_per_batch)))
                keys.append((id(grp), si, 0))
        mod_spec = pl.BlockSpec((1, RB, D), lambda i: (i // tiles_per_batch, 0, 0))
        in_specs = [pl.BlockSpec((tm, D), lambda i: (i, 0)), mod_spec, mod_spec, pl.BlockSpec((1, D), lambda i: (0, 0))]
        args = [x2, scale, shift, g.reshape(1, D)]
        if rsegs:
            wr = jnp.concatenate(rcols, axis=1)
            wh = wr.astype(BF16)
            wl = (wr - wh.astype(F32)).astype(BF16) if passes == 3 else wh
            tab_spec = pl.BlockSpec((tm, LANES), lambda i: (i % tiles_per_batch, 0))
            in_specs += [pl.BlockSpec((D, rpos), lambda i: (0, 0))] * 2 + [tab_spec] * 3
            args += [wh, wl, *tables]
        if tsegs:
            wt = jnp.concatenate(tcols, axis=1).T
            wth = wt.astype(BF16)
            wtl = (wt - wth.astype(F32)).astype(BF16) if passes == 3 else wth
            ttab_spec = pl.BlockSpec((ROPE_HALF, tm), lambda i: (0, i % tiles_per_batch))
            in_specs += [pl.BlockSpec((tpos, D), lambda i: (0, 0))] * 2 + [ttab_spec] * 2
            args += [wth, wtl, *ttables]
        res = pl.pallas_call(
            functools.partial(_proj_kernel, rsegs=tuple(rsegs), tsegs=tuple(tsegs), passes=passes), grid=(R // tm,),
            in_specs=in_specs, out_specs=out_specs, out_shape=out_shapes,
            compiler_params=_cp("parallel"), name="norm_mod_proj",
        )(*args)
        for key, r in zip(keys, res):
            results[key] = r
    outs = []
    for grp in groups:
        for si, (kind, c0, wd, spec) in enumerate(grp):
            for mi in range(len(spec) if kind == "rm" else 1):
                outs.append(results[(id(grp), si, mi)])
    return outs


def _rope_angles(pos):
    inv = ROPE_THETA ** (-(jnp.arange(ROPE_HALF, dtype=F32) / ROPE_HALF))
    ang = pos.astype(F32)[:, None] * inv[None, :]
    return jnp.cos(ang), jnp.sin(ang)


def _rope_tables(pos):
    cos, sin = _rope_angles(pos)
    R = pos.shape[0]
    one = jnp.ones((R, HEAD_DIM - ROPE_DIMS), F32)
    zero = jnp.zeros((R, HEAD_DIM - ROPE_DIMS), F32)
    zh = jnp.zeros((R, ROPE_HALF), F32)
    cos_h = jnp.concatenate([cos, cos, one], axis=1)
    sn_h = jnp.concatenate([-sin, zh, zero], axis=1)
    sp_h = jnp.concatenate([zh, sin, zero], axis=1)
    rep = LANES // HEAD_DIM
    return tuple(jnp.tile(t, (1, rep)) for t in (cos_h, sn_h, sp_h))


def _rope_tables_t(pos):
    cos, sin = _rope_angles(pos)
    return cos.T, sin.T


def _out_kernel(o_ref, z_ref, wh_ref, wl_ref, x_ref, gate_ref, *rest, final, passes):
    z = z_ref[...]
    y = o_ref[...] * (z * _sigmoid(z))
    yh, yl = _split(y, 2)
    wh = wh_ref[...]
    r = _dotf(yh, wh)
    if passes == 3:
        r = r + (_dotf(yh, wl_ref[...]) + _dotf(yl, wh))
    xn = x_ref[...] + gate_ref[0] * r
    if final:
        fg_ref, xo_ref, yo_ref = rest
        yo_ref[...] = xn * lax.rsqrt(jnp.mean(xn * xn, axis=-1, keepdims=True) + NORM_EPS) * fg_ref[...]
    else:
        (xo_ref,) = rest
    xo_ref[...] = xn


def _gated_out(o2, z2, w_out, x2, gate, tiles_per_batch, tm, passes, final_g=None):
    R, W = o2.shape
    D = x2.shape[1]
    RB = gate.shape[1]
    wh = w_out.astype(BF16)
    wl = (w_out - wh.astype(F32)).astype(BF16) if passes == 3 else wh
    final = final_g is not None
    in_specs = [pl.BlockSpec((tm, W), lambda i: (i, 0)), pl.BlockSpec((tm, W), lambda i: (i, 0)),
                pl.BlockSpec((W, D), lambda i: (0, 0)), pl.BlockSpec((W, D), lambda i: (0, 0)),
                pl.BlockSpec((tm, D), lambda i: (i, 0)),
                pl.BlockSpec((1, RB, D), lambda i: (i // tiles_per_batch, 0, 0))]
    args = [o2, z2, wh, wl, x2, gate]
    out_shape = [jax.ShapeDtypeStruct((R, D), F32)]
    out_specs = [pl.BlockSpec((tm, D), lambda i: (i, 0))]
    if final:
        in_specs.append(pl.BlockSpec((1, D), lambda i: (0, 0)))
        args.append(final_g.reshape(1, D))
        out_shape.append(jax.ShapeDtypeStruct((R, D), F32))
        out_specs.append(pl.BlockSpec((tm, D), lambda i: (i, 0)))
    res = pl.pallas_call(
        functools.partial(_out_kernel, final=final, passes=passes), grid=(R // tm,),
        in_specs=in_specs, out_specs=out_specs, out_shape=out_shape,
        compiler_params=_cp("parallel"), name="gated_out_proj",
    )(*args)
    return res if final else (res[0], None)


def _cmp_taps(wk, wv, groups):
    def table(w):
        t = jnp.tile(w.T, (1, LANES // A_CMP_STRIDE))
        return jnp.tile(t, (groups, 1))
    S = A_CMP_STRIDE
    lo = jnp.concatenate([table(wk[:S]), table(wv[:S])], axis=0)
    hi = jnp.concatenate([table(wk[S:]), table(wv[S:])], axis=0)
    return lo, hi


def _chunk_sum_matrix(first_col):
    shift = A_CMP_STRIDE.bit_length() - 1
    return _onehot((_iota((LANES, LANES), 0) >> shift) + first_col == _iota((LANES, LANES), 1))


def _cmp_kernel(x_ref, wlo_ref, whi_ref, o_ref, *, T):
    rows = x_ref.shape[1]
    per = LANES // A_CMP_STRIDE
    lo = jnp.zeros((rows, LANES), F32)
    hi = jnp.zeros((rows, LANES), F32)
    for c in range(T // LANES):
        xc = x_ref[0, :, c * LANES:(c + 1) * LANES]
        m = _chunk_sum_matrix(c * per)
        lo = lo + _dotx(xc * wlo_ref[...], m, 3)
        hi = hi + _dotx(xc * whi_ref[...], m, 3)
    o_ref[0] = lo + pltpu.roll(hi, LANES - 1, 1)


def _nsa_compress_prompt(cmp_t, wlo, whi):
    B, rows, T = cmp_t.shape
    assert T // A_CMP_STRIDE <= LANES
    return pl.pallas_call(
        functools.partial(_cmp_kernel, T=T), grid=(B,),
        in_specs=[pl.BlockSpec((1, rows, T), lambda b: (b, 0, 0)), pl.BlockSpec((rows, LANES), lambda b: (0, 0)),
                  pl.BlockSpec((rows, LANES), lambda b: (0, 0))],
        out_specs=pl.BlockSpec((1, rows, LANES), lambda b: (b, 0, 0)),
        out_shape=jax.ShapeDtypeStruct((B, rows, LANES), F32),
        compiler_params=_cp("parallel"), name="nsa_compress",
    )(cmp_t, wlo, whi)


def _nsa_prompt_kernel(q_ref, qr_ref, kvb_ref, sel_ref, win_ref, gl_ref, wt_ref, o_ref, *, tq, tk, T, flash_passes):
    i = pl.program_id(1)
    t0 = i * tq
    R = 4
    nb = kvb_ref.shape[2]
    ncb = T // A_CMP_STRIDE - 1
    nsb = T // A_SEL_LEN
    gw = A_KV * HEAD_DIM
    gates = _sigmoid(gl_ref[0])

    n_io = _iota((R * tq, nb), 1)
    t_c = (_iota((R * tq, nb), 0) & (tq - 1)) + t0
    cmask = (n_io * A_CMP_STRIDE + (A_CMP_LEN - 1) <= t_c) & (n_io < ncb)

    ratio = A_SEL_LEN // A_CMP_STRIDE
    back = A_CMP_LEN // A_CMP_STRIDE - 1
    mn, mj = _iota((nb, LANES), 0), _iota((nb, LANES), 1)
    imp_mat = _onehot((mn >= mj * ratio - back) & (mn <= mj * ratio + ratio - 1) & (mn < ncb) & (mj < nsb))

    j_io = _iota((tq, LANES), 1)
    qb = (_iota((tq, LANES), 0) + t0) >> 6
    allowed = (j_io <= qb) & (j_io < nsb)
    forced = (j_io == 0) | (j_io == qb)
    nsr = -(-nsb // 8) * 8
    jt_io = _iota((nsr, tq), 0)
    allowed_t = (jt_io <= ((_iota((nsr, tq), 1) + t0) >> 6)) & (jt_io < nsb)

    dist = (_iota((R * tq, tk), 0) & (tq - 1)) + t0 - _iota((R * tq, tk), 1)
    c_hi = (t0 + tq + tk - 1) // tk
    w_lo = jnp.maximum(t0 - (A_WINDOW - 1), 0) // tk

    def win_mask(c, s):
        bias = wt_ref[:, pl.ds(pl.multiple_of(c * tk - t0 + T, LANES), tk)]
        return s + jnp.concatenate([bias] * R, axis=0)

    def group_front(g):
        q4 = _stack_heads(q_ref, R * g, R) * QK_SCALE
        kb = kvb_ref[0, g * HEAD_DIM:(g + 1) * HEAD_DIM, :]
        vb = kvb_ref[0, gw + g * HEAD_DIM:gw + (g + 1) * HEAD_DIM, :]
        s = jnp.where(cmask, _dot3(q4, kb, NN), NEG_INF)
        m = jnp.max(s, axis=-1, keepdims=True)
        e = jnp.where(cmask, jnp.exp(s - m), 0.0)
        p = e / jnp.maximum(jnp.sum(e, axis=-1, keepdims=True), TINY)
        o_cmp = _dot3(p, vb, NT)
        pg = p[0:tq] + p[tq:2 * tq] + p[2 * tq:3 * tq] + p[3 * tq:4 * tq]
        imp = _dotx(pg, imp_mat, 3)
        score = jnp.where(allowed, jnp.where(forced, A_FORCED, imp), -1.0)
        score = jnp.where(j_io < nsb, score, -2.0)
        score_t = score.T[0:nsr]
        rank = jnp.zeros((nsr, tq), F32)
        for j2 in range(nsb):
            other = score_t[j2:j2 + 1, :]
            rank = rank + jnp.where((other > score_t) | ((other == score_t) & (j2 < jt_io)), 1.0, 0.0)
        sel_t = jnp.where((rank < min(A_SEL_TOPK, nsb)) & allowed_t, 1.0, 0.0)
        sel = jnp.concatenate([sel_t, jnp.zeros((LANES - nsr, tq), F32)], axis=0).T.astype(BF16)
        sel4 = jnp.concatenate([sel] * R, axis=0)

        def sel_mask(c, s):
            blk = (_iota((LANES, tk), 1) + c * tk) >> 6
            hit = _dotf(sel4, _onehot(blk == _iota((LANES, tk), 0)))
            return jnp.where((hit > 0.5) & (dist >= c * tk), s, NEG_INF)

        return o_cmp, sel_mask

    for g0 in range(0, A_KV, 2):
        pair = (g0, g0 + 1)
        fronts = [group_front(g) for g in pair]
        q4rs = [_stack_heads(qr_ref, R * g, R) * QK_SCALE for g in pair]
        sel_res = _flash_streams(
            [(q4r, sel_ref, sel_ref, g * HEAD_DIM, gw + g * HEAD_DIM, front[1], None) for g, q4r, front in zip(pair, q4rs, fronts)],
            0, c_hi, tk, flash_passes)
        win_res = _flash_streams(
            [(q4r, win_ref, win_ref, g * HEAD_DIM, gw + g * HEAD_DIM, win_mask, None) for g, q4r in zip(pair, q4rs)],
            w_lo, c_hi, tk, flash_passes)
        for g, (o_cmp, _), (o_sel, _, _), (o_win, _, _) in zip(pair, fronts, sel_res, win_res):
            outs = []
            for r in range(R):
                h = R * g + r
                rows = slice(r * tq, (r + 1) * tq)
                outs.append(gates[:, 3 * h:3 * h + 1] * o_cmp[rows] + gates[:, 3 * h + 1:3 * h + 2] * o_sel[rows]
                            + gates[:, 3 * h + 2:3 * h + 3] * o_win[rows])
            o_ref[0, :, g * R * HEAD_DIM:(g + 1) * R * HEAD_DIM] = jnp.concatenate(outs, axis=1)


def _nsa_prompt(q3, qr3, kvb, sel_t, win_t, gl3, flash_passes, tq=256, tk=512):
    B, T, D = q3.shape
    full = lambda a: pl.BlockSpec((1,) + a.shape[1:], lambda b, i: (b, 0, 0))
    win_table = _mask_table(tq, tk, T, lambda d: (d >= 0) & (d < A_WINDOW))
    return pl.pallas_call(
        functools.partial(_nsa_prompt_kernel, tq=tq, tk=tk, T=T, flash_passes=flash_passes), grid=(B, T // tq),
        in_specs=[pl.BlockSpec((1, tq, D), lambda b, i: (b, i, 0)), pl.BlockSpec((1, tq, D), lambda b, i: (b, i, 0)),
                  full(kvb), full(sel_t), full(win_t), pl.BlockSpec((1, tq, LANES), lambda b, i: (b, i, 0)),
                  pl.BlockSpec(win_table.shape, lambda b, i: (0, 0), pipeline_mode=pl.Buffered(1))],
        out_specs=pl.BlockSpec((1, tq, D), lambda b, i: (b, i, 0)),
        out_shape=jax.ShapeDtypeStruct((B, T, D), F32),
        compiler_params=_cp("parallel", "arbitrary"), name="nsa_prompt",
    )(q3, qr3, kvb, sel_t, win_t, gl3, win_table)


def _dsa_prompt_kernel(q_ref, qi_ref, wi_ref, kv_ref, ki_ref, o_ref, ch_ref, *, tq, tk, T, topk, passes, idx_passes):
    i = pl.program_id(1)
    t0 = i * tq
    R = 4
    gw = kv_ref.shape[1] // 2
    c_hi = (t0 + tq + tk - 1) // tk
    assert tk >= topk

    def select_keys(width):
        wi = wi_ref[0] * (B_IDX_HEADS * B_IDX_DIM) ** -0.5
        ki = ki_ref[0, :, 0:width]
        score = jnp.zeros((tq, width), F32)
        for h in range(B_IDX_HEADS):
            logits = _dotp(qi_ref[0, :, h * B_IDX_DIM:(h + 1) * B_IDX_DIM], ki, NN, idx_passes)
            score = score + wi[:, h:h + 1] * jnp.maximum(logits, 0.0)
        causal = _iota((tq, width), 1) <= _iota((tq, width), 0) + t0
        score = jnp.where(causal, score, NEG_INF)
        ch_ref[:, 0:width] = jnp.where(_topk_mask(_sortable_key(score), topk) & causal, 0.0, NEG_INF)

    for nc in range(1, T // tk + 1):
        pl.when(c_hi == nc)(functools.partial(select_keys, nc * tk))

    def mask_fn(c, s):
        picked = ch_ref[:, pl.ds(pl.multiple_of(c * tk, tk), tk)]
        return s + jnp.concatenate([picked] * R, axis=0)

    for g0 in range(0, gw // HEAD_DIM, 2):
        streams = [(_stack_heads(q_ref, R * g, R) * QK_SCALE, kv_ref, kv_ref, g * HEAD_DIM, gw + g * HEAD_DIM, mask_fn, None)
                   for g in (g0, g0 + 1)]
        for g, (o, _, _) in zip((g0, g0 + 1), _flash_streams(streams, 0, c_hi, tk, passes)):
            o_ref[0, :, g * R * HEAD_DIM:(g + 1) * R * HEAD_DIM] = jnp.concatenate(
                [o[r * tq:(r + 1) * tq] for r in range(R)], axis=1)


def _dsa_prompt(q3, qi3, wi3, kv_t, ki_t, topk, passes, idx_passes, tq=256, tk=512):
    B, T, D = q3.shape
    full = lambda a: pl.BlockSpec((1,) + a.shape[1:], lambda b, i: (b, 0, 0))
    blk = lambda w: pl.BlockSpec((1, tq, w), lambda b, i: (b, i, 0))
    return pl.pallas_call(
        functools.partial(_dsa_prompt_kernel, tq=tq, tk=tk, T=T, topk=topk, passes=passes, idx_passes=idx_passes),
        grid=(B, T // tq),
        in_specs=[blk(D), blk(qi3.shape[2]), blk(LANES), full(kv_t), full(ki_t)],
        out_specs=blk(D), out_shape=jax.ShapeDtypeStruct((B, T, D), F32),
        scratch_shapes=[pltpu.VMEM((tq, T), F32)],
        compiler_params=_cp("parallel", "arbitrary"), name="dsa_prompt",
    )(q3, qi3, wi3, kv_t, ki_t)


def _dil_prompt_kernel(*refs, tq, tk, T, passes):
    ng = len(C_GROUPS)
    q_refs, k_refs, v_refs, mt_ref, o_ref = refs[0:ng], refs[ng:2 * ng], refs[2 * ng:3 * ng], refs[3 * ng], refs[3 * ng + 1]
    i = pl.program_id(2)
    t0 = i * tq
    c_hi = (t0 + tq + tk - 1) // tk
    nhl = LANES // HEAD_DIM
    per_group = []
    for g, ((w, r), q_ref, k_ref, v_ref) in enumerate(zip(C_GROUPS, q_refs, k_refs, v_refs)):
        def mask_fn(c, s, g=g):
            return s + mt_ref[g, :, pl.ds(pl.multiple_of(c * tk - t0 + T, LANES), tk)]

        c_lo = jnp.maximum(t0 - w, 0) // tk
        streams = [(q_ref[0, :, hh * HEAD_DIM:(hh + 1) * HEAD_DIM] * QK_SCALE, k_ref, v_ref, hh * HEAD_DIM, hh * HEAD_DIM,
                    mask_fn, None) for hh in range(nhl)]
        per_group.append(_flash_streams(streams, c_lo, c_hi, tk, passes))
    outs = []
    for hh in range(nhl):
        res = [grp[hh] for grp in per_group]
        m_all = functools.reduce(jnp.maximum, [m for _, m, _ in res])
        wts = [den * jnp.exp(m - m_all) for _, m, den in res]
        tot = functools.reduce(lambda a, b: a + b, wts)
        outs.append(functools.reduce(lambda a, b: a + b, [(wt / tot) * o for wt, (o, _, _) in zip(wts, res)]))
    o_ref[0] = jnp.concatenate(outs, axis=1)


def _mask_table(tq, tk, T, valid_fn):
    d = jnp.arange(tq, dtype=I32)[:, None] - (jnp.arange(T + tq + tk, dtype=I32)[None, :] - T)
    return jnp.where(valid_fn(d), 0.0, NEG_INF).astype(F32)


def _dil_prompt(qs, kvs_t, passes, tq=512, tk=512):
    B, T, W = qs[0].shape
    hp = W // LANES
    qspec = pl.BlockSpec((1, tq, LANES), lambda b, h, i: (b, i, h))
    kspec = pl.BlockSpec((1, LANES, T), lambda b, h, i: (b, h, 0))
    vspec = pl.BlockSpec((1, LANES, T), lambda b, h, i: (b, hp + h, 0))
    ng = len(C_GROUPS)
    tables = jnp.stack([_mask_table(tq, tk, T, lambda d, w=w, r=r: (d >= 0) & (d <= w) & (d % r == 0)) for w, r in C_GROUPS])
    tspec = pl.BlockSpec(tables.shape, lambda b, h, i: (0, 0, 0), pipeline_mode=pl.Buffered(1))
    return pl.pallas_call(
        functools.partial(_dil_prompt_kernel, tq=tq, tk=tk, T=T, passes=passes), grid=(B, hp, T // tq),
        in_specs=[qspec] * ng + [kspec] * ng + [vspec] * ng + [tspec],
        out_specs=qspec, out_shape=jax.ShapeDtypeStruct((B, T, W), F32),
        compiler_params=_cp("parallel", "parallel", "arbitrary"), name="dilated_prompt",
    )(*qs, *kvs_t, *kvs_t, tables)


def _logf_kernel(f_ref, b_ref, lf_ref, c_ref, *, T, tc):
    x = f_ref[0] + b_ref[...]
    lf = jnp.minimum(x, 0.0) - jnp.log(1.0 + jnp.exp(-jnp.abs(x)))
    lf_ref[0] = lf
    parts = _split(lf, 3)
    for c in range(T // tc):
        upto = _onehot(_iota((T, tc), 0) <= _iota((T, tc), 1) + c * tc)
        c_ref[0, :, c * tc:(c + 1) * tc] = functools.reduce(lambda a, b: a + b, [_dotf(p, upto) for p in parts])


def _fox_logf(f_t, bias_col):
    B, H, T = f_t.shape
    spec = pl.BlockSpec((1, H, T), lambda b: (b, 0, 0))
    return pl.pallas_call(
        functools.partial(_logf_kernel, T=T, tc=min(256, T)), grid=(B,),
        in_specs=[spec, pl.BlockSpec((H, 1), lambda b: (0, 0))],
        out_specs=[spec, spec], out_shape=[jax.ShapeDtypeStruct((B, H, T), F32)] * 2,
        compiler_params=_cp("parallel"), name="fox_logf_cumsum",
    )(f_t, bias_col)


def _fox_prompt_kernel(q_ref, k_ref, v_ref, cc_ref, cr_ref, o_ref, *, tq, tk, T, passes):
    hp = pl.program_id(1)
    i = pl.program_id(2)
    t0 = i * tq
    dist = _iota((tq, tk), 0) + t0 - _iota((tq, tk), 1)
    c_hi = (t0 + tq + tk - 1) // tk
    c_free = (t0 + 1) // tk
    nh = cc_ref.shape[2]

    def mask_fn(c, s):
        return jnp.where(dist >= c * tk, s, NEG_INF)

    streams = []
    for hh in range(LANES // HEAD_DIM):
        h = hp * (LANES // HEAD_DIM) + hh
        lane = hh * HEAD_DIM
        c_col = jnp.sum(jnp.where(_iota((tq, nh), 1) == h, cc_ref[0], 0.0), axis=-1, keepdims=True)

        def bias_fn(c, h=h, c_col=c_col):
            off = pl.multiple_of(c * tk, tk)
            return c_col - cr_ref[0, pl.ds(h, 1), pl.ds(off, tk)]

        streams.append((q_ref[0, :, lane:lane + HEAD_DIM] * QK_SCALE, k_ref, v_ref, lane, lane, mask_fn, bias_fn))
    res = _flash_streams(streams, 0, c_hi, tk, passes, c_free)
    o_ref[0] = jnp.concatenate([o for o, _, _ in res], axis=1)


def _fox_prompt(q3, kv_t, c_col, c_row, passes, tq=512, tk=512):
    B, T, W = q3.shape
    hp = W // LANES
    nh = c_row.shape[1]
    return pl.pallas_call(
        functools.partial(_fox_prompt_kernel, tq=tq, tk=tk, T=T, passes=passes), grid=(B, hp, T // tq),
        in_specs=[pl.BlockSpec((1, tq, LANES), lambda b, h, i: (b, i, h)),
                  pl.BlockSpec((1, LANES, T), lambda b, h, i: (b, h, 0)),
                  pl.BlockSpec((1, LANES, T), lambda b, h, i: (b, hp + h, 0)),
                  pl.BlockSpec((1, tq, nh), lambda b, h, i: (b, i, 0)),
                  pl.BlockSpec((1, nh, T), lambda b, h, i: (b, 0, 0))],
        out_specs=pl.BlockSpec((1, tq, LANES), lambda b, h, i: (b, i, h)),
        out_shape=jax.ShapeDtypeStruct((B, T, W), F32),
        compiler_params=_cp("parallel", "parallel", "arbitrary"), name="fox_prompt",
    )(q3, kv_t, kv_t, c_col, c_row)


def _page_specs(n, block, layer, index_fn, lead=()):
    specs = []
    for pi in range(n):
        def imap(b, s, *pf, pi=pi):
            return (layer, index_fn(b, s, pi, *pf)) + lead + (0,) * (len(block) - 2 - len(lead))
        specs.append(pl.BlockSpec(block, imap))
    return specs


def _tdec_init(m_ref, l_ref, acc_ref):
    m_ref[...] = jnp.full(m_ref.shape, NEG_INF, F32)
    l_ref[...] = jnp.zeros(l_ref.shape, F32)
    acc_ref[...] = jnp.zeros(acc_ref.shape, F32)


def _tdec_update(kt, vt, qexp, bias, valid, m_ref, l_ref, acc_ref, r):
    G = kt.shape[0] // HEAD_DIM
    s = jnp.sum((kt * qexp).reshape(G, HEAD_DIM, LANES), axis=1)
    if bias is not None:
        s = s + bias
    if valid is not None:
        s = jnp.where(valid, s, NEG_INF)
    m_old = m_ref[r]
    m_new = jnp.maximum(m_old, jnp.max(s, axis=-1, keepdims=True))
    alpha = jnp.exp(m_old - m_new)
    e = jnp.exp(s - m_new)
    if valid is not None:
        e = jnp.where(valid, e, 0.0)
    l_ref[r] = alpha * l_ref[r] + e
    acc_ref[r] = acc_ref[r] * alpha[:, :, None] + vt.reshape(G, HEAD_DIM, LANES) * e[:, None, :]
    m_ref[r] = m_new


def _tdec_finish(l_ref, acc_ref, r):
    den = jnp.maximum(jnp.sum(l_ref[r], axis=-1, keepdims=True), TINY)
    o = jnp.sum(acc_ref[r], axis=-1, keepdims=True) / den[:, :, None]
    return jnp.broadcast_to(o, acc_ref.shape[1:])


def _tdec_scratch(R, G):
    return [pltpu.VMEM((R, G, 1), F32), pltpu.VMEM((R, G, LANES), F32), pltpu.VMEM((R, G, HEAD_DIM, LANES), F32)]


def _scmp_kernel(pt_ref, *refs, pp):
    pages, wlo_ref, whi_ref, o_ref = refs[:pp], refs[pp], refs[pp + 1], refs[pp + 2]
    rows = wlo_ref.shape[0]
    per = PAGE // A_CMP_STRIDE
    lo = jnp.zeros((rows, LANES), F32)
    hi = jnp.zeros((rows, LANES), F32)
    for pi in range(pp):
        x = pages[pi][...]
        m = _chunk_sum_matrix(pi * per)
        lo = lo + _dotx(x * wlo_ref[...], m, 2)
        hi = hi + _dotx(x * whi_ref[...], m, 2)
    o_ref[0, 0:rows, :] = lo
    o_ref[0, rows:2 * rows, :] = hi


def _nsa_compress_sample(cache_t, layer, page_table, wlo, whi):
    Bd, NP = page_table.shape
    rows = cache_t.shape[2]
    pp = LANES * A_CMP_STRIDE // PAGE
    grid_spec = pltpu.PrefetchScalarGridSpec(
        num_scalar_prefetch=1, grid=(Bd, NP // pp),
        in_specs=_page_specs(pp, (None, None, rows, PAGE), layer, lambda b, s, pi, pt: pt[b, s * pp + pi])
        + [pl.BlockSpec((rows, LANES), lambda b, s, pt: (0, 0))] * 2,
        out_specs=pl.BlockSpec((1, 2 * rows, LANES), lambda b, s, pt: (b, 0, s)))
    return pl.pallas_call(
        functools.partial(_scmp_kernel, pp=pp), grid_spec=grid_spec,
        out_shape=jax.ShapeDtypeStruct((Bd, 2 * rows, NP * PAGE // A_CMP_STRIDE), F32),
        compiler_params=_cp("parallel", "arbitrary"), name="nsa_compress_paged",
    )(page_table, *([cache_t] * pp), wlo, whi)


def _nsa_sample_cmp_kernel(lohi_ref, qt_ref, o_ref, idx_ref, *, P, ncols):
    rows = lohi_ref.shape[1] // 2
    gw = rows // 2
    nch = lohi_ref.shape[2]
    H = qt_ref.shape[1]
    R = H // A_KV
    lpad = -(-(P + 1) // A_SEL_LEN) * A_SEL_LEN
    ncb = lpad // A_CMP_STRIDE - 1
    nsb = lpad // A_SEL_LEN
    qb = P // A_SEL_LEN
    lohi = lohi_ref[0]
    kvb = lohi[0:rows] + pltpu.roll(lohi[rows:2 * rows], nch - 1, 1)
    qbd, keep = _group_block_diag(qt_ref[0] * QK_SCALE, H, A_KV)
    n_io = _iota((H, nch), 1)
    valid = (n_io * A_CMP_STRIDE + (A_CMP_LEN - 1) <= P) & (n_io < ncb)
    s = jnp.where(valid, _dot3(qbd, kvb[0:gw], NN), NEG_INF)
    m = jnp.max(s, axis=-1, keepdims=True)
    e = jnp.where(valid, jnp.exp(s - m), 0.0)
    p = e / jnp.maximum(jnp.sum(e, axis=-1, keepdims=True), TINY)
    o_ref[0] = _fold_groups(_dot3(p, kvb[gw:rows], NT), keep, A_KV)
    ratio = A_SEL_LEN // A_CMP_STRIDE
    back = A_CMP_LEN // A_CMP_STRIDE - 1
    mn, mj = _iota((nch, ncols), 0), _iota((nch, ncols), 1)
    imp_mat = _onehot((mn >= mj * ratio - back) & (mn <= mj * ratio + ratio - 1) & (mn < ncb))
    j_io = _iota((8, ncols), 1)
    jf = j_io.astype(F32)
    lane = _iota((8, LANES), 1)
    row = _iota((8, LANES), 0)
    out = jnp.zeros((8, LANES), F32)
    for g in range(A_KV):
        pg = jnp.sum(p[R * g:R * (g + 1)], axis=0, keepdims=True)
        imp = _dotx(jnp.broadcast_to(pg, (8, nch)), imp_mat, 3)
        score = jnp.where(j_io <= qb, jnp.where((j_io == 0) | (j_io == qb), A_FORCED, imp), -1.0)
        score = jnp.where(j_io < nsb, score, -2.0)
        for k in range(min(A_SEL_TOPK, nsb)):
            best = jnp.max(score, axis=-1, keepdims=True)
            pick = jnp.min(jnp.where(score == best, jf, 1e9), axis=-1, keepdims=True)
            out = jnp.where((lane == k) & (row == g), pick, out)
            score = jnp.where(jf == pick, -3.0, score)
    idx_ref[0] = out.astype(I32)


def _nsa_sample_cmp(lohi, qt3, P):
    Bd, rows2, nch = lohi.shape
    H, gw = qt3.shape[1], qt3.shape[2]
    nsb = -(-(P + 1) // A_SEL_LEN)
    ncols = -(-nsb // LANES) * LANES
    return pl.pallas_call(
        functools.partial(_nsa_sample_cmp_kernel, P=P, ncols=ncols), grid=(Bd,),
        in_specs=[pl.BlockSpec((1, rows2, nch), lambda b: (b, 0, 0)), pl.BlockSpec((1, H, gw), lambda b: (b, 0, 0))],
        out_specs=[pl.BlockSpec((1, H, HEAD_DIM), lambda b: (b, 0, 0)), pl.BlockSpec((1, 8, LANES), lambda b: (b, 0, 0))],
        out_shape=[jax.ShapeDtypeStruct((Bd, H, HEAD_DIM), F32), jax.ShapeDtypeStruct((Bd, 8, LANES), I32)],
        compiler_params=_cp("parallel"), name="nsa_sample_cmp_select",
    )(lohi, qt3)


def _nsa_sample_sel_kernel(pt_ref, ix_ref, *refs, P, nk, bps):
    nblk = A_KV * bps
    kblks, vblks = refs[0:nblk], refs[nblk:2 * nblk]
    q_ref, new_ref, o_ref, m_ref, l_ref, acc_ref = refs[2 * nblk:]
    b = pl.program_id(0)
    k = pl.program_id(1)
    R = q_ref.shape[2] // HEAD_DIM
    qb = P // A_SEL_LEN
    half_shift = A_SEL_LEN.bit_length() - 1
    lane = _iota((1, PAGE), 1)

    @pl.when(k == 0)
    def _():
        _tdec_init(m_ref, l_ref, acc_ref)

    for g in range(A_KV):
        for u in range(bps):
            j = ix_ref[b, g * nk + k * bps + u]
            valid = ((lane >> half_shift) == (j & (PAGE // A_SEL_LEN - 1))) & (j < qb)
            kt = jnp.concatenate([kblks[g * bps + u][...]] * R, axis=0)
            vt = jnp.concatenate([vblks[g * bps + u][...]] * R, axis=0)
            _tdec_update(kt, vt, q_ref[0, g] * QK_SCALE, None, valid, m_ref, l_ref, acc_ref, g)

    @pl.when(k == nk // bps - 1)
    def _():
        for g in range(A_KV):
            hit = ix_ref[b, g * nk] == qb
            for kk in range(1, nk):
                hit = hit | (ix_ref[b, g * nk + kk] == qb)
            kt = jnp.concatenate([new_ref[0, 0, g]] * R, axis=0)
            vt = jnp.concatenate([new_ref[0, 1, g]] * R, axis=0)
            _tdec_update(kt, vt, q_ref[0, g] * QK_SCALE, None, (lane == 0) & hit, m_ref, l_ref, acc_ref, g)
            o_ref[0, g] = _tdec_finish(l_ref, acc_ref, g)


def _nsa_sample_sel(cache_t6, layer, page_table, idx_flat, qexp, new_cols, P):
    Bd, G, rw = qexp.shape[0], qexp.shape[1], qexp.shape[2]
    R = rw // HEAD_DIM
    nk = idx_flat.shape[1] // A_KV
    per = PAGE // A_SEL_LEN
    last = P // A_SEL_LEN - 1

    bps = math.gcd(nk, 4)

    def blk_index(c, g, u):
        def imap(b, k, pt, ix):
            j = jnp.minimum(ix[b, g * nk + k * bps + u], last)
            return (layer, pt[b, j // per], c, g, 0, 0)
        return imap

    per_b = lambda shape: pl.BlockSpec((1,) + shape, lambda b, k, pt, ix: (b,) + (0,) * len(shape))
    blk = (None, None, None, None, HEAD_DIM, PAGE)
    grid_spec = pltpu.PrefetchScalarGridSpec(
        num_scalar_prefetch=2, grid=(Bd, nk // bps),
        in_specs=[pl.BlockSpec(blk, blk_index(c, g, u)) for c in range(2) for g in range(A_KV) for u in range(bps)]
        + [per_b((G, rw, LANES)), per_b((2, G, HEAD_DIM, LANES))],
        out_specs=per_b((G, R, HEAD_DIM, LANES)),
        scratch_shapes=_tdec_scratch(G, R))
    return pl.pallas_call(
        functools.partial(_nsa_sample_sel_kernel, P=P, nk=nk, bps=bps), grid_spec=grid_spec,
        out_shape=jax.ShapeDtypeStruct((Bd, G, R, HEAD_DIM, LANES), F32),
        compiler_params=_cp("parallel", "arbitrary"), name="nsa_sample_selected",
    )(page_table, idx_flat, *([cache_t6] * (2 * A_KV * bps)), qexp, new_cols)


def _nsa_merge_kernel(gl_ref, ocmp_ref, osel_ref, owin_ref, o_ref):
    gates = _sigmoid(gl_ref[...])
    o_ref[...] = gates[:, :, 0:1] * ocmp_ref[...] + gates[:, :, 1:2] * osel_ref[...] + gates[:, :, 2:3] * owin_ref[...]


def _nsa_sample_merge(gl3, ocmp3, osel3, owin3):
    whole = lambda a: pl.BlockSpec(a.shape, lambda i: (0,) * a.ndim)
    return pl.pallas_call(
        _nsa_merge_kernel, grid=(1,),
        in_specs=[whole(gl3), whole(ocmp3), whole(osel3), whole(owin3)], out_specs=whole(ocmp3),
        out_shape=jax.ShapeDtypeStruct(ocmp3.shape, F32),
        compiler_params=_cp("arbitrary"), name="nsa_sample_merge",
    )(gl3, ocmp3, osel3, owin3)


def _slide_lanes(buf, new_cols):
    W = buf.shape[-1]
    axis = buf.ndim - 1
    new = jnp.concatenate([new_cols] * (W // LANES), axis=axis)
    return jnp.where(_iota(buf.shape, axis) == W - 1, new, pltpu.roll(buf, W - 1, axis))


def _nsa_sample_win_kernel(buf_ref, new_ref, qt_ref, nbuf_ref, o_ref):
    rows = buf_ref.shape[1]
    gw = rows // 2
    H = qt_ref.shape[1]
    nb = _slide_lanes(buf_ref[0], new_ref[0])
    nbuf_ref[0] = nb
    qbd, keep = _group_block_diag(qt_ref[0] * QK_SCALE, H, A_KV)
    s = _dot3(qbd, nb[0:gw], NN)
    e = jnp.exp(s - jnp.max(s, axis=-1, keepdims=True))
    p = e / jnp.maximum(jnp.sum(e, axis=-1, keepdims=True), TINY)
    o_ref[0] = _fold_groups(_dot3(p, nb[gw:rows], NT), keep, A_KV)


def _nsa_sample_win(buf_t, new_cols, qt3):
    Bd, rows, W = buf_t.shape
    H, gw = qt3.shape[1], qt3.shape[2]
    return pl.pallas_call(
        _nsa_sample_win_kernel, grid=(Bd,),
        in_specs=[pl.BlockSpec((1, rows, W), lambda b: (b, 0, 0)), pl.BlockSpec((1, rows, LANES), lambda b: (b, 0, 0)),
                  pl.BlockSpec((1, H, gw), lambda b: (b, 0, 0))],
        out_specs=[pl.BlockSpec((1, rows, W), lambda b: (b, 0, 0)), pl.BlockSpec((1, H, HEAD_DIM), lambda b: (b, 0, 0))],
        out_shape=[jax.ShapeDtypeStruct((Bd, rows, W), F32), jax.ShapeDtypeStruct((Bd, H, HEAD_DIM), F32)],
        compiler_params=_cp("parallel"), name="nsa_sample_window",
    )(buf_t, new_cols, qt3)


def _dsa_idx_kernel(pt_ref, *refs, pp):
    pages, qi_ref, wi_ref, o_ref = refs[:pp], refs[pp], refs[pp + 1], refs[pp + 2]
    wi = wi_ref[0] * (B_IDX_HEADS * B_IDX_DIM) ** -0.5
    qi = qi_ref[0]
    for pi in range(pp):
        logits = _dot3(qi, pages[pi][...], NN)
        o_ref[0, pi] = jnp.sum(wi * jnp.maximum(logits, 0.0), axis=0, keepdims=True)


def _dsa_sample_scores(kidx_t, layer, page_table, qi3, wi3, pp=8):
    Bd, NP = page_table.shape
    grid_spec = pltpu.PrefetchScalarGridSpec(
        num_scalar_prefetch=1, grid=(Bd, NP // pp),
        in_specs=_page_specs(pp, (None, None, B_IDX_DIM, PAGE), layer, lambda b, s, pi, pt: pt[b, s * pp + pi])
        + [pl.BlockSpec((1, B_IDX_HEADS, B_IDX_DIM), lambda b, s, pt: (b, 0, 0)),
           pl.BlockSpec((1, B_IDX_HEADS, 1), lambda b, s, pt: (b, 0, 0))],
        out_specs=pl.BlockSpec((1, pp, 1, PAGE), lambda b, s, pt: (b, s, 0, 0)))
    return pl.pallas_call(
        functools.partial(_dsa_idx_kernel, pp=pp), grid_spec=grid_spec,
        out_shape=jax.ShapeDtypeStruct((Bd, NP, 1, PAGE), F32),
        compiler_params=_cp("parallel", "arbitrary"), name="dsa_sample_indexer",
    )(page_table, *([kidx_t] * pp), qi3, wi3)


def _dsa_select_kernel(sc_ref, qi_ref, wi_ref, kin_ref, o_ref, *, topk):
    Bd, P = sc_ref.shape
    wi = wi_ref[...] * (B_IDX_HEADS * B_IDX_DIM) ** -0.5
    logit = jnp.sum(qi_ref[...] * kin_ref[...], axis=-1, keepdims=True)
    s_new = jnp.sum(wi * jnp.maximum(logit, 0.0), axis=1)
    tail = jnp.where(_iota((Bd, LANES), 1) == 0, s_new, -jnp.inf)
    full = jnp.concatenate([sc_ref[...], tail], axis=1)
    o_ref[...] = jnp.where(_topk_mask(_sortable_key(full), topk), 1.0, 0.0)


def _dsa_sample_select(scores2, qi3, wi3, kinew3, topk):
    Bd, P = scores2.shape
    whole = lambda shape: pl.BlockSpec(shape, lambda i: (0,) * len(shape))
    return pl.pallas_call(
        functools.partial(_dsa_select_kernel, topk=topk), grid=(1,),
        in_specs=[whole(scores2.shape), whole(qi3.shape), whole(wi3.shape), whole(kinew3.shape)],
        out_specs=whole((Bd, P + LANES)), out_shape=jax.ShapeDtypeStruct((Bd, P + LANES), F32),
        compiler_params=_cp("arbitrary"), name="dsa_sample_topk",
    )(scores2, qi3, wi3, kinew3)


def _dsa_sample_attn_kernel(pt_ref, *refs, pp, nsteps):
    pages = refs[:pp]
    mask_ref, q_ref, new_ref, mnew_ref, o_ref, m_ref, l_ref, acc_ref = refs[pp:]
    s_id = pl.program_id(1)
    R = q_ref.shape[1]
    gw = new_ref.shape[1] // 2

    @pl.when(s_id == 0)
    def _():
        _tdec_init(m_ref, l_ref, acc_ref)

    for pi in range(pp):
        kv = pages[pi][...]
        valid = mask_ref[0, pi] > 0.5
        for r in range(R):
            _tdec_update(kv[0:gw], kv[gw:2 * gw], q_ref[0, r] * QK_SCALE, None, valid, m_ref, l_ref, acc_ref, r)

    @pl.when(s_id == nsteps - 1)
    def _():
        new = new_ref[0]
        valid = (_iota((1, LANES), 1) == 0) & (mnew_ref[0] > 0.5)
        for r in range(R):
            _tdec_update(new[0:gw], new[gw:2 * gw], q_ref[0, r] * QK_SCALE, None, valid, m_ref, l_ref, acc_ref, r)
            o_ref[0, r] = _tdec_finish(l_ref, acc_ref, r)


def _dsa_sample_attn(kv_t, layer, page_table, mask4, qexp, new_cols, mnew3, pp=16):
    Bd, NP = page_table.shape
    rows = kv_t.shape[2]
    R, gw = qexp.shape[1], qexp.shape[2]
    G = gw // HEAD_DIM
    nsteps = NP // pp
    per_b = lambda shape: pl.BlockSpec((1,) + shape, lambda b, s, pt: (b,) + (0,) * len(shape))
    grid_spec = pltpu.PrefetchScalarGridSpec(
        num_scalar_prefetch=1, grid=(Bd, nsteps),
        in_specs=_page_specs(pp, (None, None, rows, PAGE), layer, lambda b, s, pi, pt: pt[b, s * pp + pi])
        + [pl.BlockSpec((1, pp, 1, PAGE), lambda b, s, pt: (b, s, 0, 0)), per_b((R, gw, LANES)), per_b((rows, LANES)),
           per_b((1, LANES))],
        out_specs=per_b((R, G, HEAD_DIM, LANES)),
        scratch_shapes=_tdec_scratch(R, G))
    return pl.pallas_call(
        functools.partial(_dsa_sample_attn_kernel, pp=pp, nsteps=nsteps), grid_spec=grid_spec,
        out_shape=jax.ShapeDtypeStruct((Bd, R, G, HEAD_DIM, LANES), F32),
        compiler_params=_cp("parallel", "arbitrary"), name="dsa_sample_attention",
    )(page_table, *([kv_t] * pp), mask4, qexp, new_cols, mnew3)


def _dil_sample_kernel(*refs, P):
    ng = len(C_GROUPS)
    buf_refs, new_refs, q_refs = refs[0:ng], refs[ng:2 * ng], refs[2 * ng:3 * ng]
    nbuf_refs, o_ref = refs[3 * ng:4 * ng], refs[4 * ng]
    res = []
    for (w, r), buf_ref, new_ref, q_ref, nbuf_ref in zip(C_GROUPS, buf_refs, new_refs, q_refs, nbuf_refs):
        buf = buf_ref[...]
        new = new_ref[...]
        W = buf.shape[2]
        nbuf_ref[...] = _slide_lanes(buf, new)
        q = q_ref[...] * QK_SCALE
        qw = jnp.concatenate([q] * (W // LANES), axis=1)
        s_old = jnp.sum(buf[0] * qw, axis=0, keepdims=True)
        dist = W - _iota((1, W), 1)
        valid = ((dist & (r - 1)) == 0) & (dist <= P)
        s_old = jnp.where(valid, s_old, NEG_INF)
        s_new = jnp.sum(new[0] * q, axis=0, keepdims=True)
        m = jnp.maximum(jnp.max(s_old, axis=-1, keepdims=True), s_new)
        e_old = jnp.where(valid, jnp.exp(s_old - m[:, 0:1]), 0.0)
        e_new = jnp.exp(s_new - m)
        den = jnp.sum(e_old, axis=-1, keepdims=True) + e_new
        o = (jnp.sum(buf[1] * e_old, axis=-1, keepdims=True) + new[1] * e_new) / jnp.maximum(den, TINY)
        res.append((o, m, den))
    m_all = functools.reduce(jnp.maximum, [m for _, m, _ in res])
    wts = [den * jnp.exp(m - m_all) for _, m, den in res]
    tot = functools.reduce(lambda a, b: a + b, wts)
    o_ref[...] = functools.reduce(lambda a, b: a + b, [(wt / tot) * o for wt, (o, _, _) in zip(wts, res)])


def _dil_sample(bufs_t, layer, news_cols, qs_cols, P):
    Bd, H = qs_cols[0].shape[0], qs_cols[0].shape[1]
    in_specs, out_specs, out_shape = [], [], []
    for buf in bufs_t:
        W = buf.shape[5]
        in_specs.append(pl.BlockSpec((None, None, 2, None, HEAD_DIM, W), lambda b, h: (layer, b, 0, h, 0, 0)))
        out_specs.append(pl.BlockSpec((None, 2, None, HEAD_DIM, W), lambda b, h: (b, 0, h, 0, 0)))
        out_shape.append(jax.ShapeDtypeStruct((Bd, 2, H, HEAD_DIM, W), F32))
    in_specs += [pl.BlockSpec((None, 2, None, HEAD_DIM, LANES), lambda b, h: (b, 0, h, 0, 0))] * len(bufs_t)
    in_specs += [pl.BlockSpec((None, None, HEAD_DIM, LANES), lambda b, h: (b, h, 0, 0))] * len(bufs_t)
    out_specs.append(pl.BlockSpec((None, None, HEAD_DIM, LANES), lambda b, h: (b, h, 0, 0)))
    out_shape.append(jax.ShapeDtypeStruct((Bd, H, HEAD_DIM, LANES), F32))
    return pl.pallas_call(
        functools.partial(_dil_sample_kernel, P=P), grid=(Bd, H),
        in_specs=in_specs, out_specs=out_specs, out_shape=out_shape,
        compiler_params=_cp("parallel", "parallel"), name="dilated_sample",
    )(*bufs_t, *news_cols, *qs_cols)


def _fox_sample_kernel(pt_ref, *refs, pp, nsteps):
    pages, lfs = refs[:pp], refs[pp:2 * pp]
    q_ref, new_ref, lfnew_ref, o_ref, m_ref, l_ref, acc_ref, carry_ref = refs[2 * pp:]
    s_id = pl.program_id(1)
    L = new_ref.shape[1] // 2
    q = q_ref[0, 0] * QK_SCALE

    @pl.when(s_id == 0)
    def _():
        _tdec_init(m_ref, l_ref, acc_ref)
        new = new_ref[0]
        _tdec_update(new[0:L], new[L:2 * L], q, None, _iota((1, LANES), 1) == 0, m_ref, l_ref, acc_ref, 0)
        carry_ref[...] = lfnew_ref[0]

    later = _onehot(_iota((PAGE, PAGE), 0) > _iota((PAGE, PAGE), 1))
    for pi in range(pp):
        kv = pages[pi][...]
        lf = lfs[pi][...]
        carry = carry_ref[...]
        bias = _dotx(lf, later, 3) + carry
        carry_ref[...] = carry + jnp.sum(lf, axis=-1, keepdims=True)
        _tdec_update(kv[0:L], kv[L:2 * L], q, bias, None, m_ref, l_ref, acc_ref, 0)

    @pl.when(s_id == nsteps - 1)
    def _():
        o_ref[0, 0] = _tdec_finish(l_ref, acc_ref, 0)


def _fox_sample(kv_t, lf_t, layer, page_table, qexp, new_cols, lfnew3, pp=8):
    Bd, NP = page_table.shape
    rows = kv_t.shape[2]
    H = lf_t.shape[2]
    nsteps = NP // pp
    rev = lambda b, s, pi, pt: pt[b, NP - 1 - (s * pp + pi)]
    per_b = lambda shape: pl.BlockSpec((1,) + shape, lambda b, s, pt: (b,) + (0,) * len(shape))
    grid_spec = pltpu.PrefetchScalarGridSpec(
        num_scalar_prefetch=1, grid=(Bd, nsteps),
        in_specs=_page_specs(pp, (None, None, rows, PAGE), layer, rev) + _page_specs(pp, (None, None, H, PAGE), layer, rev)
        + [per_b((1, rows // 2, LANES)), per_b((rows, LANES)), per_b((H, LANES))],
        out_specs=per_b((1, H, HEAD_DIM, LANES)),
        scratch_shapes=_tdec_scratch(1, H) + [pltpu.VMEM((H, LANES), F32)])
    return pl.pallas_call(
        functools.partial(_fox_sample_kernel, pp=pp, nsteps=nsteps), grid_spec=grid_spec,
        out_shape=jax.ShapeDtypeStruct((Bd, 1, H, HEAD_DIM, LANES), F32),
        compiler_params=_cp("parallel", "arbitrary"), name="fox_sample",
    )(page_table, *([kv_t] * pp), *([lf_t] * pp), qexp, new_cols, lfnew3)


def _cols(x, lead):
    Bd = x.shape[0]
    return jnp.broadcast_to(x.reshape((Bd,) + lead + (1,)), (Bd,) + lead + (LANES,))


def _rows_minor(cache):
    n = cache.ndim
    return jnp.moveaxis(cache, n - 4, n - 1)


def _rows_major(x):
    n = x.ndim
    return jnp.moveaxis(x, n - 1, n - 4)


def kernel(x_prompt, x_sample, cache_a_cmp, cache_a_sel, cache_a_win, cache_b_kv, cache_b_kidx, cache_c_win0, cache_c_win1, cache_c_win2, cache_d_kv, cache_d_logf, page_table, c_prompt, c_sample, ada_w, ada_b, norm_g, final_g, a_w_in, a_w_out, a_cmp_wk, a_cmp_wv, b_w_in, b_w_out, c_w_in, c_w_out, d_w_in, d_w_out, d_f_bias):
    B, T, D = x_prompt.shape
    Bd = x_sample.shape[0]
    depth = ada_w.shape[0]
    NP = page_table.shape[1]
    P = NP * PAGE
    pool = cache_b_kidx.shape[1]
    assert x_sample.shape[1] == 1 and D == 1024 and P % A_SEL_LEN == 0
    assert cache_a_win.shape[2] == A_WINDOW
    assert all(b.shape[2] == w for b, (w, _) in zip((cache_c_win0, cache_c_win1, cache_c_win2), C_GROUPS))
    H = D // HEAD_DIM
    R = H // A_KV
    tm_p = 512
    tpb = T // tm_p

    mod = _mod_all(jnp.concatenate([c_prompt, c_sample], axis=0), ada_w, ada_b)
    pos_p = jnp.arange(T, dtype=I32)
    pos_s = jnp.full((Bd,), P, I32)
    tab_p, ttab_p = _rope_tables(pos_p), _rope_tables_t(pos_p)
    tab_s = _rope_tables(pos_s)
    xp = x_prompt.reshape(B * T, D)
    xs = x_sample.reshape(Bd, D)
    c_bufs = (cache_c_win0, cache_c_win1, cache_c_win2)
    st = {}
    put = lambda name, val: st.setdefault(name, []).append(val)
    rows_of = lambda t, c, g: _rows_major(t.reshape(t.shape[0], c, g, HEAD_DIM, t.shape[2]))
    yp = ys = None
    for i in range(depth):
        kind, li = i % 4, i // 4
        shift_p, scale_p, gate_p = [mod[i, :B, j * D:(j + 1) * D].reshape(B, 1, D) for j in range(3)]
        shift_s, scale_s, gate_s = [mod[i, B:, j * D:(j + 1) * D].reshape(1, Bd, D) for j in range(3)]
        proj_p = lambda w, segs: _proj(xp, scale_p, shift_p, norm_g[i], w, segs, tab_p, ttab_p, tpb, tm_p, PASSES_PROMPT_PROJ)
        proj_s = lambda w, segs: _proj(xs, scale_s, shift_s, norm_g[i], w, segs, tab_s, None, 1, Bd, PASSES_SELECTIVE)
        if kind == 0:
            w = a_w_in[li]
            gw = A_KV * HEAD_DIM
            q, qr, gl, z, cmp_t, sel_t, win_t = proj_p(w, [
                ("rm", 0, D, ("n", "r")), ("rm", D + 6 * gw, 3 * H, ("n",)), ("rm", D + 6 * gw + 3 * H, D, ("n",)),
                ("t", D, 2 * gw, "t"), ("t", D + 2 * gw, 2 * gw, "trk"), ("t", D + 4 * gw, 2 * gw, "trk")])
            wlo, whi = _cmp_taps(a_cmp_wk[li], a_cmp_wv[li], A_KV)
            kvb = _nsa_compress_prompt(cmp_t, wlo, whi)
            o = _nsa_prompt(q.reshape(B, T, D), qr.reshape(B, T, D), kvb, sel_t, win_t, gl.reshape(B, T, LANES),
                            PASSES_NSA_FLASH)
            op, zp, wo = o.reshape(B * T, D), z, a_w_out[li]
            put("a_cmp_p", rows_of(cmp_t, 2, A_KV))
            put("a_sel_p", rows_of(sel_t, 2, A_KV))
            put("a_win_p", rows_of(win_t[:, :, T - min(A_WINDOW, T):], 2, A_KV))

            q, qr, cmp_, sel, win, gl, z = proj_s(w, [
                ("rm", 0, D, ("n", "r")), ("rm", D, 2 * gw, ("n",)), ("rm", D + 2 * gw, 2 * gw, ("rk",)),
                ("rm", D + 4 * gw, 2 * gw, ("rk",)), ("rm", D + 6 * gw, 3 * H, ("n",)), ("rm", D + 6 * gw + 3 * H, D, ("n",))])
            cmp_t6 = _rows_minor(cache_a_cmp)
            lohi = _nsa_compress_sample(cmp_t6.reshape(cmp_t6.shape[0], pool, 2 * gw, PAGE), li, page_table, wlo, whi)
            tile_g = lambda t: jnp.tile(t.reshape(Bd, H, HEAD_DIM), (1, 1, A_KV))
            ocmp, idx = _nsa_sample_cmp(lohi, tile_g(q), P)
            win_t6 = _rows_minor(cache_a_win)
            nwin, owin = _nsa_sample_win(win_t6[li].reshape(Bd, 2 * gw, A_WINDOW), _cols(win, (2 * gw,)), tile_g(qr))
            nk = min(A_SEL_TOPK, -(-(P + 1) // A_SEL_LEN))
            osel = _nsa_sample_sel(_rows_minor(cache_a_sel), li, page_table, idx[:, :A_KV, :nk].reshape(Bd, A_KV * nk),
                                   _cols(qr, (A_KV, R * HEAD_DIM)), _cols(sel, (2, A_KV, HEAD_DIM)), P)
            o = _nsa_sample_merge(gl[:, :3 * H].reshape(Bd, H, 3), ocmp, osel[..., 0].reshape(Bd, H, HEAD_DIM), owin)
            os_, zs = o.reshape(Bd, D), z
            put("a_cmp_s", cmp_.reshape(Bd, 1, 2, A_KV, HEAD_DIM))
            put("a_sel_s", sel.reshape(Bd, 1, 2, A_KV, HEAD_DIM))
            put("a_win_s", _rows_major(nwin.reshape(Bd, 2, A_KV, HEAD_DIM, A_WINDOW)))
        elif kind == 1:
            w = b_w_in[li]
            gw = A_KV * HEAD_DIM
            c_qi = D + 2 * gw
            c_ki = c_qi + B_IDX_HEADS * B_IDX_DIM
            c_wi = c_ki + B_IDX_DIM
            c_z = c_wi + B_IDX_HEADS
            q, qi, wi, z, kv_t, ki_t = proj_p(w, [
                ("rm", 0, D, ("r",)), ("rm", c_qi, c_ki - c_qi, ("r",)), ("rm", c_wi, B_IDX_HEADS, ("n",)), ("rm", c_z, D, ("n",)),
                ("t", D, 2 * gw, "trk"), ("t", c_ki, B_IDX_DIM, "tr")])
            o = _dsa_prompt(q.reshape(B, T, D), qi.reshape(B, T, c_ki - c_qi), wi.reshape(B, T, LANES), kv_t, ki_t,
                            min(B_TOPK_MAX, T // 4), PASSES_SMOOTH, PASSES_PROMPT_INDEXER)
            op, zp, wo = o.reshape(B * T, D), z, b_w_out[li]
            put("b_kv_p", rows_of(kv_t, 2, A_KV))
            put("b_kidx_p", jnp.swapaxes(ki_t, 1, 2))

            q, kv, qi, ki, wi, z = proj_s(w, [
                ("rm", 0, D, ("r",)), ("rm", D, 2 * gw, ("rk",)), ("rm", c_qi, c_ki - c_qi, ("r",)), ("rm", c_ki, B_IDX_DIM, ("r",)),
                ("rm", c_wi, B_IDX_HEADS, ("n",)), ("rm", c_z, D, ("n",))])
            qi3 = qi.reshape(Bd, B_IDX_HEADS, B_IDX_DIM)
            wi3 = wi[:, :B_IDX_HEADS].reshape(Bd, B_IDX_HEADS, 1)
            ki_new = ki[:, :B_IDX_DIM]
            sc = _dsa_sample_scores(jnp.swapaxes(cache_b_kidx, 2, 3), li, page_table, qi3, wi3)
            mask = _dsa_sample_select(sc.reshape(Bd, P), qi3, wi3, ki_new.reshape(Bd, 1, B_IDX_DIM),
                                      min(B_TOPK_MAX, (P + 1) // 4))
            kv_t6 = _rows_minor(cache_b_kv)
            qexp = _cols(q.reshape(Bd, A_KV, R, HEAD_DIM).transpose(0, 2, 1, 3).reshape(Bd, R, gw), (R, gw))
            o = _dsa_sample_attn(kv_t6.reshape(kv_t6.shape[0], pool, 2 * gw, PAGE), li, page_table,
                                 mask[:, :P].reshape(Bd, NP, 1, PAGE), qexp, _cols(kv, (2 * gw,)),
                                 jnp.broadcast_to(mask[:, P:P + 1], (Bd, LANES)).reshape(Bd, 1, LANES))
            os_, zs = o[..., 0].transpose(0, 2, 1, 3).reshape(Bd, D), z
            put("b_kv_s", kv.reshape(Bd, 1, 2, A_KV, HEAD_DIM))
            put("b_kidx_s", ki_new.reshape(Bd, 1, B_IDX_DIM))
        elif kind == 2:
            w = c_w_in[li]
            Wd = C_HEADS * HEAD_DIM
            ng = len(C_GROUPS)
            res = proj_p(w, [("rm", 3 * g * Wd, Wd, ("r",)) for g in range(ng)] + [("rm", 3 * ng * Wd, Wd, ("n",))]
                         + [("t", (3 * g + 1) * Wd, 2 * Wd, "trk") for g in range(ng)])
            qs, z, kvs_t = res[0:ng], res[ng], res[ng + 1:]
            o = _dil_prompt([t.reshape(B, T, Wd) for t in qs], kvs_t, PASSES_SMOOTH)
            op, zp, wo = o.reshape(B * T, Wd), z, c_w_out[li]
            for g, (wg, _) in enumerate(C_GROUPS):
                put("c_win%d_p" % g, rows_of(kvs_t[g][:, :, T - min(wg, T):], 2, C_HEADS))

            res = proj_s(w, [seg for g in range(ng) for seg in (("rm", 3 * g * Wd, Wd, ("r",)), ("rm", (3 * g + 1) * Wd, 2 * Wd, ("rk",)))]
                         + [("rm", 3 * ng * Wd, Wd, ("n",))])
            qs, kvs, z = res[0:2 * ng:2], res[1:2 * ng:2], res[2 * ng]
            outs = _dil_sample([_rows_minor(b) for b in c_bufs], li, [_cols(t, (2, C_HEADS, HEAD_DIM)) for t in kvs],
                               [_cols(t, (C_HEADS, HEAD_DIM)) for t in qs], P)
            os_, zs = outs[ng][..., 0].reshape(Bd, Wd), z
            for g in range(ng):
                put("c_win%d_s" % g, _rows_major(outs[g]))
        else:
            w = d_w_in[li]
            q, z, kv_t, f_t = proj_p(w, [("rm", 0, D, ("n",)), ("rm", 3 * D + H, D, ("n",)), ("t", D, 2 * D, "t"), ("t", 3 * D, H, "t")])
            fb = d_f_bias[li].reshape(H, 1)
            lf_t, c_t = _fox_logf(f_t, fb)
            o = _fox_prompt(q.reshape(B, T, D), kv_t, jnp.swapaxes(c_t, 1, 2), c_t, PASSES_SMOOTH)
            op, zp, wo = o.reshape(B * T, D), z, d_w_out[li]
            put("d_kv_p", rows_of(kv_t, 2, H))
            put("d_logf_p", jnp.swapaxes(lf_t, 1, 2))

            q, kv, f, z = proj_s(w, [("rm", 0, D, ("n",)), ("rm", D, 2 * D, ("n",)), ("rm", 3 * D, H, ("n",)), ("rm", 3 * D + H, D, ("n",))])
            lf = _fox_logf(f[:, :H].T.reshape(1, H, Bd), fb)[0][0].T
            kv_t6 = _rows_minor(cache_d_kv)
            o = _fox_sample(kv_t6.reshape(kv_t6.shape[0], pool, 2 * D, PAGE), jnp.swapaxes(cache_d_logf, 2, 3), li, page_table,
                            _cols(q, (1, D)), _cols(kv, (2 * D,)), _cols(lf, (H,)))
            os_, zs = o[..., 0].reshape(Bd, D), z
            put("d_kv_s", kv.reshape(Bd, 1, 2, H, HEAD_DIM))
            put("d_logf_s", lf.reshape(Bd, 1, H))
        fg = final_g if i == depth - 1 else None
        xp, yp = _gated_out(op, zp, wo, xp, gate_p, tpb, tm_p, PASSES_PROMPT_PROJ, fg)
        xs, ys = _gated_out(os_, zs, wo, xs, gate_s, 1, Bd, PASSES_SELECTIVE, fg)
    ns = {n: jnp.stack(v) for n, v in st.items()}
    names = ("a_cmp_p", "a_cmp_s", "a_sel_p", "a_sel_s", "a_win_p", "a_win_s", "b_kv_p", "b_kv_s", "b_kidx_p", "b_kidx_s",
             "c_win0_p", "c_win0_s", "c_win1_p", "c_win1_s", "c_win2_p", "c_win2_s", "d_kv_p", "d_kv_s", "d_logf_p", "d_logf_s")
    return (yp.reshape(B, T, D), ys.reshape(Bd, 1, D)) + tuple(ns[n] for n in names)
```
